```python
import jax, jax.numpy as jnp
from jax import lax
import numpy as np

D_MODEL = 2048
BATCH = 4
SEQ = 2048
DEPTH = 1
DEC_BATCH = 128
DEC_SEQ = 8
PAST_LEN = 16384
PAGE_SIZE = 128

MIX_WIDTH = D_MODEL
GLA_WIDTH = MIX_WIDTH // 2
GMLP_WIDTH = MIX_WIDTH - GLA_WIDTH
GLA_HEADS = 4
GLA_DV = GLA_WIDTH // GLA_HEADS
GLA_DK = GLA_DV // 2
GLA_KEY_WIDTH = GLA_HEADS * GLA_DK
GLA_GATE_RANK = 16
GLA_TAU = 16.0
GLA_CHUNK = 64
GMLP_GROUPS = 8
GMLP_GROUP_DIM = GMLP_WIDTH // GMLP_GROUPS
GMLP_CHUNK = 128
D_FF = -(-8 * D_MODEL // (3 * 256)) * 256
PLE_DIM = 256
EPS = 1e-6
IN_SIZES = (GLA_KEY_WIDTH, GLA_KEY_WIDTH, GLA_WIDTH, GLA_WIDTH, GLA_GATE_RANK, GMLP_WIDTH, GMLP_WIDTH)
IN_WIDTH = sum(IN_SIZES)

kernel_name = 'hybrid_gla_gmlp_step'


def rmsnorm(x, g):
    xf = x.astype(jnp.float32)
    y = xf * lax.rsqrt(jnp.mean(xf * xf, axis=-1, keepdims=True) + EPS)
    return (y * g.astype(jnp.float32)).astype(x.dtype)


def layernorm(x, g, b):
    xf = x.astype(jnp.float32)
    mu = jnp.mean(xf, axis=-1, keepdims=True)
    xc = xf - mu
    y = xc * lax.rsqrt(jnp.mean(xc * xc, axis=-1, keepdims=True) + EPS)
    return (y * g.astype(jnp.float32) + b.astype(jnp.float32)).astype(x.dtype)


def split_cols(z, sizes):
    out, start = [], 0
    for s in sizes:
        out.append(z[..., start:start + s])
        start += s
    return out


def gla_recurrence(q, k, v, log_a, s0):
    f32 = jnp.float32
    Bn, T = q.shape[0], q.shape[1]
    C = min(GLA_CHUNK, T)
    n = -(-T // C)
    pad = n * C - T
    def prep(a):
        a = a.astype(f32)
        if pad:
            a = jnp.pad(a, ((0, 0), (0, pad), (0, 0), (0, 0)))
        return a.reshape(Bn, n, C, a.shape[2], a.shape[3]).transpose(1, 0, 3, 2, 4)
    qc, kc, vc, lc = prep(q), prep(k), prep(v), prep(log_a)
    mask = jnp.tril(jnp.ones((C, C), dtype=bool))
    mid = C // 2

    def step(S, inp):
        qb, kb, vb, lb = inp
        b = jnp.cumsum(lb, axis=2)
        b_mid = b[:, :, mid:mid + 1]
        b_last = b[:, :, -1:]
        o_inter = jnp.einsum('bhtk,bhkv->bhtv', qb * jnp.exp(b), S)
        att = jnp.einsum('bhtk,bhsk->bhts', qb * jnp.exp(b - b_mid), kb * jnp.exp(b_mid - b))
        att = jnp.where(mask, att, 0.0)
        o = o_inter + jnp.einsum('bhts,bhsv->bhtv', att, vb)
        S_new = jnp.exp(b_last)[:, :, 0, :, None] * S + jnp.einsum(
            'bhsk,bhsv->bhkv', kb * jnp.exp(b_last - b), vb)
        return S_new, o

    S, o = lax.scan(step, s0.astype(f32), (qc, kc, vc, lc))
    o = o.transpose(1, 0, 3, 2, 4).reshape(Bn, n * C, o.shape[2], o.shape[4])[:, :T]
    return o, S


def chunk_spatial_gate(u, vn, w_s, b_s):
    Bn, T = u.shape[0], u.shape[1]
    C = GMLP_CHUNK
    n = -(-T // C)
    pad = n * C - T
    vp = jnp.pad(vn, ((0, 0), (0, pad), (0, 0), (0, 0))) if pad else vn
    vp = vp.reshape(Bn, n, C, GMLP_GROUPS, GMLP_GROUP_DIM)
    w = jnp.where(jnp.tril(jnp.ones((C, C), dtype=bool))[None], w_s, 0.0).astype(vn.dtype)
    mixed = jnp.einsum('gts,bnsgd->bntgd', w, vp) + b_s.T.astype(vn.dtype)[None, None, :, :, None]
    mixed = mixed.reshape(Bn, n * C, GMLP_GROUPS, GMLP_GROUP_DIM)[:, :T]
    return u * mixed


def hybrid_layer(h, p_l, s0, g_mix, w_in, w_a2, b_a, g_gla_norm, g_gmlp_ln, b_gmlp_ln,
                 w_s, b_s, g_gmlp_out, w_out, g_ffn, w_ffn_in, w_ffn_out,
                 w_ple, g_ple, g_ple_gate, w_ple_gate):
    Bn, T = h.shape[0], h.shape[1]
    a = rmsnorm(h, g_mix)
    z = a @ w_in
    q, k, v, r, alow, u, vg = split_cols(z, IN_SIZES)
    q = q.reshape(Bn, T, GLA_HEADS, GLA_DK) * (GLA_DK ** -0.5)
    k = k.reshape(Bn, T, GLA_HEADS, GLA_DK)
    v = v.reshape(Bn, T, GLA_HEADS, GLA_DV)
    log_a = jax.nn.log_sigmoid((alow @ w_a2 + b_a).astype(jnp.float32)) / GLA_TAU
    log_a = log_a.reshape(Bn, T, GLA_HEADS, GLA_DK)
    o, s_new = gla_recurrence(q, k, v, log_a, s0)
    o = rmsnorm(o.astype(h.dtype), g_gla_norm)
    o = o.reshape(Bn, T, GLA_WIDTH) * jax.nn.silu(r)
    u = jax.nn.gelu(u)
    vn = layernorm(jax.nn.gelu(vg), g_gmlp_ln, b_gmlp_ln)
    m = chunk_spatial_gate(u.reshape(Bn, T, GMLP_GROUPS, GMLP_GROUP_DIM),
                           vn.reshape(Bn, T, GMLP_GROUPS, GMLP_GROUP_DIM), w_s, b_s)
    m = rmsnorm(m.reshape(Bn, T, GMLP_WIDTH), g_gmlp_out)
    h = h + jnp.concatenate([o, m], axis=-1) @ w_out
    f = rmsnorm(h, g_ffn)
    gate, up = split_cols(f @ w_ffn_in, (D_FF, D_FF))
    h = h + (jax.nn.silu(gate) * up) @ w_ffn_out
    pe = rmsnorm(p_l @ w_ple, g_ple)
    h = h + pe * jax.nn.sigmoid(rmsnorm(h, g_ple_gate) @ w_ple_gate)
    return h, s_new, vn


def setup_inputs(seed: int = 0) -> dict:
    key = jax.random.key(seed)
    ks = jax.random.split(key, 26)
    def nrm(k, shape, scale):
        return jax.random.normal(k, shape, jnp.float32) * scale
    def gain(k, shape):
        return 1.0 + nrm(k, shape, 0.02)
    return {
        'x_prompt': nrm(ks[0], (BATCH, SEQ, D_MODEL), 1.0),
        'x_sample': nrm(ks[1], (DEC_BATCH, DEC_SEQ, D_MODEL), 1.0),
        'state_gla': nrm(ks[2], (DEPTH, DEC_BATCH, GLA_HEADS, GLA_DK, GLA_DV), 0.3),
        'p_prompt': nrm(ks[3], (DEPTH, BATCH, SEQ, PLE_DIM), 1.0),
        'p_sample': nrm(ks[4], (DEPTH, DEC_BATCH, DEC_SEQ, PLE_DIM), 1.0),
        'g_mix': gain(ks[5], (DEPTH, D_MODEL)),
        'w_in': nrm(ks[6], (DEPTH, D_MODEL, IN_WIDTH), D_MODEL ** -0.5),
        'w_a2': nrm(ks[7], (DEPTH, GLA_GATE_RANK, GLA_KEY_WIDTH), GLA_GATE_RANK ** -0.5),
        'b_a': nrm(ks[8], (DEPTH, GLA_KEY_WIDTH), 0.1),
        'g_gla_norm': gain(ks[9], (DEPTH, GLA_HEADS, GLA_DV)),
        'g_gmlp_ln': gain(ks[10], (DEPTH, GMLP_WIDTH)),
        'b_gmlp_ln': nrm(ks[11], (DEPTH, GMLP_WIDTH), 0.02),
        'w_s': nrm(ks[12], (DEPTH, GMLP_GROUPS, GMLP_CHUNK, GMLP_CHUNK), 0.5 * GMLP_CHUNK ** -0.5),
        'b_s': 1.0 + nrm(ks[13], (DEPTH, GMLP_GROUPS, GMLP_CHUNK), 0.1),
        'g_gmlp_out': gain(ks[14], (DEPTH, GMLP_WIDTH)),
        'w_out': nrm(ks[15], (DEPTH, MIX_WIDTH, D_MODEL), MIX_WIDTH ** -0.5),
        'g_ffn': gain(ks[16], (DEPTH, D_MODEL)),
        'w_ffn_in': nrm(ks[17], (DEPTH, D_MODEL, 2 * D_FF), D_MODEL ** -0.5),
        'w_ffn_out': nrm(ks[18], (DEPTH, D_FF, D_MODEL), D_FF ** -0.5),
        'w_ple': nrm(ks[19], (DEPTH, PLE_DIM, D_MODEL), PLE_DIM ** -0.5),
        'g_ple': gain(ks[20], (DEPTH, D_MODEL)),
        'g_ple_gate': gain(ks[21], (DEPTH, D_MODEL)),
        'w_ple_gate': nrm(ks[22], (DEPTH, D_MODEL, D_MODEL), D_MODEL ** -0.5),
        'g_final': gain(ks[23], (D_MODEL,)),
    }


def reference(x_prompt, x_sample, state_gla, p_prompt, p_sample, g_mix, w_in, w_a2, b_a,
              g_gla_norm, g_gmlp_ln, b_gmlp_ln, w_s, b_s, g_gmlp_out, w_out, g_ffn,
              w_ffn_in, w_ffn_out, w_ple, g_ple, g_ple_gate, w_ple_gate, g_final):
    hp, hs = x_prompt, x_sample
    sp_list, ss_list, vs_list = [], [], []
    for l in range(DEPTH):
        lw = (g_mix[l], w_in[l], w_a2[l], b_a[l], g_gla_norm[l], g_gmlp_ln[l], b_gmlp_ln[l],
              w_s[l], b_s[l], g_gmlp_out[l], w_out[l], g_ffn[l], w_ffn_in[l], w_ffn_out[l],
              w_ple[l], g_ple[l], g_ple_gate[l], w_ple_gate[l])
        s0_p = jnp.zeros((hp.shape[0], GLA_HEADS, GLA_DK, GLA_DV), jnp.float32)
        hp, sp, _ = hybrid_layer(hp, p_prompt[l], s0_p, *lw)
        hs, ss, vs = hybrid_layer(hs, p_sample[l], state_gla[l], *lw)
        sp_list.append(sp)
        ss_list.append(ss)
        vs_list.append(vs)
    y_prompt = rmsnorm(hp, g_final)
    y_sample = rmsnorm(hs, g_final)
    new_state_gla_prompt = jnp.stack(sp_list, axis=0)
    new_state_gla_sample = jnp.stack(ss_list, axis=0)
    new_gmlp_v_sample = jnp.stack(vs_list, axis=0)
    return (y_prompt, y_sample, new_state_gla_prompt, new_state_gla_sample, new_gmlp_v_sample)
```

```python
import functools

import jax
import jax.numpy as jnp
from jax import lax
from jax.experimental import pallas as pl
from jax.experimental.pallas import tpu as pltpu

F32 = jnp.float32
BF16 = jnp.bfloat16

D_MODEL = 2048
GLA_WIDTH = 1024
GMLP_WIDTH = 1024
GLA_HEADS = 4
GLA_DV = 256
GLA_DK = 128
GLA_KEY_WIDTH = 512
GLA_GATE_RANK = 16
GLA_TAU = 16.0
GLA_CHUNK = 64
GMLP_GROUPS = 8
GMLP_GROUP_DIM = 128
GMLP_CHUNK = 128
D_FF = 5632
PLE_DIM = 256
EPS = 1e-6
IN_OFFSETS = (0, 512, 1024, 2048, 3072, 3088, 4112, 5136)
Z_WIDTH = 5120
Z_U_OFF = 3072
Z_VG_OFF = 4096

V7X_LANES = 128
V7X_VMEM_BYTES = 64 * 1024 * 1024
VMEM_LIMIT_BYTES = 56 * 1024 * 1024


def _params(*sem):
    return pltpu.CompilerParams(dimension_semantics=sem, vmem_limit_bytes=VMEM_LIMIT_BYTES)


def _rms(x, g):
    return x * lax.rsqrt(jnp.mean(x * x, axis=-1, keepdims=True) + EPS) * g


def _dot(a, b):
    return jnp.dot(a, b, preferred_element_type=F32)


def _inproj_kernel(x_ref, g_ref, w_ref, wa_ref, z_ref, alow_ref, a_scr):
    @pl.when(pl.program_id(1) == 0)
    def _():
        a = _rms(x_ref[...], g_ref[...]).astype(BF16)
        a_scr[...] = a
        alow_ref[...] = _dot(a, wa_ref[...])

    z_ref[...] = _dot(a_scr[...], w_ref[...]).astype(BF16)


def _inproj(x, g_mix, w_z, w_alow, tm, tn):
    m = x.shape[0]
    return pl.pallas_call(
        _inproj_kernel,
        grid=(m // tm, Z_WIDTH // tn),
        in_specs=[
            pl.BlockSpec((tm, D_MODEL), lambda i, j: (i, 0)),
            pl.BlockSpec((1, D_MODEL), lambda i, j: (0, 0)),
            pl.BlockSpec((D_MODEL, tn), lambda i, j: (0, j)),
            pl.BlockSpec((D_MODEL, V7X_LANES), lambda i, j: (0, 0)),
        ],
        out_specs=[
            pl.BlockSpec((tm, tn), lambda i, j: (i, j)),
            pl.BlockSpec((tm, V7X_LANES), lambda i, j: (i, 0)),
        ],
        out_shape=[
            jax.ShapeDtypeStruct((m, Z_WIDTH), BF16),
            jax.ShapeDtypeStruct((m, V7X_LANES), F32),
        ],
        scratch_shapes=[pltpu.VMEM((tm, D_MODEL), BF16)],
        compiler_params=_params("parallel", "arbitrary"),
        name="inproj",
    )(x, g_mix, w_z, w_alow)


def _gla_kernel(*refs, C, R, sequential):
    if sequential:
        (q_ref, k_ref, v_ref, r_ref, alow_ref, wa2_ref, ba_ref, gn_ref,
         o_ref, s_out_ref, s_scr, b_scr, e_scr) = refs
    else:
        (q_ref, k_ref, v_ref, r_ref, alow_ref, wa2_ref, ba_ref, gn_ref, s0_ref,
         o_ref, s_out_ref, b_scr, e_scr) = refs
    n = R // C
    mid = C // 2
    shift = C.bit_length() - 1

    x = _dot(alow_ref[...].astype(BF16), wa2_ref[...].astype(BF16)) + ba_ref[...]
    lb = (jnp.minimum(x, 0.0) - jnp.log1p(jnp.exp(-jnp.abs(x)))) / GLA_TAU

    row = lax.broadcasted_iota(jnp.int32, (R, R), 0)
    col = lax.broadcasted_iota(jnp.int32, (R, R), 1)
    causal = jnp.logical_and((row >> shift) == (col >> shift), col <= row)

    tri = jnp.where(causal, 1.0, 0.0).astype(BF16)
    lb_hi = lb.astype(BF16)
    lb_lo = (lb - lb_hi.astype(F32)).astype(BF16)
    b = _dot(tri, lb_hi) + _dot(tri, lb_lo)
    b_scr[...] = b
    b3 = b.reshape(n, C, GLA_DK)
    b_mid = jnp.broadcast_to(b3[:, mid:mid + 1, :], (n, C, GLA_DK)).reshape(R, GLA_DK)
    b_last = jnp.broadcast_to(b3[:, C - 1:C, :], (n, C, GLA_DK)).reshape(R, GLA_DK)

    q = q_ref[...].astype(F32) * (GLA_DK ** -0.5)
    k = k_ref[...].astype(F32)
    v = v_ref[...]
    q1 = (q * jnp.exp(b)).astype(BF16)
    q2 = (q * jnp.exp(b - b_mid)).astype(BF16)
    k2 = (k * jnp.exp(b_mid - b)).astype(BF16)
    k3 = (k * jnp.exp(b_last - b))

    att = lax.dot_general(q2, k2, (((1,), (1,)), ((), ())), preferred_element_type=F32)
    att = jnp.where(causal, att, 0.0).astype(BF16)
    o = _dot(att, v)

    e_scr[...] = jnp.zeros((V7X_LANES, GLA_DK), F32)
    e_scr[0:n, :] = jnp.exp(b_scr[pl.ds(C - 1, n, stride=C), :])
    d_cols = e_scr[...].T

    if sequential:
        @pl.when(pl.program_id(1) == 0)
        def _():
            s_scr[...] = jnp.zeros((GLA_DK, GLA_DV), F32)

        s = s_scr[...]
        o_parts = []
        for g in range(n):
            rows = slice(g * C, (g + 1) * C)
            o_parts.append(o[rows] + _dot(q1[rows], s.astype(BF16)))
            upd = lax.dot_general(k3[rows].astype(BF16), v[rows], (((0,), (0,)), ((), ())),
                                  preferred_element_type=F32)
            s = d_cols[:, g:g + 1] * s + upd
        s_scr[...] = s
        s_out_ref[0, 0] = s
        o = jnp.concatenate(o_parts, axis=0)
    else:
        grp = lax.broadcasted_iota(jnp.int32, (R, 1), 0) >> shift
        for g in range(n):
            s = s0_ref[g, 0]
            o = o + jnp.where(grp == g, _dot(q1, s.astype(BF16)), 0.0)
            k3g = jnp.where(grp == g, k3, 0.0).astype(BF16)
            upd = lax.dot_general(k3g, v, (((0,), (0,)), ((), ())), preferred_element_type=F32)
            s_out_ref[g, 0] = d_cols[:, g:g + 1] * s + upd

    o = _rms(o, gn_ref[...]) * jax.nn.silu(r_ref[...].astype(F32))
    o_ref[...] = o.astype(BF16)


def _gla(z, alow, wa2_pad, b_a, g_norm, s0, *, n_seq, seq_len):
    m = z.shape[0]
    sequential = s0 is None
    if sequential:
        C, R = GLA_CHUNK, 4 * GLA_CHUNK
        nt = seq_len // R
        grid = (n_seq * GLA_HEADS, nt)
        rowblk = lambda bh, t: (bh // GLA_HEADS) * nt + t
        head = lambda bh, t: bh % GLA_HEADS
        s_block = (1, 1, GLA_DK, GLA_DV)
        s_map = lambda bh, t: (bh // GLA_HEADS, bh % GLA_HEADS, 0, 0)
        sem = ("parallel", "arbitrary")
    else:
        C = seq_len
        R = 64
        grid = (m // R, GLA_HEADS)
        rowblk = lambda i, h: i
        head = lambda i, h: h
        s_block = (R // C, 1, GLA_DK, GLA_DV)
        s_map = lambda i, h: (i, h, 0, 0)
        sem = ("parallel", "arbitrary")
    v_off = IN_OFFSETS[2] // GLA_DV
    r_off = IN_OFFSETS[3] // GLA_DV
    in_specs = [
        pl.BlockSpec((R, GLA_DK), lambda a, c: (rowblk(a, c), head(a, c))),
        pl.BlockSpec((R, GLA_DK), lambda a, c: (rowblk(a, c), GLA_HEADS + head(a, c))),
        pl.BlockSpec((R, GLA_DV), lambda a, c: (rowblk(a, c), v_off + head(a, c))),
        pl.BlockSpec((R, GLA_DV), lambda a, c: (rowblk(a, c), r_off + head(a, c))),
        pl.BlockSpec((R, V7X_LANES), lambda a, c: (rowblk(a, c), 0)),
        pl.BlockSpec((V7X_LANES, GLA_DK), lambda a, c: (0, head(a, c))),
        pl.BlockSpec((1, GLA_DK), lambda a, c: (0, head(a, c))),
        pl.BlockSpec((1, GLA_DV), lambda a, c: (0, head(a, c))),
    ]
    args = [z, z, z, z, alow, wa2_pad, b_a, g_norm]
    scratch = [pltpu.VMEM((R, GLA_DK), F32), pltpu.VMEM((V7X_LANES, GLA_DK), F32)]
    if sequential:
        scratch = [pltpu.VMEM((GLA_DK, GLA_DV), F32)] + scratch
    else:
        in_specs.append(pl.BlockSpec(s_block, s_map))
        args.append(s0)
    return pl.pallas_call(
        functools.partial(_gla_kernel, C=C, R=R, sequential=sequential),
        grid=grid,
        in_specs=in_specs,
        out_specs=[
            pl.BlockSpec((R, GLA_DV), lambda a, c: (rowblk(a, c), head(a, c))),
            pl.BlockSpec(s_block, s_map),
        ],
        out_shape=[
            jax.ShapeDtypeStruct((m, GLA_WIDTH), BF16),
            jax.ShapeDtypeStruct((n_seq, GLA_HEADS, GLA_DK, GLA_DV), F32),
        ],
        scratch_shapes=scratch,
        compiler_params=_params(*sem),
        name="gla_seq" if sequential else "gla_step",
    )(*args)


def _gmlp_kernel(u_ref, vg_ref, gln_ref, bln_ref, ws_ref, bias_ref, gout_ref, *out_refs, tm, blk, emit_vn):
    m_ref = out_refs[0]
    shift = blk.bit_length() - 1
    row = lax.broadcasted_iota(jnp.int32, (GMLP_CHUNK, GMLP_CHUNK), 0)
    col = lax.broadcasted_iota(jnp.int32, (GMLP_CHUNK, GMLP_CHUNK), 1)
    causal = jnp.logical_and((row >> shift) == (col >> shift), col <= row)
    w = [jnp.where(causal, ws_ref[g], 0.0).astype(BF16) for g in range(GMLP_GROUPS)]
    for c in range(tm // GMLP_CHUNK):
        rows = pl.ds(c * GMLP_CHUNK, GMLP_CHUNK)
        vg = jax.nn.gelu(vg_ref[rows, :].astype(F32))
        mu = jnp.mean(vg, axis=-1, keepdims=True)
        vc = vg - mu
        vn = vc * lax.rsqrt(jnp.mean(vc * vc, axis=-1, keepdims=True) + EPS) * gln_ref[...] + bln_ref[...]
        if emit_vn:
            out_refs[1][rows, :] = vn
        vn16 = vn.astype(BF16)
        mixed = jnp.concatenate(
            [_dot(w[g], vn16[:, g * GMLP_GROUP_DIM:(g + 1) * GMLP_GROUP_DIM]) for g in range(GMLP_GROUPS)],
            axis=1) + bias_ref[...]
        mm = jax.nn.gelu(u_ref[rows, :].astype(F32)) * mixed
        m_ref[rows, :] = _rms(mm, gout_ref[...]).astype(BF16)


def _gmlp(z, g_ln, b_ln, w_mix, bias, g_out, *, tm, blk, emit_vn):
    m = z.shape[0]
    out_specs = [pl.BlockSpec((tm, GMLP_WIDTH), lambda i: (i, 0))]
    out_shape = [jax.ShapeDtypeStruct((m, GMLP_WIDTH), BF16)]
    if emit_vn:
        out_specs.append(pl.BlockSpec((tm, GMLP_WIDTH), lambda i: (i, 0)))
        out_shape.append(jax.ShapeDtypeStruct((m, GMLP_WIDTH), F32))
    return pl.pallas_call(
        functools.partial(_gmlp_kernel, tm=tm, blk=blk, emit_vn=emit_vn),
        grid=(m // tm,),
        in_specs=[
            pl.BlockSpec((tm, GMLP_WIDTH), lambda i: (i, Z_U_OFF // GMLP_WIDTH)),
            pl.BlockSpec((tm, GMLP_WIDTH), lambda i: (i, Z_VG_OFF // GMLP_WIDTH)),
            pl.BlockSpec((1, GMLP_WIDTH), lambda i: (0, 0)),
            pl.BlockSpec((1, GMLP_WIDTH), lambda i: (0, 0)),
            pl.BlockSpec((GMLP_GROUPS, GMLP_CHUNK, GMLP_CHUNK), lambda i: (0, 0, 0)),
            pl.BlockSpec((GMLP_CHUNK, GMLP_WIDTH), lambda i: (0, 0)),
            pl.BlockSpec((1, GMLP_WIDTH), lambda i: (0, 0)),
        ],
        out_specs=out_specs,
        out_shape=out_shape,
        compiler_params=_params("parallel"),
        name="gmlp",
    )(z, z, g_ln, b_ln, w_mix, bias, g_out)


def _outproj_kernel(h_ref, o_ref, m_ref, wo_ref, wm_ref, g_ref, h1_ref, f_ref):
    h1 = h_ref[...] + _dot(o_ref[...], wo_ref[...]) + _dot(m_ref[...], wm_ref[...])
    h1_ref[...] = h1
    f_ref[...] = _rms(h1, g_ref[...]).astype(BF16)


def _outproj(h, o, mm, w_out, g_ffn, tm):
    m = h.shape[0]
    return pl.pallas_call(
        _outproj_kernel,
        grid=(m // tm,),
        in_specs=[
            pl.BlockSpec((tm, D_MODEL), lambda i: (i, 0)),
            pl.BlockSpec((tm, GLA_WIDTH), lambda i: (i, 0)),
            pl.BlockSpec((tm, GMLP_WIDTH), lambda i: (i, 0)),
            pl.BlockSpec((GLA_WIDTH, D_MODEL), lambda i: (0, 0)),
            pl.BlockSpec((GMLP_WIDTH, D_MODEL), lambda i: (1, 0)),
            pl.BlockSpec((1, D_MODEL), lambda i: (0, 0)),
        ],
        out_specs=[
            pl.BlockSpec((tm, D_MODEL), lambda i: (i, 0)),
            pl.BlockSpec((tm, D_MODEL), lambda i: (i, 0)),
        ],
        out_shape=[
            jax.ShapeDtypeStruct((m, D_MODEL), F32),
            jax.ShapeDtypeStruct((m, D_MODEL), BF16),
        ],
        compiler_params=_params("parallel"),
        name="outproj",
    )(h, o, mm, w_out, w_out, g_ffn)


def _ffn_kernel(f_ref, h_ref, wg_ref, wu_ref, wo_ref, o_ref):
    @pl.when(pl.program_id(1) == 0)
    def _():
        o_ref[...] = h_ref[...]

    f = f_ref[...]
    act = (jax.nn.silu(_dot(f, wg_ref[...])) * _dot(f, wu_ref[...])).astype(BF16)
    o_ref[...] += _dot(act, wo_ref[...])


def _ffn(f, h1, w_in, w_out, tm, tf):
    m = f.shape[0]
    nf = D_FF // tf
    return pl.pallas_call(
        _ffn_kernel,
        grid=(m // tm, nf),
        in_specs=[
            pl.BlockSpec((tm, D_MODEL), lambda i, j: (i, 0)),
            pl.BlockSpec((tm, D_MODEL), lambda i, j: (i, 0)),
            pl.BlockSpec((D_MODEL, tf), lambda i, j: (0, j)),
            pl.BlockSpec((D_MODEL, tf), lambda i, j: (0, nf + j)),
            pl.BlockSpec((tf, D_MODEL), lambda i, j: (j, 0)),
        ],
        out_specs=pl.BlockSpec((tm, D_MODEL), lambda i, j: (i, 0)),
        out_shape=jax.ShapeDtypeStruct((m, D_MODEL), F32),
        compiler_params=_params("parallel", "arbitrary"),
        name="ffn",
    )(f, h1, w_in, w_in, w_out)


def _ple_kernel(h_ref, p_ref, wp_ref, gp_ref, gg_ref, wg_ref, gf_ref, y_ref):
    h = h_ref[...]
    pe = _rms(_dot(p_ref[...].astype(BF16), wp_ref[...]), gp_ref[...])
    gate = jax.nn.sigmoid(_dot(_rms(h, gg_ref[...]).astype(BF16), wg_ref[...]))
    y_ref[...] = _rms(h + pe * gate, gf_ref[...])


def _ple(h, p, w_ple, g_ple, g_gate, w_gate, g_final, tm):
    m = h.shape[0]
    vec = pl.BlockSpec((1, D_MODEL), lambda i: (0, 0))
    return pl.pallas_call(
        _ple_kernel,
        grid=(m // tm,),
        in_specs=[
            pl.BlockSpec((tm, D_MODEL), lambda i: (i, 0)),
            pl.BlockSpec((tm, PLE_DIM), lambda i: (i, 0)),
            pl.BlockSpec((PLE_DIM, D_MODEL), lambda i: (0, 0)),
            vec, vec,
            pl.BlockSpec((D_MODEL, D_MODEL), lambda i: (0, 0)),
            vec,
        ],
        out_specs=pl.BlockSpec((tm, D_MODEL), lambda i: (i, 0)),
        out_shape=jax.ShapeDtypeStruct((m, D_MODEL), F32),
        compiler_params=_params("parallel"),
        name="ple",
    )(h, p, w_ple, g_ple, g_gate, w_gate, g_final)


def _layer(x, p, s0, wts, *, n_seq, seq_len):
    z, alow = _inproj(x, wts["g_mix"], wts["w_z"], wts["w_alow"], tm=1024, tn=1024)
    o, s_new = _gla(z, alow, wts["wa2_pad"], wts["b_a"], wts["g_gla_norm"], s0, n_seq=n_seq, seq_len=seq_len)
    blk = min(seq_len, GMLP_CHUNK)
    reps = GMLP_CHUNK // blk
    w_mix = jnp.tile(wts["w_s"][:, :blk, :blk], (1, reps, reps))
    bias = jnp.tile(jnp.repeat(wts["b_s"][:, :blk].T, GMLP_GROUP_DIM, axis=1), (reps, 1))
    emit_vn = s0 is not None
    gm = _gmlp(z, wts["g_gmlp_ln"], wts["b_gmlp_ln"], w_mix, bias, wts["g_gmlp_out"],
               tm=256, blk=blk, emit_vn=emit_vn)
    h1, f = _outproj(x, o, gm[0], wts["w_out"], wts["g_ffn"], tm=512)
    h2 = _ffn(f, h1, wts["w_ffn_in"], wts["w_ffn_out"], tm=512, tf=512)
    y = _ple(h2, p, wts["w_ple"], wts["g_ple"], wts["g_ple_gate"], wts["w_ple_gate"], wts["g_final"], tm=512)
    return y, s_new, (gm[1] if emit_vn else None)


def kernel(x_prompt, x_sample, state_gla, p_prompt, p_sample, g_mix, w_in, w_a2, b_a, g_gla_norm, g_gmlp_ln,
           b_gmlp_ln, w_s, b_s, g_gmlp_out, w_out, g_ffn, w_ffn_in, w_ffn_out, w_ple, g_ple, g_ple_gate,
           w_ple_gate, g_final):
    batch, seq, _ = x_prompt.shape
    dec_batch, dec_seq, _ = x_sample.shape
    depth = w_in.shape[0]
    assert depth == 1, "one layer: the prompt / sample passes below are not chained over depth"
    row = lambda a: a.reshape(1, -1)
    w = w_in[0]
    wts = dict(
        g_mix=row(g_mix[0]),
        w_z=jnp.concatenate([w[:, :IN_OFFSETS[4]], w[:, IN_OFFSETS[5]:]], axis=1).astype(BF16),
        w_alow=jnp.pad(w[:, IN_OFFSETS[4]:IN_OFFSETS[5]], ((0, 0), (0, V7X_LANES - GLA_GATE_RANK))).astype(BF16),
        wa2_pad=jnp.pad(w_a2[0], ((0, V7X_LANES - GLA_GATE_RANK), (0, 0))),
        b_a=row(b_a[0]),
        g_gla_norm=row(g_gla_norm[0]),
        g_gmlp_ln=row(g_gmlp_ln[0]),
        b_gmlp_ln=row(b_gmlp_ln[0]),
        w_s=w_s[0],
        b_s=b_s[0],
        g_gmlp_out=row(g_gmlp_out[0]),
        w_out=w_out[0].astype(BF16),
        g_ffn=row(g_ffn[0]),
        w_ffn_in=w_ffn_in[0].astype(BF16),
        w_ffn_out=w_ffn_out[0].astype(BF16),
        w_ple=w_ple[0].astype(BF16),
        g_ple=row(g_ple[0]),
        g_ple_gate=row(g_ple_gate[0]),
        w_ple_gate=w_ple_gate[0].astype(BF16),
        g_final=row(g_final),
    )
    yp, sp, _ = _layer(x_prompt.reshape(batch * seq, D_MODEL), p_prompt[0].reshape(batch * seq, PLE_DIM),
                       None, wts, n_seq=batch, seq_len=seq)
    ys, ss, vs = _layer(x_sample.reshape(dec_batch * dec_seq, D_MODEL),
                        p_sample[0].reshape(dec_batch * dec_seq, PLE_DIM),
                        state_gla[0], wts, n_seq=dec_batch, seq_len=dec_seq)
    return (yp.reshape(batch, seq, D_MODEL), ys.reshape(dec_batch, dec_seq, D_MODEL),
            sp[None], ss[None], vs.reshape(1, dec_batch, dec_seq, GMLP_WIDTH))
```

```python
import functools

import numpy as np

import jax
import jax.numpy as jnp
from jax import lax
from jax.experimental import pallas as pl
from jax.experimental.pallas import tpu as pltpu

F32 = jnp.float32
BF16 = jnp.bfloat16

D_MODEL = 2048
GLA_WIDTH = 1024
GMLP_WIDTH = 1024
GLA_HEADS = 4
GLA_DV = 256
GLA_DK = 128
GLA_KEY_WIDTH = 512
GLA_GATE_RANK = 16
GLA_TAU = 16.0
GLA_CHUNK = 64
GMLP_GROUPS = 8
GMLP_GROUP_DIM = 128
GMLP_CHUNK = 128
D_FF = 5632
PLE_DIM = 256
EPS = 1e-6
IN_OFFSETS = (0, 512, 1024, 2048, 3072, 3088, 4112, 5136)
Z_WIDTH = 5120
Z_GLA_WIDTH = 3072
Z_U_OFF = 3072
Z_VG_OFF = 4096

V7X_LANES = 128
V7X_VMEM_BYTES = 64 * 1024 * 1024
VMEM_LIMIT_BYTES = 56 * 1024 * 1024


def _params(*sem):
    return pltpu.CompilerParams(dimension_semantics=sem, vmem_limit_bytes=VMEM_LIMIT_BYTES)


def _rms(x, g):
    return x * lax.rsqrt(jnp.mean(x * x, axis=-1, keepdims=True) + EPS) * g


def _dot(a, b):
    return jnp.dot(a, b, preferred_element_type=F32)


def _dot_tn(a, b):
    return lax.dot_general(a, b, (((0,), (0,)), ((), ())), preferred_element_type=F32)


def _dot_nt(a, b):
    return lax.dot_general(a, b, (((1,), (1,)), ((), ())), preferred_element_type=F32)


def _inproj_kernel(x_ref, g_ref, wa_ref, wb_ref, wl_ref, z_ref, alow_ref, a_scr, *, n_a):
    j = pl.program_id(1)

    @pl.when(j == 0)
    def _():
        a = _rms(x_ref[...], g_ref[...]).astype(BF16)
        a_scr[...] = a
        alow_ref[...] = _dot(a, wl_ref[...].astype(BF16))

    @pl.when(j < n_a)
    def _():
        z_ref[...] = _dot(a_scr[...], wa_ref[...]).astype(BF16)

    @pl.when(j >= n_a)
    def _():
        z_ref[...] = _dot(a_scr[...], wb_ref[...]).astype(BF16)


def _inproj(x, g_mix, w_gla, w_gmlp, w_in_f32, tm, tn):
    m = x.shape[0]
    n_a = Z_GLA_WIDTH // tn
    n_b = (Z_WIDTH - Z_GLA_WIDTH) // tn
    return pl.pallas_call(
        functools.partial(_inproj_kernel, n_a=n_a),
        grid=(m // tm, n_a + n_b),
        in_specs=[
            pl.BlockSpec((tm, D_MODEL), lambda i, j: (i, 0)),
            pl.BlockSpec((1, D_MODEL), lambda i, j: (0, 0)),
            pl.BlockSpec((D_MODEL, tn), lambda i, j: (0, jnp.minimum(j, n_a - 1))),
            pl.BlockSpec((D_MODEL, tn), lambda i, j: (0, jnp.maximum(j - n_a, 0))),
            pl.BlockSpec((D_MODEL, V7X_LANES), lambda i, j: (0, IN_OFFSETS[4] // V7X_LANES)),
        ],
        out_specs=[
            pl.BlockSpec((tm, tn), lambda i, j: (i, j)),
            pl.BlockSpec((tm, V7X_LANES), lambda i, j: (i, 0)),
        ],
        out_shape=[
            jax.ShapeDtypeStruct((m, Z_WIDTH), BF16),
            jax.ShapeDtypeStruct((m, V7X_LANES), F32),
        ],
        scratch_shapes=[pltpu.VMEM((tm, D_MODEL), BF16)],
        compiler_params=_params("parallel", "arbitrary"),
        name="inproj",
    )(x, g_mix, w_gla, w_gmlp, w_in_f32)


def _group_mask(R, C):
    t = np.arange(R)[:, None]
    s = np.arange(R)[None, :]
    return ((t // C == s // C) & (s <= t)).astype(np.float32)


def _gla_kernel(*refs, C, R, sequential):
    if sequential:
        (q_ref, k_ref, v_ref, r_ref, alow_ref, wa2_ref, ba_ref, gn_ref, tri_ref, mask_ref,
         o_ref, s_out_ref, s_scr, b_scr, e_scr) = refs
    else:
        (q_ref, k_ref, v_ref, r_ref, alow_ref, wa2_ref, ba_ref, gn_ref, tri_ref, mask_ref, s0_ref,
         o_ref, s_out_ref, b_scr, e_scr) = refs
    n = R // C
    mid = C // 2
    W = GLA_KEY_WIDTH

    x = _dot(alow_ref[...].astype(BF16), wa2_ref[...].astype(BF16)) + ba_ref[...]
    lb = (jnp.minimum(x, 0.0) - jnp.log1p(jnp.exp(-jnp.abs(x)))) / GLA_TAU

    tri = tri_ref[...]
    lb_hi = lb.astype(BF16)
    lb_lo = (lb - lb_hi.astype(F32)).astype(BF16)
    b = _dot(tri, lb_hi) + _dot(tri, lb_lo)
    b3 = b.reshape(n, C, W)
    b_mid = jnp.broadcast_to(b3[:, mid:mid + 1, :], (n, C, W)).reshape(R, W)
    b_last = jnp.broadcast_to(b3[:, C - 1:C, :], (n, C, W)).reshape(R, W)

    q = q_ref[...].astype(F32) * (GLA_DK ** -0.5)
    k = k_ref[...].astype(F32)
    q1 = (q * jnp.exp(b)).astype(BF16)
    q2 = (q * jnp.exp(b - b_mid)).astype(BF16)
    k2 = (k * jnp.exp(b_mid - b)).astype(BF16)
    k3 = k * jnp.exp(b_last - b)

    e_scr[...] = jnp.zeros((V7X_LANES, GLA_DK), F32)
    for h in range(GLA_HEADS):
        b_scr[h] = b[:, h * GLA_DK:(h + 1) * GLA_DK]
        e_scr[h * n:(h + 1) * n, :] = jnp.exp(b_scr[h, pl.ds(C - 1, n, stride=C), :])
    d_cols = e_scr[...].T

    causal = mask_ref[...] > 0.0
    if sequential:
        @pl.when(pl.program_id(1) == 0)
        def _():
            s_scr[...] = jnp.zeros((GLA_HEADS, GLA_DK, GLA_DV), F32)
    else:
        grp = lax.broadcasted_iota(jnp.int32, (R, 1), 0) >> (C.bit_length() - 1)

    for h in range(GLA_HEADS):
        ks = slice(h * GLA_DK, (h + 1) * GLA_DK)
        vs = slice(h * GLA_DV, (h + 1) * GLA_DV)
        v = v_ref[:, vs]
        att = jnp.where(causal, _dot_nt(q2[:, ks], k2[:, ks]), 0.0).astype(BF16)
        o = _dot(att, v)
        q1h = q1[:, ks]
        k3h = k3[:, ks]
        if sequential:
            s = s_scr[h]
            o_parts = []
            for g in range(n):
                rows = slice(g * C, (g + 1) * C)
                o_parts.append(o[rows] + _dot(q1h[rows], s.astype(BF16)))
                s = d_cols[:, h * n + g:h * n + g + 1] * s + _dot_tn(k3h[rows].astype(BF16), v[rows])
            s_scr[h] = s
            s_out_ref[0, h] = s
            o = jnp.concatenate(o_parts, axis=0)
        else:
            for g in range(n):
                s = s0_ref[g, h]
                o = o + jnp.where(grp == g, _dot(q1h, s.astype(BF16)), 0.0)
                k3g = jnp.where(grp == g, k3h, 0.0).astype(BF16)
                s_out_ref[g, h] = d_cols[:, h * n + g:h * n + g + 1] * s + _dot_tn(k3g, v)
        o = _rms(o, gn_ref[:, vs]) * jax.nn.silu(r_ref[:, vs].astype(F32))
        o_ref[:, vs] = o.astype(BF16)


def _gla(z, alow, wa2_pad, b_a, g_norm, s0, *, n_seq, seq_len):
    m = z.shape[0]
    sequential = s0 is None
    if sequential:
        C, R = GLA_CHUNK, 4 * GLA_CHUNK
        nt = seq_len // R
        grid = (n_seq, nt)
        rowblk = lambda a, c: a * nt + c
        s_block = (1, GLA_HEADS, GLA_DK, GLA_DV)
        s_map = lambda a, c: (a, 0, 0, 0)
    else:
        C, R = seq_len, 64
        grid = (m // R, 1)
        rowblk = lambda a, c: a
        s_block = (R // C, GLA_HEADS, GLA_DK, GLA_DV)
        s_map = lambda a, c: (a, 0, 0, 0)
    mask = _group_mask(R, C)
    const = lambda a, c: (0, 0)
    in_specs = [
        pl.BlockSpec((R, GLA_KEY_WIDTH), lambda a, c: (rowblk(a, c), 0)),
        pl.BlockSpec((R, GLA_KEY_WIDTH), lambda a, c: (rowblk(a, c), 1)),
        pl.BlockSpec((R, GLA_WIDTH), lambda a, c: (rowblk(a, c), 1)),
        pl.BlockSpec((R, GLA_WIDTH), lambda a, c: (rowblk(a, c), 2)),
        pl.BlockSpec((R, V7X_LANES), lambda a, c: (rowblk(a, c), 0)),
        pl.BlockSpec((V7X_LANES, GLA_KEY_WIDTH), const),
        pl.BlockSpec((1, GLA_KEY_WIDTH), const),
        pl.BlockSpec((1, GLA_WIDTH), const),
        pl.BlockSpec((R, R), const),
        pl.BlockSpec((R, R), const),
    ]
    args = [z, z, z, z, alow, wa2_pad, b_a, g_norm, jnp.asarray(mask, BF16), jnp.asarray(mask, F32)]
    scratch = [pltpu.VMEM((GLA_HEADS, R, GLA_DK), F32), pltpu.VMEM((V7X_LANES, GLA_DK), F32)]
    if sequential:
        scratch = [pltpu.VMEM((GLA_HEADS, GLA_DK, GLA_DV), F32)] + scratch
    else:
        in_specs.append(pl.BlockSpec(s_block, s_map))
        args.append(s0)
    return pl.pallas_call(
        functools.partial(_gla_kernel, C=C, R=R, sequential=sequential),
        grid=grid,
        in_specs=in_specs,
        out_specs=[
            pl.BlockSpec((R, GLA_WIDTH), lambda a, c: (rowblk(a, c), 0)),
            pl.BlockSpec(s_block, s_map),
        ],
        out_shape=[
            jax.ShapeDtypeStruct((m, GLA_WIDTH), BF16),
            jax.ShapeDtypeStruct((n_seq, GLA_HEADS, GLA_DK, GLA_DV), F32),
        ],
        scratch_shapes=scratch,
        compiler_params=_params("parallel", "arbitrary"),
        name="gla_seq" if sequential else "gla_step",
    )(*args)


def _gmlp_kernel(u_ref, vg_ref, gln_ref, bln_ref, ws_ref, bias_ref, gout_ref, *out_refs, tm, blk, emit_vn):
    m_ref = out_refs[0]
    shift = blk.bit_length() - 1
    row = lax.broadcasted_iota(jnp.int32, (GMLP_CHUNK, GMLP_CHUNK), 0)
    col = lax.broadcasted_iota(jnp.int32, (GMLP_CHUNK, GMLP_CHUNK), 1)
    causal = jnp.logical_and((row >> shift) == (col >> shift), col <= row)
    w = [jnp.where(causal, ws_ref[g], 0.0).astype(BF16) for g in range(GMLP_GROUPS)]
    for c in range(tm // GMLP_CHUNK):
        rows = pl.ds(c * GMLP_CHUNK, GMLP_CHUNK)
        vg = jax.nn.gelu(vg_ref[rows, :].astype(F32))
        mu = jnp.mean(vg, axis=-1, keepdims=True)
        vc = vg - mu
        vn = vc * lax.rsqrt(jnp.mean(vc * vc, axis=-1, keepdims=True) + EPS) * gln_ref[...] + bln_ref[...]
        if emit_vn:
            out_refs[1][rows, :] = vn
        vn16 = vn.astype(BF16)
        mixed = jnp.concatenate(
            [_dot(w[g], vn16[:, g * GMLP_GROUP_DIM:(g + 1) * GMLP_GROUP_DIM]) for g in range(GMLP_GROUPS)],
            axis=1) + bias_ref[...]
        mm = jax.nn.gelu(u_ref[rows, :].astype(F32)) * mixed
        m_ref[rows, :] = _rms(mm, gout_ref[...]).astype(BF16)


def _gmlp(z, g_ln, b_ln, w_mix, bias, g_out, *, tm, blk, emit_vn):
    m = z.shape[0]
    out_specs = [pl.BlockSpec((tm, GMLP_WIDTH), lambda i: (i, 0))]
    out_shape = [jax.ShapeDtypeStruct((m, GMLP_WIDTH), BF16)]
    if emit_vn:
        out_specs.append(pl.BlockSpec((tm, GMLP_WIDTH), lambda i: (i, 0)))
        out_shape.append(jax.ShapeDtypeStruct((m, GMLP_WIDTH), F32))
    return pl.pallas_call(
        functools.partial(_gmlp_kernel, tm=tm, blk=blk, emit_vn=emit_vn),
        grid=(m // tm,),
        in_specs=[
            pl.BlockSpec((tm, GMLP_WIDTH), lambda i: (i, Z_U_OFF // GMLP_WIDTH)),
            pl.BlockSpec((tm, GMLP_WIDTH), lambda i: (i, Z_VG_OFF // GMLP_WIDTH)),
            pl.BlockSpec((1, GMLP_WIDTH), lambda i: (0, 0)),
            pl.BlockSpec((1, GMLP_WIDTH), lambda i: (0, 0)),
            pl.BlockSpec((GMLP_GROUPS, GMLP_CHUNK, GMLP_CHUNK), lambda i: (0, 0, 0)),
            pl.BlockSpec((GMLP_CHUNK, GMLP_WIDTH), lambda i: (0, 0)),
            pl.BlockSpec((1, GMLP_WIDTH), lambda i: (0, 0)),
        ],
        out_specs=out_specs,
        out_shape=out_shape,
        compiler_params=_params("parallel"),
        name="gmlp",
    )(z, z, g_ln, b_ln, w_mix, bias, g_out)


def _outproj_kernel(h_ref, o_ref, m_ref, wo_ref, wm_ref, g_ref, h1_ref, f_ref):
    h1 = h_ref[...] + _dot(o_ref[...], wo_ref[...]) + _dot(m_ref[...], wm_ref[...])
    h1_ref[...] = h1
    f_ref[...] = _rms(h1, g_ref[...]).astype(BF16)


def _outproj(h, o, mm, w_out, g_ffn, tm):
    m = h.shape[0]
    return pl.pallas_call(
        _outproj_kernel,
        grid=(m // tm,),
        in_specs=[
            pl.BlockSpec((tm, D_MODEL), lambda i: (i, 0)),
            pl.BlockSpec((tm, GLA_WIDTH), lambda i: (i, 0)),
            pl.BlockSpec((tm, GMLP_WIDTH), lambda i: (i, 0)),
            pl.BlockSpec((GLA_WIDTH, D_MODEL), lambda i: (0, 0)),
            pl.BlockSpec((GMLP_WIDTH, D_MODEL), lambda i: (1, 0)),
            pl.BlockSpec((1, D_MODEL), lambda i: (0, 0)),
        ],
        out_specs=[
            pl.BlockSpec((tm, D_MODEL), lambda i: (i, 0)),
            pl.BlockSpec((tm, D_MODEL), lambda i: (i, 0)),
        ],
        out_shape=[
            jax.ShapeDtypeStruct((m, D_MODEL), F32),
            jax.ShapeDtypeStruct((m, D_MODEL), BF16),
        ],
        compiler_params=_params("parallel"),
        name="outproj",
    )(h, o, mm, w_out, w_out, g_ffn)


def _ffn_kernel(f_ref, h_ref, wg_ref, wu_ref, wo_ref, o_ref):
    @pl.when(pl.program_id(1) == 0)
    def _():
        o_ref[...] = h_ref[...]

    f = f_ref[...]
    act = (jax.nn.silu(_dot(f, wg_ref[...])) * _dot(f, wu_ref[...])).astype(BF16)
    o_ref[...] += _dot(act, wo_ref[...])


def _ffn(f, h1, w_in, w_out, tm, tf):
    m = f.shape[0]
    nf = D_FF // tf
    return pl.pallas_call(
        _ffn_kernel,
        grid=(m // tm, nf),
        in_specs=[
            pl.BlockSpec((tm, D_MODEL), lambda i, j: (i, 0)),
            pl.BlockSpec((tm, D_MODEL), lambda i, j: (i, 0)),
            pl.BlockSpec((D_MODEL, tf), lambda i, j: (0, j)),
            pl.BlockSpec((D_MODEL, tf), lambda i, j: (0, nf + j)),
            pl.BlockSpec((tf, D_MODEL), lambda i, j: (j, 0)),
        ],
        out_specs=pl.BlockSpec((tm, D_MODEL), lambda i, j: (i, 0)),
        out_shape=jax.ShapeDtypeStruct((m, D_MODEL), F32),
        compiler_params=_params("parallel", "arbitrary"),
        name="ffn",
    )(f, h1, w_in, w_in, w_out)


def _ple_kernel(h_ref, p_ref, wp_ref, gp_ref, gg_ref, wg_ref, gf_ref, y_ref):
    h = h_ref[...]
    pe = _rms(_dot(p_ref[...].astype(BF16), wp_ref[...]), gp_ref[...])
    gate = jax.nn.sigmoid(_dot(_rms(h, gg_ref[...]).astype(BF16), wg_ref[...]))
    y_ref[...] = _rms(h + pe * gate, gf_ref[...])


def _ple(h, p, w_ple, g_ple, g_gate, w_gate, g_final, tm):
    m = h.shape[0]
    vec = pl.BlockSpec((1, D_MODEL), lambda i: (0, 0))
    return pl.pallas_call(
        _ple_kernel,
        grid=(m // tm,),
        in_specs=[
            pl.BlockSpec((tm, D_MODEL), lambda i: (i, 0)),
            pl.BlockSpec((tm, PLE_DIM), lambda i: (i, 0)),
            pl.BlockSpec((PLE_DIM, D_MODEL), lambda i: (0, 0)),
            vec, vec,
            pl.BlockSpec((D_MODEL, D_MODEL), lambda i: (0, 0)),
            vec,
        ],
        out_specs=pl.BlockSpec((tm, D_MODEL), lambda i: (i, 0)),
        out_shape=jax.ShapeDtypeStruct((m, D_MODEL), F32),
        compiler_params=_params("parallel"),
        name="ple",
    )(h, p, w_ple, g_ple, g_gate, w_gate, g_final)


def _layer(x, p, s0, wts, *, n_seq, seq_len):
    z, alow = _inproj(x, wts["g_mix"], wts["w_gla"], wts["w_gmlp"], wts["w_in"], tm=1024, tn=1024)
    o, s_new = _gla(z, alow, wts["wa2_pad"], wts["b_a"], wts["g_gla_norm"], s0, n_seq=n_seq, seq_len=seq_len)
    blk = min(seq_len, GMLP_CHUNK)
    reps = GMLP_CHUNK // blk
    w_mix = jnp.tile(wts["w_s"][:, :blk, :blk], (1, reps, reps))
    bias = jnp.tile(jnp.repeat(wts["b_s"][:, :blk].T, GMLP_GROUP_DIM, axis=1), (reps, 1))
    emit_vn = s0 is not None
    gm = _gmlp(z, wts["g_gmlp_ln"], wts["b_gmlp_ln"], w_mix, bias, wts["g_gmlp_out"],
               tm=256, blk=blk, emit_vn=emit_vn)
    h1, f = _outproj(x, o, gm[0], wts["w_out"], wts["g_ffn"], tm=512)
    h2 = _ffn(f, h1, wts["w_ffn_in"], wts["w_ffn_out"], tm=512, tf=512)
    y = _ple(h2, p, wts["w_ple"], wts["g_ple"], wts["g_ple_gate"], wts["w_ple_gate"], wts["g_final"], tm=512)
    return y, s_new, (gm[1] if emit_vn else None)


def kernel(x_prompt, x_sample, state_gla, p_prompt, p_sample, g_mix, w_in, w_a2, b_a, g_gla_norm, g_gmlp_ln,
           b_gmlp_ln, w_s, b_s, g_gmlp_out, w_out, g_ffn, w_ffn_in, w_ffn_out, w_ple, g_ple, g_ple_gate,
           w_ple_gate, g_final):
    batch, seq, _ = x_prompt.shape
    dec_batch, dec_seq, _ = x_sample.shape
    depth = w_in.shape[0]
    assert depth == 1, "one layer: the prompt / sample passes below are not chained over depth"
    row = lambda a: a.reshape(1, -1)
    w = w_in[0]
    wts = dict(
        g_mix=row(g_mix[0]),
        w_in=w,
        w_gla=w[:, :IN_OFFSETS[4]].astype(BF16),
        w_gmlp=w[:, IN_OFFSETS[5]:].astype(BF16),
        wa2_pad=jnp.pad(w_a2[0], ((0, V7X_LANES - GLA_GATE_RANK), (0, 0))),
        b_a=row(b_a[0]),
        g_gla_norm=row(g_gla_norm[0]),
        g_gmlp_ln=row(g_gmlp_ln[0]),
        b_gmlp_ln=row(b_gmlp_ln[0]),
        w_s=w_s[0],
        b_s=b_s[0],
        g_gmlp_out=row(g_gmlp_out[0]),
        w_out=w_out[0].astype(BF16),
        g_ffn=row(g_ffn[0]),
        w_ffn_in=w_ffn_in[0].astype(BF16),
        w_ffn_out=w_ffn_out[0].astype(BF16),
        w_ple=w_ple[0].astype(BF16),
        g_ple=row(g_ple[0]),
        g_ple_gate=row(g_ple_gate[0]),
        w_ple_gate=w_ple_gate[0].astype(BF16),
        g_final=row(g_final),
    )
    yp, sp, _ = _layer(x_prompt.reshape(batch * seq, D_MODEL), p_prompt[0].reshape(batch * seq, PLE_DIM),
                       None, wts, n_seq=batch, seq_len=seq)
    ys, ss, vs = _layer(x_sample.reshape(dec_batch * dec_seq, D_MODEL),
                        p_sample[0].reshape(dec_batch * dec_seq, PLE_DIM),
                        state_gla[0], wts, n_seq=dec_batch, seq_len=dec_seq)
    return (yp.reshape(batch, seq, D_MODEL), ys.reshape(dec_batch, dec_seq, D_MODEL),
            sp[None], ss[None], vs.reshape(1, dec_batch, dec_seq, GMLP_WIDTH))
```

```python
import functools

import numpy as np

import jax
import jax.numpy as jnp
from jax import lax
from jax.experimental import pallas as pl
from jax.experimental.pallas import tpu as pltpu

F32 = jnp.float32
BF16 = jnp.bfloat16

D_MODEL = 2048
GLA_WIDTH = 1024
GMLP_WIDTH = 1024
GLA_HEADS = 4
GLA_DV = 256
GLA_DK = 128
GLA_KEY_WIDTH = 512
GLA_GATE_RANK = 16
GLA_TAU = 16.0
GLA_CHUNK = 64
GMLP_GROUPS = 8
GMLP_GROUP_DIM = 128
GMLP_CHUNK = 128
D_FF = 5632
PLE_DIM = 256
EPS = 1e-6
IN_OFFSETS = (0, 512, 1024, 2048, 3072, 3088, 4112, 5136)
Z_WIDTH = 5120
Z_GLA_WIDTH = 3072
W_GLA_ROWS = 3200
Z_U_OFF = 3072
Z_VG_OFF = 4096

V7X_LANES = 128
V7X_VMEM_BYTES = 64 * 1024 * 1024
VMEM_LIMIT_BYTES = 56 * 1024 * 1024


def _params(*sem):
    return pltpu.CompilerParams(dimension_semantics=sem, vmem_limit_bytes=VMEM_LIMIT_BYTES)


def _rms(x, g):
    return x * lax.rsqrt(jnp.mean(x * x, axis=-1, keepdims=True) + EPS) * g


def _dot(a, b):
    return jnp.dot(a, b, preferred_element_type=F32)


def _dot_tn(a, b):
    return lax.dot_general(a, b, (((0,), (0,)), ((), ())), preferred_element_type=F32)


def _dot_nt(a, b):
    return lax.dot_general(a, b, (((1,), (1,)), ((), ())), preferred_element_type=F32)


def _inproj_kernel(x_ref, wa_ref, wb_ref, wl_ref, z_ref, alow_ref, xb_scr, r_scr, *, n_a):
    j = pl.program_id(1)

    @pl.when(j == 0)
    def _():
        x = x_ref[...]
        xb = x.astype(BF16)
        xb_scr[...] = xb
        r = lax.rsqrt(jnp.mean(x * x, axis=-1, keepdims=True) + EPS)
        r_scr[...] = r
        alow_ref[...] = r * _dot_nt(xb, wl_ref[...])

    @pl.when(j < n_a)
    def _():
        z_ref[...] = (r_scr[...] * _dot_nt(xb_scr[...], wa_ref[...])).astype(BF16)

    @pl.when(j >= n_a)
    def _():
        z_ref[...] = (r_scr[...] * _dot_nt(xb_scr[...], wb_ref[...])).astype(BF16)


def _inproj(x, wt_gla, wt_gmlp, tm, tn):
    m = x.shape[0]
    n_a = Z_GLA_WIDTH // tn
    n_b = (Z_WIDTH - Z_GLA_WIDTH) // tn
    return pl.pallas_call(
        functools.partial(_inproj_kernel, n_a=n_a),
        grid=(m // tm, n_a + n_b),
        in_specs=[
            pl.BlockSpec((tm, D_MODEL), lambda i, j: (i, 0)),
            pl.BlockSpec((tn, D_MODEL), lambda i, j: (jnp.minimum(j, n_a - 1), 0)),
            pl.BlockSpec((tn, D_MODEL), lambda i, j: (jnp.maximum(j - n_a, 0), 0)),
            pl.BlockSpec((V7X_LANES, D_MODEL), lambda i, j: (IN_OFFSETS[4] // V7X_LANES, 0)),
        ],
        out_specs=[
            pl.BlockSpec((tm, tn), lambda i, j: (i, j)),
            pl.BlockSpec((tm, V7X_LANES), lambda i, j: (i, 0)),
        ],
        out_shape=[
            jax.ShapeDtypeStruct((m, Z_WIDTH), BF16),
            jax.ShapeDtypeStruct((m, V7X_LANES), F32),
        ],
        scratch_shapes=[pltpu.VMEM((tm, D_MODEL), BF16), pltpu.VMEM((tm, 1), F32)],
        compiler_params=_params("parallel", "arbitrary"),
        name="inproj",
    )(x, wt_gla, wt_gmlp, wt_gla)


def _group_mask(R, C):
    t = np.arange(R)[:, None]
    s = np.arange(R)[None, :]
    return ((t // C == s // C) & (s <= t)).astype(np.float32)


def _gla_kernel(*refs, C, R, sequential):
    if sequential:
        (q_ref, k_ref, v_ref, r_ref, alow_ref, wa2_ref, ba_ref, gn_ref, tri_ref, mask_ref,
         o_ref, s_out_ref, s_scr, b_scr, e_scr) = refs
    else:
        (q_ref, k_ref, v_ref, r_ref, alow_ref, wa2_ref, ba_ref, gn_ref, tri_ref, mask_ref, s0_ref,
         o_ref, s_out_ref, b_scr, e_scr) = refs
    n = R // C
    mid = C // 2
    W = GLA_KEY_WIDTH

    x = _dot(alow_ref[...].astype(BF16), wa2_ref[...].astype(BF16)) + ba_ref[...]
    lb = (jnp.minimum(x, 0.0) - jnp.log1p(jnp.exp(-jnp.abs(x)))) / GLA_TAU

    tri = tri_ref[...]
    lb_hi = lb.astype(BF16)
    lb_lo = (lb - lb_hi.astype(F32)).astype(BF16)
    b = _dot(tri, lb_hi) + _dot(tri, lb_lo)
    b3 = b.reshape(n, C, W)
    b_mid = jnp.broadcast_to(b3[:, mid:mid + 1, :], (n, C, W)).reshape(R, W)
    b_last = jnp.broadcast_to(b3[:, C - 1:C, :], (n, C, W)).reshape(R, W)

    q = q_ref[...].astype(F32) * (GLA_DK ** -0.5)
    k = k_ref[...].astype(F32)
    q1 = (q * jnp.exp(b)).astype(BF16)
    q2 = (q * jnp.exp(b - b_mid)).astype(BF16)
    k2 = (k * jnp.exp(b_mid - b)).astype(BF16)
    k3 = k * jnp.exp(b_last - b)

    e_scr[...] = jnp.zeros((V7X_LANES, GLA_DK), F32)
    for h in range(GLA_HEADS):
        b_scr[h] = b[:, h * GLA_DK:(h + 1) * GLA_DK]
        e_scr[h * n:(h + 1) * n, :] = jnp.exp(b_scr[h, pl.ds(C - 1, n, stride=C), :])
    d_cols = e_scr[...].T

    causal = mask_ref[...] > 0.0
    if sequential:
        @pl.when(pl.program_id(1) == 0)
        def _():
            s_scr[...] = jnp.zeros((GLA_HEADS, GLA_DK, GLA_DV), F32)
    else:
        grp = lax.broadcasted_iota(jnp.int32, (R, 1), 0) >> (C.bit_length() - 1)

    for h in range(GLA_HEADS):
        ks = slice(h * GLA_DK, (h + 1) * GLA_DK)
        vs = slice(h * GLA_DV, (h + 1) * GLA_DV)
        v = v_ref[:, vs]
        att = jnp.where(causal, _dot_nt(q2[:, ks], k2[:, ks]), 0.0).astype(BF16)
        o = _dot(att, v)
        q1h = q1[:, ks]
        k3h = k3[:, ks]
        if sequential:
            s = s_scr[h]
            o_parts = []
            for g in range(n):
                rows = slice(g * C, (g + 1) * C)
                o_parts.append(o[rows] + _dot(q1h[rows], s.astype(BF16)))
                s = d_cols[:, h * n + g:h * n + g + 1] * s + _dot_tn(k3h[rows].astype(BF16), v[rows])
            s_scr[h] = s
            s_out_ref[0, h] = s
            o = jnp.concatenate(o_parts, axis=0)
        else:
            for g in range(n):
                s = s0_ref[g, h]
                o = o + jnp.where(grp == g, _dot(q1h, s.astype(BF16)), 0.0)
                k3g = jnp.where(grp == g, k3h, 0.0).astype(BF16)
                s_out_ref[g, h] = d_cols[:, h * n + g:h * n + g + 1] * s + _dot_tn(k3g, v)
        o = _rms(o, gn_ref[:, vs]) * jax.nn.silu(r_ref[:, vs].astype(F32))
        o_ref[:, vs] = o.astype(BF16)


def _gla(z, alow, wa2_pad, b_a, g_norm, s0, *, n_seq, seq_len):
    m = z.shape[0]
    sequential = s0 is None
    if sequential:
        C, R = GLA_CHUNK, 4 * GLA_CHUNK
        nt = seq_len // R
        grid = (n_seq, nt)
        rowblk = lambda a, c: a * nt + c
        s_block = (1, GLA_HEADS, GLA_DK, GLA_DV)
        s_map = lambda a, c: (a, 0, 0, 0)
    else:
        C, R = seq_len, 64
        grid = (m // R, 1)
        rowblk = lambda a, c: a
        s_block = (R // C, GLA_HEADS, GLA_DK, GLA_DV)
        s_map = lambda a, c: (a, 0, 0, 0)
    mask = _group_mask(R, C)
    const = lambda a, c: (0, 0)
    in_specs = [
        pl.BlockSpec((R, GLA_KEY_WIDTH), lambda a, c: (rowblk(a, c), 0)),
        pl.BlockSpec((R, GLA_KEY_WIDTH), lambda a, c: (rowblk(a, c), 1)),
        pl.BlockSpec((R, GLA_WIDTH), lambda a, c: (rowblk(a, c), 1)),
        pl.BlockSpec((R, GLA_WIDTH), lambda a, c: (rowblk(a, c), 2)),
        pl.BlockSpec((R, V7X_LANES), lambda a, c: (rowblk(a, c), 0)),
        pl.BlockSpec((V7X_LANES, GLA_KEY_WIDTH), const),
        pl.BlockSpec((1, GLA_KEY_WIDTH), const),
        pl.BlockSpec((1, GLA_WIDTH), const),
        pl.BlockSpec((R, R), const),
        pl.BlockSpec((R, R), const),
    ]
    args = [z, z, z, z, alow, wa2_pad, b_a, g_norm, jnp.asarray(mask, BF16), jnp.asarray(mask, F32)]
    scratch = [pltpu.VMEM((GLA_HEADS, R, GLA_DK), F32), pltpu.VMEM((V7X_LANES, GLA_DK), F32)]
    if sequential:
        scratch = [pltpu.VMEM((GLA_HEADS, GLA_DK, GLA_DV), F32)] + scratch
    else:
        in_specs.append(pl.BlockSpec(s_block, s_map))
        args.append(s0)
    return pl.pallas_call(
        functools.partial(_gla_kernel, C=C, R=R, sequential=sequential),
        grid=grid,
        in_specs=in_specs,
        out_specs=[
            pl.BlockSpec((R, GLA_WIDTH), lambda a, c: (rowblk(a, c), 0)),
            pl.BlockSpec(s_block, s_map),
        ],
        out_shape=[
            jax.ShapeDtypeStruct((m, GLA_WIDTH), BF16),
            jax.ShapeDtypeStruct((n_seq, GLA_HEADS, GLA_DK, GLA_DV), F32),
        ],
        scratch_shapes=scratch,
        compiler_params=_params("parallel", "arbitrary"),
        name="gla_seq" if sequential else "gla_step",
    )(*args)


def _gmlp_kernel(u_ref, vg_ref, gln_ref, bln_ref, ws_ref, bias_ref, gout_ref, *out_refs, tm, blk, emit_vn):
    m_ref = out_refs[0]
    shift = blk.bit_length() - 1
    row = lax.broadcasted_iota(jnp.int32, (GMLP_CHUNK, GMLP_CHUNK), 0)
    col = lax.broadcasted_iota(jnp.int32, (GMLP_CHUNK, GMLP_CHUNK), 1)
    causal = jnp.logical_and((row >> shift) == (col >> shift), col <= row)
    w = [jnp.where(causal, ws_ref[g], 0.0).astype(BF16) for g in range(GMLP_GROUPS)]
    for c in range(tm // GMLP_CHUNK):
        rows = pl.ds(c * GMLP_CHUNK, GMLP_CHUNK)
        vg = jax.nn.gelu(vg_ref[rows, :].astype(F32))
        mu = jnp.mean(vg, axis=-1, keepdims=True)
        vc = vg - mu
        vn = vc * lax.rsqrt(jnp.mean(vc * vc, axis=-1, keepdims=True) + EPS) * gln_ref[...] + bln_ref[...]
        if emit_vn:
            out_refs[1][rows, :] = vn
        vn16 = vn.astype(BF16)
        mixed = jnp.concatenate(
            [_dot(w[g], vn16[:, g * GMLP_GROUP_DIM:(g + 1) * GMLP_GROUP_DIM]) for g in range(GMLP_GROUPS)],
            axis=1) + bias_ref[...]
        mm = jax.nn.gelu(u_ref[rows, :].astype(F32)) * mixed
        m_ref[rows, :] = _rms(mm, gout_ref[...]).astype(BF16)


def _gmlp(z, g_ln, b_ln, w_mix, bias, g_out, *, tm, blk, emit_vn):
    m = z.shape[0]
    out_specs = [pl.BlockSpec((tm, GMLP_WIDTH), lambda i: (i, 0))]
    out_shape = [jax.ShapeDtypeStruct((m, GMLP_WIDTH), BF16)]
    if emit_vn:
        out_specs.append(pl.BlockSpec((tm, GMLP_WIDTH), lambda i: (i, 0)))
        out_shape.append(jax.ShapeDtypeStruct((m, GMLP_WIDTH), F32))
    return pl.pallas_call(
        functools.partial(_gmlp_kernel, tm=tm, blk=blk, emit_vn=emit_vn),
        grid=(m // tm,),
        in_specs=[
            pl.BlockSpec((tm, GMLP_WIDTH), lambda i: (i, Z_U_OFF // GMLP_WIDTH)),
            pl.BlockSpec((tm, GMLP_WIDTH), lambda i: (i, Z_VG_OFF // GMLP_WIDTH)),
            pl.BlockSpec((1, GMLP_WIDTH), lambda i: (0, 0)),
            pl.BlockSpec((1, GMLP_WIDTH), lambda i: (0, 0)),
            pl.BlockSpec((GMLP_GROUPS, GMLP_CHUNK, GMLP_CHUNK), lambda i: (0, 0, 0)),
            pl.BlockSpec((GMLP_CHUNK, GMLP_WIDTH), lambda i: (0, 0)),
            pl.BlockSpec((1, GMLP_WIDTH), lambda i: (0, 0)),
        ],
        out_specs=out_specs,
        out_shape=out_shape,
        compiler_params=_params("parallel"),
        name="gmlp",
    )(z, z, g_ln, b_ln, w_mix, bias, g_out)


def _outproj_kernel(h_ref, o_ref, m_ref, wo_ref, wm_ref, g_ref, h1_ref, f_ref):
    h1 = h_ref[...] + _dot(o_ref[...], wo_ref[...]) + _dot(m_ref[...], wm_ref[...])
    h1_ref[...] = h1
    f_ref[...] = _rms(h1, g_ref[...]).astype(BF16)


def _outproj(h, o, mm, w_out, g_ffn, tm):
    m = h.shape[0]
    return pl.pallas_call(
        _outproj_kernel,
        grid=(m // tm,),
        in_specs=[
            pl.BlockSpec((tm, D_MODEL), lambda i: (i, 0)),
            pl.BlockSpec((tm, GLA_WIDTH), lambda i: (i, 0)),
            pl.BlockSpec((tm, GMLP_WIDTH), lambda i: (i, 0)),
            pl.BlockSpec((GLA_WIDTH, D_MODEL), lambda i: (0, 0)),
            pl.BlockSpec((GMLP_WIDTH, D_MODEL), lambda i: (1, 0)),
            pl.BlockSpec((1, D_MODEL), lambda i: (0, 0)),
        ],
        out_specs=[
            pl.BlockSpec((tm, D_MODEL), lambda i: (i, 0)),
            pl.BlockSpec((tm, D_MODEL), lambda i: (i, 0)),
        ],
        out_shape=[
            jax.ShapeDtypeStruct((m, D_MODEL), F32),
            jax.ShapeDtypeStruct((m, D_MODEL), BF16),
        ],
        compiler_params=_params("parallel"),
        name="outproj",
    )(h, o, mm, w_out, w_out, g_ffn)


def _ffn_kernel(f_ref, h_ref, wg_ref, wu_ref, wo_ref, o_ref):
    @pl.when(pl.program_id(1) == 0)
    def _():
        o_ref[...] = h_ref[...]

    f = f_ref[...]
    act = (jax.nn.silu(_dot(f, wg_ref[...])) * _dot(f, wu_ref[...])).astype(BF16)
    o_ref[...] += _dot(act, wo_ref[...])


def _ffn(f, h1, w_in, w_out, tm, tf):
    m = f.shape[0]
    nf = D_FF // tf
    return pl.pallas_call(
        _ffn_kernel,
        grid=(m // tm, nf),
        in_specs=[
            pl.BlockSpec((tm, D_MODEL), lambda i, j: (i, 0)),
            pl.BlockSpec((tm, D_MODEL), lambda i, j: (i, 0)),
            pl.BlockSpec((D_MODEL, tf), lambda i, j: (0, j)),
            pl.BlockSpec((D_MODEL, tf), lambda i, j: (0, nf + j)),
            pl.BlockSpec((tf, D_MODEL), lambda i, j: (j, 0)),
        ],
        out_specs=pl.BlockSpec((tm, D_MODEL), lambda i, j: (i, 0)),
        out_shape=jax.ShapeDtypeStruct((m, D_MODEL), F32),
        compiler_params=_params("parallel", "arbitrary"),
        name="ffn",
    )(f, h1, w_in, w_in, w_out)


def _ple_kernel(h_ref, p_ref, wp_ref, gp_ref, gg_ref, wg_ref, gf_ref, y_ref):
    h = h_ref[...]
    pe = _rms(_dot(p_ref[...].astype(BF16), wp_ref[...]), gp_ref[...])
    gate = jax.nn.sigmoid(_dot(_rms(h, gg_ref[...]).astype(BF16), wg_ref[...]))
    y_ref[...] = _rms(h + pe * gate, gf_ref[...])


def _ple(h, p, w_ple, g_ple, g_gate, w_gate, g_final, tm):
    m = h.shape[0]
    vec = pl.BlockSpec((1, D_MODEL), lambda i: (0, 0))
    return pl.pallas_call(
        _ple_kernel,
        grid=(m // tm,),
        in_specs=[
            pl.BlockSpec((tm, D_MODEL), lambda i: (i, 0)),
            pl.BlockSpec((tm, PLE_DIM), lambda i: (i, 0)),
            pl.BlockSpec((PLE_DIM, D_MODEL), lambda i: (0, 0)),
            vec, vec,
            pl.BlockSpec((D_MODEL, D_MODEL), lambda i: (0, 0)),
            vec,
        ],
        out_specs=pl.BlockSpec((tm, D_MODEL), lambda i: (i, 0)),
        out_shape=jax.ShapeDtypeStruct((m, D_MODEL), F32),
        compiler_params=_params("parallel"),
        name="ple",
    )(h, p, w_ple, g_ple, g_gate, w_gate, g_final)


def _layer(x, p, s0, wts, *, n_seq, seq_len):
    z, alow = _inproj(x, wts["wt_gla"], wts["wt_gmlp"], tm=1024, tn=1024)
    o, s_new = _gla(z, alow, wts["wa2_pad"], wts["b_a"], wts["g_gla_norm"], s0, n_seq=n_seq, seq_len=seq_len)
    blk = min(seq_len, GMLP_CHUNK)
    reps = GMLP_CHUNK // blk
    sel = jnp.asarray(np.tile(np.eye(blk, dtype=np.float32), (reps, 1)))
    w_mix = jnp.einsum("ra,gab,cb->grc", sel, wts["w_s"][:, :blk, :blk], sel, precision=lax.Precision.HIGHEST)
    bias = jnp.tile(jnp.repeat(wts["b_s"][:, :blk].T, GMLP_GROUP_DIM, axis=1), (reps, 1))
    emit_vn = s0 is not None
    gm = _gmlp(z, wts["g_gmlp_ln"], wts["b_gmlp_ln"], w_mix, bias, wts["g_gmlp_out"],
               tm=256, blk=blk, emit_vn=emit_vn)
    h1, f = _outproj(x, o, gm[0], wts["w_out"], wts["g_ffn"], tm=512)
    h2 = _ffn(f, h1, wts["w_ffn_in"], wts["w_ffn_out"], tm=512, tf=512)
    y = _ple(h2, p, wts["w_ple"], wts["g_ple"], wts["g_ple_gate"], wts["w_ple_gate"], wts["g_final"], tm=512)
    return y, s_new, (gm[1] if emit_vn else None)


def kernel(x_prompt, x_sample, state_gla, p_prompt, p_sample, g_mix, w_in, w_a2, b_a, g_gla_norm, g_gmlp_ln,
           b_gmlp_ln, w_s, b_s, g_gmlp_out, w_out, g_ffn, w_ffn_in, w_ffn_out, w_ple, g_ple, g_ple_gate,
           w_ple_gate, g_final):
    batch, seq, _ = x_prompt.shape
    dec_batch, dec_seq, _ = x_sample.shape
    depth = w_in.shape[0]
    assert depth == 1, "one layer: the prompt / sample passes below are not chained over depth"
    row = lambda a: a.reshape(1, -1)
    wt = jnp.swapaxes(w_in, 1, 2)[0] * g_mix[0][None, :]
    wts = dict(
        wt_gla=wt[:W_GLA_ROWS].astype(BF16),
        wt_gmlp=wt[IN_OFFSETS[5]:].astype(BF16),
        wa2_pad=jnp.pad(w_a2[0], ((0, V7X_LANES - GLA_GATE_RANK), (0, 0))),
        b_a=row(b_a[0]),
        g_gla_norm=row(g_gla_norm[0]),
        g_gmlp_ln=row(g_gmlp_ln[0]),
        b_gmlp_ln=row(b_gmlp_ln[0]),
        w_s=w_s[0],
        b_s=b_s[0],
        g_gmlp_out=row(g_gmlp_out[0]),
        w_out=w_out[0].astype(BF16),
        g_ffn=row(g_ffn[0]),
        w_ffn_in=w_ffn_in[0].astype(BF16),
        w_ffn_out=w_ffn_out[0].astype(BF16),
        w_ple=w_ple[0].astype(BF16),
        g_ple=row(g_ple[0]),
        g_ple_gate=row(g_ple_gate[0]),
        w_ple_gate=w_ple_gate[0].astype(BF16),
        g_final=row(g_final),
    )
    yp, sp, _ = _layer(x_prompt.reshape(batch * seq, D_MODEL), p_prompt[0].reshape(batch * seq, PLE_DIM),
                       None, wts, n_seq=batch, seq_len=seq)
    ys, ss, vs = _layer(x_sample.reshape(dec_batch * dec_seq, D_MODEL),
                        p_sample[0].reshape(dec_batch * dec_seq, PLE_DIM),
                        state_gla[0], wts, n_seq=dec_batch, seq_len=dec_seq)
    return (yp.reshape(batch, seq, D_MODEL), ys.reshape(dec_batch, dec_seq, D_MODEL),
            sp[None], ss[None], vs.reshape(1, dec_batch, dec_seq, GMLP_WIDTH))
```

```python
import functools

import numpy as np

import jax
import jax.numpy as jnp
from jax import lax
from jax.experimental import pallas as pl
from jax.experimental.pallas import tpu as pltpu

F32 = jnp.float32
BF16 = jnp.bfloat16

D_MODEL = 2048
GLA_WIDTH = 1024
GMLP_WIDTH = 1024
GLA_HEADS = 4
GLA_DV = 256
GLA_DK = 128
GLA_KEY_WIDTH = 512
GLA_GATE_RANK = 16
GLA_TAU = 16.0
GLA_CHUNK = 64
GMLP_GROUPS = 8
GMLP_GROUP_DIM = 128
GMLP_CHUNK = 128
D_FF = 5632
PLE_DIM = 256
EPS = 1e-6
IN_OFFSETS = (0, 512, 1024, 2048, 3072, 3088, 4112, 5136)
Z_WIDTH = 5120
Z_GLA_WIDTH = 3072
Z_U_OFF = 3072
Z_VG_OFF = 4096

V7X_LANES = 128
V7X_VMEM_BYTES = 64 * 1024 * 1024
VMEM_LIMIT_BYTES = 60 * 1024 * 1024


def _params(*sem):
    return pltpu.CompilerParams(dimension_semantics=sem, vmem_limit_bytes=VMEM_LIMIT_BYTES)


def _rms(x, g):
    return x * lax.rsqrt(jnp.mean(x * x, axis=-1, keepdims=True) + EPS) * g


def _dot(a, b):
    return jnp.dot(a, b, preferred_element_type=F32)


def _dot_tn(a, b):
    return lax.dot_general(a, b, (((0,), (0,)), ((), ())), preferred_element_type=F32)


def _dot_nt(a, b):
    return lax.dot_general(a, b, (((1,), (1,)), ((), ())), preferred_element_type=F32)


def _inproj_kernel(x_ref, g_ref, w_ref, wl_ref, z_ref, alow_ref, xb_scr, r_scr):
    j = pl.program_id(1)

    @pl.when(j == 0)
    def _():
        x = x_ref[...]
        xb = (x * g_ref[...]).astype(BF16)
        xb_scr[...] = xb
        r = lax.rsqrt(jnp.mean(x * x, axis=-1, keepdims=True) + EPS)
        r_scr[...] = r
        alow_ref[...] = r * _dot_nt(xb, wl_ref[...])
        z_ref[...] = (r * _dot_nt(xb, w_ref[...])).astype(BF16)

    @pl.when(j > 0)
    def _():
        z_ref[...] = (r_scr[...] * _dot_nt(xb_scr[...], w_ref[...])).astype(BF16)


def _inproj(x, g_mix, wt, tm, tn):
    m = x.shape[0]
    n_a = Z_GLA_WIDTH // tn

    def w_row(j):
        return pl.multiple_of(jnp.where(j < n_a, j * tn, IN_OFFSETS[5] + (j - n_a) * tn), 16)

    return pl.pallas_call(
        _inproj_kernel,
        grid=(m // tm, Z_WIDTH // tn),
        in_specs=[
            pl.BlockSpec((tm, D_MODEL), lambda i, j: (i, 0)),
            pl.BlockSpec((1, D_MODEL), lambda i, j: (0, 0)),
            pl.BlockSpec((pl.Element(tn), pl.Element(D_MODEL)), lambda i, j: (w_row(j), 0)),
            pl.BlockSpec((V7X_LANES, D_MODEL), lambda i, j: (IN_OFFSETS[4] // V7X_LANES, 0)),
        ],
        out_specs=[
            pl.BlockSpec((tm, tn), lambda i, j: (i, j)),
            pl.BlockSpec((tm, V7X_LANES), lambda i, j: (i, 0)),
        ],
        out_shape=[
            jax.ShapeDtypeStruct((m, Z_WIDTH), BF16),
            jax.ShapeDtypeStruct((m, V7X_LANES), F32),
        ],
        scratch_shapes=[pltpu.VMEM((tm, D_MODEL), BF16), pltpu.VMEM((tm, 1), F32)],
        compiler_params=_params("parallel", "arbitrary"),
        name="inproj",
    )(x, g_mix, wt, wt)


N_GLA_IN = 10


def _group_mask(R, C):
    t = np.arange(R)[:, None]
    s = np.arange(R)[None, :]
    return ((t // C == s // C) & (s <= t)).astype(np.float32)


def _gla_kernel(*refs, C, R, sequential, n_cast):
    q_ref, k_ref, v_ref, r_ref, alow_ref, wa2_ref, ba_ref, gn_ref, tri_ref, mask_ref = refs[:N_GLA_IN]
    if sequential:
        cast_in = refs[N_GLA_IN:N_GLA_IN + n_cast]
        o_ref, s_out_ref = refs[N_GLA_IN + n_cast:N_GLA_IN + n_cast + 2]
        cast_out = refs[N_GLA_IN + n_cast + 2:N_GLA_IN + 2 * n_cast + 2]
        s_scr, b_scr, e_scr = refs[N_GLA_IN + 2 * n_cast + 2:]
        for src_ref, dst_ref in zip(cast_in, cast_out):
            dst_ref[...] = src_ref[...].astype(BF16)
    else:
        s0_ref, o_ref, s_out_ref, b_scr, e_scr = refs[N_GLA_IN:]
    n = R // C
    mid = C // 2
    W = GLA_KEY_WIDTH

    x = _dot(alow_ref[...].astype(BF16), wa2_ref[...].astype(BF16)) + ba_ref[...]
    lb = (jnp.minimum(x, 0.0) - jnp.log1p(jnp.exp(-jnp.abs(x)))) / GLA_TAU

    tri = tri_ref[...]
    lb_hi = lb.astype(BF16)
    lb_lo = (lb - lb_hi.astype(F32)).astype(BF16)
    b = _dot(tri, lb_hi) + _dot(tri, lb_lo)
    b3 = b.reshape(n, C, W)
    b_mid = jnp.broadcast_to(b3[:, mid:mid + 1, :], (n, C, W)).reshape(R, W)
    b_last = jnp.broadcast_to(b3[:, C - 1:C, :], (n, C, W)).reshape(R, W)

    q = q_ref[...].astype(F32) * (GLA_DK ** -0.5)
    k = k_ref[...].astype(F32)
    q1 = (q * jnp.exp(b)).astype(BF16)
    q2 = (q * jnp.exp(b - b_mid)).astype(BF16)
    k2 = (k * jnp.exp(b_mid - b)).astype(BF16)
    k3 = k * jnp.exp(b_last - b)

    e_scr[...] = jnp.zeros((V7X_LANES, GLA_DK), F32)
    for h in range(GLA_HEADS):
        b_scr[h] = b[:, h * GLA_DK:(h + 1) * GLA_DK]
        e_scr[h * n:(h + 1) * n, :] = jnp.exp(b_scr[h, pl.ds(C - 1, n, stride=C), :])
    d_cols = e_scr[...].T

    causal = mask_ref[...] > 0.0
    if sequential:
        @pl.when(pl.program_id(1) == 0)
        def _():
            s_scr[...] = jnp.zeros((GLA_HEADS, GLA_DK, GLA_DV), F32)
    else:
        grp = lax.broadcasted_iota(jnp.int32, (R, 1), 0) >> (C.bit_length() - 1)

    for h in range(GLA_HEADS):
        ks = slice(h * GLA_DK, (h + 1) * GLA_DK)
        vs = slice(h * GLA_DV, (h + 1) * GLA_DV)
        v = v_ref[:, vs]
        att = jnp.where(causal, _dot_nt(q2[:, ks], k2[:, ks]), 0.0).astype(BF16)
        o = _dot(att, v)
        q1h = q1[:, ks]
        k3h = k3[:, ks]
        if sequential:
            s = s_scr[h]
            o_parts = []
            for g in range(n):
                rows = slice(g * C, (g + 1) * C)
                o_parts.append(o[rows] + _dot(q1h[rows], s.astype(BF16)))
                s = d_cols[:, h * n + g:h * n + g + 1] * s + _dot_tn(k3h[rows].astype(BF16), v[rows])
            s_scr[h] = s
            s_out_ref[0, h] = s
            o = jnp.concatenate(o_parts, axis=0)
        else:
            for g in range(n):
                s = s0_ref[g, h]
                o = o + jnp.where(grp == g, _dot(q1h, s.astype(BF16)), 0.0)
                k3g = jnp.where(grp == g, k3h, 0.0).astype(BF16)
                s_out_ref[g, h] = d_cols[:, h * n + g:h * n + g + 1] * s + _dot_tn(k3g, v)
        o = _rms(o, gn_ref[:, vs]) * jax.nn.silu(r_ref[:, vs].astype(F32))
        o_ref[:, vs] = o.astype(BF16)


def _gla(z, alow, wa2_pad, b_a, g_norm, s0, *, n_seq, seq_len, cast=()):
    m = z.shape[0]
    sequential = s0 is None
    assert sequential or not cast
    if sequential:
        C, R = GLA_CHUNK, 4 * GLA_CHUNK
        nt = seq_len // R
        grid = (n_seq, nt)
        rowblk = lambda a, c: a * nt + c
        s_block = (1, GLA_HEADS, GLA_DK, GLA_DV)
    else:
        C, R = seq_len, 64
        grid = (m // R, 1)
        rowblk = lambda a, c: a
        s_block = (R // C, GLA_HEADS, GLA_DK, GLA_DV)
    s_map = lambda a, c: (a, 0, 0, 0)
    mask = _group_mask(R, C)
    const = lambda a, c: (0, 0)
    in_specs = [
        pl.BlockSpec((R, GLA_KEY_WIDTH), lambda a, c: (rowblk(a, c), 0)),
        pl.BlockSpec((R, GLA_KEY_WIDTH), lambda a, c: (rowblk(a, c), 1)),
        pl.BlockSpec((R, GLA_WIDTH), lambda a, c: (rowblk(a, c), 1)),
        pl.BlockSpec((R, GLA_WIDTH), lambda a, c: (rowblk(a, c), 2)),
        pl.BlockSpec((R, V7X_LANES), lambda a, c: (rowblk(a, c), 0)),
        pl.BlockSpec((V7X_LANES, GLA_KEY_WIDTH), const),
        pl.BlockSpec((1, GLA_KEY_WIDTH), const),
        pl.BlockSpec((1, GLA_WIDTH), const),
        pl.BlockSpec((R, R), const),
        pl.BlockSpec((R, R), const),
    ]
    args = [z, z, z, z, alow, wa2_pad, b_a, g_norm, jnp.asarray(mask, BF16), jnp.asarray(mask, F32)]
    assert len(args) == N_GLA_IN
    scratch = [pltpu.VMEM((GLA_HEADS, R, GLA_DK), F32), pltpu.VMEM((V7X_LANES, GLA_DK), F32)]
    out_specs = [
        pl.BlockSpec((R, GLA_WIDTH), lambda a, c: (rowblk(a, c), 0)),
        pl.BlockSpec(s_block, s_map),
    ]
    out_shape = [
        jax.ShapeDtypeStruct((m, GLA_WIDTH), BF16),
        jax.ShapeDtypeStruct((n_seq, GLA_HEADS, GLA_DK, GLA_DV), F32),
    ]
    if sequential:
        scratch = [pltpu.VMEM((GLA_HEADS, GLA_DK, GLA_DV), F32)] + scratch
        n_steps = grid[0] * grid[1]
        for w in cast:
            spec = pl.BlockSpec((w.shape[0] // n_steps, w.shape[1]), lambda a, c: (rowblk(a, c), 0))
            in_specs.append(spec)
            out_specs.append(spec)
            out_shape.append(jax.ShapeDtypeStruct(w.shape, BF16))
        args.extend(cast)
    else:
        in_specs.append(pl.BlockSpec(s_block, s_map))
        args.append(s0)
    return pl.pallas_call(
        functools.partial(_gla_kernel, C=C, R=R, sequential=sequential, n_cast=len(cast)),
        grid=grid,
        in_specs=in_specs,
        out_specs=out_specs,
        out_shape=out_shape,
        scratch_shapes=scratch,
        compiler_params=_params("parallel", "arbitrary"),
        name="gla_seq" if sequential else "gla_step",
    )(*args)


def _gmlp_kernel(u_ref, vg_ref, gln_ref, bln_ref, ws_ref, bias_ref, gout_ref, *out_refs, tm, blk, emit_vn):
    m_ref = out_refs[0]
    shift = blk.bit_length() - 1
    row = lax.broadcasted_iota(jnp.int32, (GMLP_CHUNK, GMLP_CHUNK), 0)
    col = lax.broadcasted_iota(jnp.int32, (GMLP_CHUNK, GMLP_CHUNK), 1)
    causal = jnp.logical_and((row >> shift) == (col >> shift), col <= row)
    w = [jnp.where(causal, ws_ref[g], 0.0).astype(BF16) for g in range(GMLP_GROUPS)]
    for c in range(tm // GMLP_CHUNK):
        rows = pl.ds(c * GMLP_CHUNK, GMLP_CHUNK)
        vg = jax.nn.gelu(vg_ref[rows, :].astype(F32))
        mu = jnp.mean(vg, axis=-1, keepdims=True)
        vc = vg - mu
        vn = vc * lax.rsqrt(jnp.mean(vc * vc, axis=-1, keepdims=True) + EPS) * gln_ref[...] + bln_ref[...]
        if emit_vn:
            out_refs[1][rows, :] = vn
        vn16 = vn.astype(BF16)
        mixed = jnp.concatenate(
            [_dot(w[g], vn16[:, g * GMLP_GROUP_DIM:(g + 1) * GMLP_GROUP_DIM]) for g in range(GMLP_GROUPS)],
            axis=1) + bias_ref[...]
        mm = jax.nn.gelu(u_ref[rows, :].astype(F32)) * mixed
        m_ref[rows, :] = _rms(mm, gout_ref[...]).astype(BF16)


def _gmlp(z, g_ln, b_ln, w_mix, bias, g_out, *, tm, blk, emit_vn):
    m = z.shape[0]
    out_specs = [pl.BlockSpec((tm, GMLP_WIDTH), lambda i: (i, 0))]
    out_shape = [jax.ShapeDtypeStruct((m, GMLP_WIDTH), BF16)]
    if emit_vn:
        out_specs.append(pl.BlockSpec((tm, GMLP_WIDTH), lambda i: (i, 0)))
        out_shape.append(jax.ShapeDtypeStruct((m, GMLP_WIDTH), F32))
    return pl.pallas_call(
        functools.partial(_gmlp_kernel, tm=tm, blk=blk, emit_vn=emit_vn),
        grid=(m // tm,),
        in_specs=[
            pl.BlockSpec((tm, GMLP_WIDTH), lambda i: (i, Z_U_OFF // GMLP_WIDTH)),
            pl.BlockSpec((tm, GMLP_WIDTH), lambda i: (i, Z_VG_OFF // GMLP_WIDTH)),
            pl.BlockSpec((1, GMLP_WIDTH), lambda i: (0, 0)),
            pl.BlockSpec((1, GMLP_WIDTH), lambda i: (0, 0)),
            pl.BlockSpec((GMLP_GROUPS, GMLP_CHUNK, GMLP_CHUNK), lambda i: (0, 0, 0)),
            pl.BlockSpec((GMLP_CHUNK, GMLP_WIDTH), lambda i: (0, 0)),
            pl.BlockSpec((1, GMLP_WIDTH), lambda i: (0, 0)),
        ],
        out_specs=out_specs,
        out_shape=out_shape,
        compiler_params=_params("parallel"),
        name="gmlp",
    )(z, z, g_ln, b_ln, w_mix, bias, g_out)


def _outproj_kernel(h_ref, o_ref, m_ref, wo_ref, wm_ref, h1_ref):
    h1_ref[...] = h_ref[...] + _dot(o_ref[...], wo_ref[...]) + _dot(m_ref[...], wm_ref[...])


def _outproj(h, o, mm, w_out, tm):
    m = h.shape[0]
    return pl.pallas_call(
        _outproj_kernel,
        grid=(m // tm,),
        in_specs=[
            pl.BlockSpec((tm, D_MODEL), lambda i: (i, 0)),
            pl.BlockSpec((tm, GLA_WIDTH), lambda i: (i, 0)),
            pl.BlockSpec((tm, GMLP_WIDTH), lambda i: (i, 0)),
            pl.BlockSpec((GLA_WIDTH, D_MODEL), lambda i: (0, 0)),
            pl.BlockSpec((GMLP_WIDTH, D_MODEL), lambda i: (1, 0)),
        ],
        out_specs=pl.BlockSpec((tm, D_MODEL), lambda i: (i, 0)),
        out_shape=jax.ShapeDtypeStruct((m, D_MODEL), F32),
        compiler_params=_params("parallel"),
        name="outproj",
    )(h, o, mm, w_out, w_out)


def _ffn_kernel(h_ref, g_ref, wg_ref, wu_ref, wo_ref, o_ref, hb_scr, r_scr):
    j = pl.program_id(1)

    def tile(hb, r):
        gate = r * _dot(hb, wg_ref[...])
        up = r * _dot(hb, wu_ref[...])
        return _dot((jax.nn.silu(gate) * up).astype(BF16), wo_ref[...])

    @pl.when(j == 0)
    def _():
        h = h_ref[...]
        hb = (h * g_ref[...]).astype(BF16)
        hb_scr[...] = hb
        r = lax.rsqrt(jnp.mean(h * h, axis=-1, keepdims=True) + EPS)
        r_scr[...] = r
        o_ref[...] = h + tile(hb, r)

    @pl.when(j > 0)
    def _():
        o_ref[...] += tile(hb_scr[...], r_scr[...])


def _ffn(h1, g_ffn, w_in, w_out, tm, tf):
    m = h1.shape[0]
    nf = D_FF // tf
    return pl.pallas_call(
        _ffn_kernel,
        grid=(m // tm, nf),
        in_specs=[
            pl.BlockSpec((tm, D_MODEL), lambda i, j: (i, 0)),
            pl.BlockSpec((1, D_MODEL), lambda i, j: (0, 0)),
            pl.BlockSpec((D_MODEL, tf), lambda i, j: (0, j)),
            pl.BlockSpec((D_MODEL, tf), lambda i, j: (0, nf + j)),
            pl.BlockSpec((tf, D_MODEL), lambda i, j: (j, 0)),
        ],
        out_specs=pl.BlockSpec((tm, D_MODEL), lambda i, j: (i, 0)),
        out_shape=jax.ShapeDtypeStruct((m, D_MODEL), F32),
        scratch_shapes=[pltpu.VMEM((tm, D_MODEL), BF16), pltpu.VMEM((tm, 1), F32)],
        compiler_params=_params("parallel", "arbitrary"),
        name="ffn",
    )(h1, g_ffn, w_in, w_in, w_out)


def _ple_kernel(h_ref, p_ref, wp_ref, gp_ref, gg_ref, wg_ref, gf_ref, y_ref):
    h = h_ref[...]
    r = lax.rsqrt(jnp.mean(h * h, axis=-1, keepdims=True) + EPS)
    gate = jax.nn.sigmoid(r * _dot((h * gg_ref[...]).astype(BF16), wg_ref[...]))
    pe = _rms(_dot(p_ref[...].astype(BF16), wp_ref[...]), gp_ref[...])
    y_ref[...] = _rms(h + pe * gate, gf_ref[...])


def _ple(h, p, w_ple, g_ple, g_gate, w_gate, g_final, tm):
    m = h.shape[0]
    vec = pl.BlockSpec((1, D_MODEL), lambda i: (0, 0))
    return pl.pallas_call(
        _ple_kernel,
        grid=(m // tm,),
        in_specs=[
            pl.BlockSpec((tm, D_MODEL), lambda i: (i, 0)),
            pl.BlockSpec((tm, PLE_DIM), lambda i: (i, 0)),
            pl.BlockSpec((PLE_DIM, D_MODEL), lambda i: (0, 0)),
            vec, vec,
            pl.BlockSpec((D_MODEL, D_MODEL), lambda i: (0, 0)),
            vec,
        ],
        out_specs=pl.BlockSpec((tm, D_MODEL), lambda i: (i, 0)),
        out_shape=jax.ShapeDtypeStruct((m, D_MODEL), F32),
        compiler_params=_params("parallel"),
        name="ple",
    )(h, p, w_ple, g_ple, g_gate, w_gate, g_final)


CAST_NAMES = ("w_out", "w_ffn_in", "w_ffn_out", "w_ple_gate")


def _layer(x, p, s0, wts, cast=()):
    n_seq, seq_len = p.shape[:2]
    x = x.reshape(n_seq * seq_len, D_MODEL)
    p = p.reshape(n_seq * seq_len, PLE_DIM)
    z, alow = _inproj(x, wts["g_mix"], wts["wt_in"], tm=1024, tn=1024)
    o, s_new, *cast16 = _gla(z, alow, wts["wa2_pad"], wts["b_a"], wts["g_gla_norm"], s0,
                             n_seq=n_seq, seq_len=seq_len, cast=cast)
    if cast:
        wts = dict(wts, **dict(zip(CAST_NAMES, cast16)))
    blk = min(seq_len, GMLP_CHUNK)
    reps = GMLP_CHUNK // blk
    sel = jnp.asarray(np.tile(np.eye(blk, dtype=np.float32), (reps, 1)))
    w_mix = jnp.einsum("ra,gab,cb->grc", sel, wts["w_s"][:, :blk, :blk], sel, precision=lax.Precision.HIGHEST)
    bias = jnp.tile(jnp.repeat(wts["b_s"][:, :blk].T, GMLP_GROUP_DIM, axis=1), (reps, 1))
    emit_vn = s0 is not None
    gm = _gmlp(z, wts["g_gmlp_ln"], wts["b_gmlp_ln"], w_mix, bias, wts["g_gmlp_out"],
               tm=256, blk=blk, emit_vn=emit_vn)
    h1 = _outproj(x, o, gm[0], wts["w_out"], tm=512)
    h2 = _ffn(h1, wts["g_ffn"], wts["w_ffn_in"], wts["w_ffn_out"], tm=1024, tf=512)
    y = _ple(h2, p, wts["w_ple"], wts["g_ple"], wts["g_ple_gate"], wts["w_ple_gate"], wts["g_final"], tm=512)
    return y.reshape(n_seq, seq_len, D_MODEL), s_new, (gm[1] if emit_vn else None), wts


def kernel(x_prompt, x_sample, state_gla, p_prompt, p_sample, g_mix, w_in, w_a2, b_a, g_gla_norm, g_gmlp_ln,
           b_gmlp_ln, w_s, b_s, g_gmlp_out, w_out, g_ffn, w_ffn_in, w_ffn_out, w_ple, g_ple, g_ple_gate,
           w_ple_gate, g_final):
    dec_batch, dec_seq, _ = x_sample.shape
    depth = w_in.shape[0]
    assert depth == 1, "one layer: the prompt / sample passes below are not chained over depth"
    row = lambda a: a.reshape(1, -1)
    wts = dict(
        g_mix=row(g_mix[0]),
        wt_in=jnp.swapaxes(w_in, 1, 2)[0].astype(BF16),
        wa2_pad=jnp.pad(w_a2[0], ((0, V7X_LANES - GLA_GATE_RANK), (0, 0))),
        b_a=row(b_a[0]),
        g_gla_norm=row(g_gla_norm[0]),
        g_gmlp_ln=row(g_gmlp_ln[0]),
        b_gmlp_ln=row(b_gmlp_ln[0]),
        w_s=w_s[0],
        b_s=b_s[0],
        g_gmlp_out=row(g_gmlp_out[0]),
        g_ffn=row(g_ffn[0]),
        w_ple=w_ple[0].astype(BF16),
        g_ple=row(g_ple[0]),
        g_ple_gate=row(g_ple_gate[0]),
        g_final=row(g_final),
    )
    yp, sp, _, wts = _layer(x_prompt, p_prompt[0], None, wts,
                            cast=(w_out[0], w_ffn_in[0], w_ffn_out[0], w_ple_gate[0]))
    ys, ss, vs, _ = _layer(x_sample, p_sample[0], state_gla[0], wts)
    return (yp, ys, sp[None], ss[None], vs.reshape(1, dec_batch, dec_seq, GMLP_WIDTH))
```

```python
import functools

import numpy as np

import jax
import jax.numpy as jnp
from jax import lax
from jax.experimental import pallas as pl
from jax.experimental.pallas import tpu as pltpu

F32 = jnp.float32
BF16 = jnp.bfloat16

D_MODEL = 2048
GLA_WIDTH = 1024
GMLP_WIDTH = 1024
GLA_HEADS = 4
GLA_DV = 256
GLA_DK = 128
GLA_KEY_WIDTH = 512
GLA_GATE_RANK = 16
GLA_TAU = 16.0
GLA_CHUNK = 64
GMLP_GROUPS = 8
GMLP_GROUP_DIM = 128
GMLP_CHUNK = 128
D_FF = 5632
PLE_DIM = 256
EPS = 1e-6
IN_OFFSETS = (0, 512, 1024, 2048, 3072, 3088, 4112, 5136)
Z_WIDTH = 5120
Z_GLA_WIDTH = 3072
Z_U_OFF = 3072
Z_VG_OFF = 4096

V7X_LANES = 128
V7X_VMEM_BYTES = 64 * 1024 * 1024
VMEM_LIMIT_BYTES = 60 * 1024 * 1024


def _params(*sem):
    return pltpu.CompilerParams(dimension_semantics=sem, vmem_limit_bytes=VMEM_LIMIT_BYTES)


def _rms(x, g):
    return x * lax.rsqrt(jnp.mean(x * x, axis=-1, keepdims=True) + EPS) * g


def _dot(a, b):
    return jnp.dot(a, b, preferred_element_type=F32)


def _dot_tn(a, b):
    return lax.dot_general(a, b, (((0,), (0,)), ((), ())), preferred_element_type=F32)


def _dot_nt(a, b):
    return lax.dot_general(a, b, (((1,), (1,)), ((), ())), preferred_element_type=F32)


def _inproj_kernel(x_ref, g_ref, w_ref, wl_ref, z_ref, alow_ref, xb_scr, r_scr):
    j = pl.program_id(1)

    @pl.when(j == 0)
    def _():
        x = x_ref[...]
        xb = (x * g_ref[...]).astype(BF16)
        xb_scr[...] = xb
        r = lax.rsqrt(jnp.mean(x * x, axis=-1, keepdims=True) + EPS)
        r_scr[...] = r
        alow_ref[...] = r * _dot_nt(xb, wl_ref[...])
        z_ref[...] = (r * _dot_nt(xb, w_ref[...])).astype(BF16)

    @pl.when(j > 0)
    def _():
        z_ref[...] = (r_scr[...] * _dot_nt(xb_scr[...], w_ref[...])).astype(BF16)


def _inproj(x, g_mix, wt, tm, tn):
    m = x.shape[0]
    n_a = Z_GLA_WIDTH // tn

    def w_row(j):
        return pl.multiple_of(jnp.where(j < n_a, j * tn, IN_OFFSETS[5] + (j - n_a) * tn), 16)

    return pl.pallas_call(
        _inproj_kernel,
        grid=(m // tm, Z_WIDTH // tn),
        in_specs=[
            pl.BlockSpec((tm, D_MODEL), lambda i, j: (i, 0)),
            pl.BlockSpec((1, D_MODEL), lambda i, j: (0, 0)),
            pl.BlockSpec((pl.Element(tn), pl.Element(D_MODEL)), lambda i, j: (w_row(j), 0)),
            pl.BlockSpec((V7X_LANES, D_MODEL), lambda i, j: (IN_OFFSETS[4] // V7X_LANES, 0)),
        ],
        out_specs=[
            pl.BlockSpec((tm, tn), lambda i, j: (i, j)),
            pl.BlockSpec((tm, V7X_LANES), lambda i, j: (i, 0)),
        ],
        out_shape=[
            jax.ShapeDtypeStruct((m, Z_WIDTH), BF16),
            jax.ShapeDtypeStruct((m, V7X_LANES), F32),
        ],
        scratch_shapes=[pltpu.VMEM((tm, D_MODEL), BF16), pltpu.VMEM((tm, 1), F32)],
        compiler_params=_params("parallel", "arbitrary"),
        name="inproj",
    )(x, g_mix, wt, wt)


N_GLA_IN = 10


def _group_mask(R, C):
    t = np.arange(R)[:, None]
    s = np.arange(R)[None, :]
    return ((t // C == s // C) & (s <= t)).astype(np.float32)


def _gla_kernel(*refs, C, R, sequential, n_cast):
    q_ref, k_ref, v_ref, r_ref, alow_ref, wa2_ref, ba_ref, gn_ref, tri_ref, mask_ref = refs[:N_GLA_IN]
    if sequential:
        cast_in = refs[N_GLA_IN:N_GLA_IN + n_cast]
        o_ref, s_out_ref = refs[N_GLA_IN + n_cast:N_GLA_IN + n_cast + 2]
        cast_out = refs[N_GLA_IN + n_cast + 2:N_GLA_IN + 2 * n_cast + 2]
        s_scr, b_scr, e_scr = refs[N_GLA_IN + 2 * n_cast + 2:]
        for src_ref, dst_ref in zip(cast_in, cast_out):
            dst_ref[...] = src_ref[...].astype(BF16)
    else:
        s0_ref, o_ref, s_out_ref, b_scr, e_scr = refs[N_GLA_IN:]
    n = R // C
    mid = C // 2
    W = GLA_KEY_WIDTH

    x = _dot(alow_ref[...].astype(BF16), wa2_ref[...].astype(BF16)) + ba_ref[...]
    lb = (jnp.minimum(x, 0.0) - jnp.log1p(jnp.exp(-jnp.abs(x)))) / GLA_TAU

    tri = tri_ref[...]
    lb_hi = lb.astype(BF16)
    lb_lo = (lb - lb_hi.astype(F32)).astype(BF16)
    b = _dot(tri, lb_hi) + _dot(tri, lb_lo)
    b3 = b.reshape(n, C, W)
    b_mid = jnp.broadcast_to(b3[:, mid:mid + 1, :], (n, C, W)).reshape(R, W)
    b_last = jnp.broadcast_to(b3[:, C - 1:C, :], (n, C, W)).reshape(R, W)

    q = q_ref[...].astype(F32) * (GLA_DK ** -0.5)
    k = k_ref[...].astype(F32)
    q1 = (q * jnp.exp(b)).astype(BF16)
    q2 = (q * jnp.exp(b - b_mid)).astype(BF16)
    k2 = (k * jnp.exp(b_mid - b)).astype(BF16)
    k3 = k * jnp.exp(b_last - b)

    e_scr[...] = jnp.zeros((V7X_LANES, GLA_DK), F32)
    for h in range(GLA_HEADS):
        b_scr[h] = b[:, h * GLA_DK:(h + 1) * GLA_DK]
        e_scr[h * n:(h + 1) * n, :] = jnp.exp(b_scr[h, pl.ds(C - 1, n, stride=C), :])
    d_cols = e_scr[...].T

    causal = mask_ref[...] > 0.0
    if sequential:
        @pl.when(pl.program_id(1) == 0)
        def _():
            s_scr[...] = jnp.zeros((GLA_HEADS, GLA_DK, GLA_DV), F32)
    else:
        grp = lax.broadcasted_iota(jnp.int32, (R, 1), 0) >> (C.bit_length() - 1)

    for h in range(GLA_HEADS):
        ks = slice(h * GLA_DK, (h + 1) * GLA_DK)
        vs = slice(h * GLA_DV, (h + 1) * GLA_DV)
        v = v_ref[:, vs]
        att = jnp.where(causal, _dot_nt(q2[:, ks], k2[:, ks]), 0.0).astype(BF16)
        o = _dot(att, v)
        q1h = q1[:, ks]
        k3h = k3[:, ks]
        if sequential:
            s = s_scr[h]
            o_parts = []
            for g in range(n):
                rows = slice(g * C, (g + 1) * C)
                o_parts.append(o[rows] + _dot(q1h[rows], s.astype(BF16)))
                s = d_cols[:, h * n + g:h * n + g + 1] * s + _dot_tn(k3h[rows].astype(BF16), v[rows])
            s_scr[h] = s
            s_out_ref[0, h] = s
            o = jnp.concatenate(o_parts, axis=0)
        else:
            for g in range(n):
                s = s0_ref[g, h]
                o = o + jnp.where(grp == g, _dot(q1h, s.astype(BF16)), 0.0)
                k3g = jnp.where(grp == g, k3h, 0.0).astype(BF16)
                s_out_ref[g, h] = d_cols[:, h * n + g:h * n + g + 1] * s + _dot_tn(k3g, v)
        o = _rms(o, gn_ref[:, vs]) * jax.nn.silu(r_ref[:, vs].astype(F32))
        o_ref[:, vs] = o.astype(BF16)


def _gla(z, alow, wa2_pad, b_a, g_norm, s0, *, n_seq, seq_len, cast=()):
    m = z.shape[0]
    sequential = s0 is None
    assert sequential or not cast
    if sequential:
        C, R = GLA_CHUNK, 4 * GLA_CHUNK
        nt = seq_len // R
        grid = (n_seq, nt)
        rowblk = lambda a, c: a * nt + c
        s_block = (1, GLA_HEADS, GLA_DK, GLA_DV)
    else:
        C, R = seq_len, 64
        grid = (m // R, 1)
        rowblk = lambda a, c: a
        s_block = (R // C, GLA_HEADS, GLA_DK, GLA_DV)
    s_map = lambda a, c: (a, 0, 0, 0)
    mask = _group_mask(R, C)
    const = lambda a, c: (0, 0)
    in_specs = [
        pl.BlockSpec((R, GLA_KEY_WIDTH), lambda a, c: (rowblk(a, c), 0)),
        pl.BlockSpec((R, GLA_KEY_WIDTH), lambda a, c: (rowblk(a, c), 1)),
        pl.BlockSpec((R, GLA_WIDTH), lambda a, c: (rowblk(a, c), 1)),
        pl.BlockSpec((R, GLA_WIDTH), lambda a, c: (rowblk(a, c), 2)),
        pl.BlockSpec((R, V7X_LANES), lambda a, c: (rowblk(a, c), 0)),
        pl.BlockSpec((V7X_LANES, GLA_KEY_WIDTH), const),
        pl.BlockSpec((1, GLA_KEY_WIDTH), const),
        pl.BlockSpec((1, GLA_WIDTH), const),
        pl.BlockSpec((R, R), const),
        pl.BlockSpec((R, R), const),
    ]
    args = [z, z, z, z, alow, wa2_pad, b_a, g_norm, jnp.asarray(mask, BF16), jnp.asarray(mask, F32)]
    assert len(args) == N_GLA_IN
    scratch = [pltpu.VMEM((GLA_HEADS, R, GLA_DK), F32), pltpu.VMEM((V7X_LANES, GLA_DK), F32)]
    out_specs = [
        pl.BlockSpec((R, GLA_WIDTH), lambda a, c: (rowblk(a, c), 0)),
        pl.BlockSpec(s_block, s_map),
    ]
    out_shape = [
        jax.ShapeDtypeStruct((m, GLA_WIDTH), BF16),
        jax.ShapeDtypeStruct((n_seq, GLA_HEADS, GLA_DK, GLA_DV), F32),
    ]
    if sequential:
        scratch = [pltpu.VMEM((GLA_HEADS, GLA_DK, GLA_DV), F32)] + scratch
        n_steps = grid[0] * grid[1]
        for w in cast:
            spec = pl.BlockSpec((w.shape[0] // n_steps, w.shape[1]), lambda a, c: (rowblk(a, c), 0))
            in_specs.append(spec)
            out_specs.append(spec)
            out_shape.append(jax.ShapeDtypeStruct(w.shape, BF16))
        args.extend(cast)
    else:
        in_specs.append(pl.BlockSpec(s_block, s_map))
        args.append(s0)
    return pl.pallas_call(
        functools.partial(_gla_kernel, C=C, R=R, sequential=sequential, n_cast=len(cast)),
        grid=grid,
        in_specs=in_specs,
        out_specs=out_specs,
        out_shape=out_shape,
        scratch_shapes=scratch,
        compiler_params=_params("parallel", "arbitrary"),
        name="gla_seq" if sequential else "gla_step",
    )(*args)


def _mixout_kernel(h_ref, o_ref, u_ref, vg_ref, gln_ref, bln_ref, ws_ref, bias_ref, gout_ref, wo_ref, wm_ref,
                   h1_ref, *vn_refs, tm, blk, emit_vn):
    n_chunks = tm // GMLP_CHUNK
    col_w = D_MODEL // n_chunks
    shift = blk.bit_length() - 1
    row = lax.broadcasted_iota(jnp.int32, (GMLP_CHUNK, GMLP_CHUNK), 0)
    col = lax.broadcasted_iota(jnp.int32, (GMLP_CHUNK, GMLP_CHUNK), 1)
    causal = jnp.logical_and((row >> shift) == (col >> shift), col <= row)
    w = [jnp.where(causal, ws_ref[g], 0.0).astype(BF16) for g in range(GMLP_GROUPS)]
    m_parts = []
    for c in range(n_chunks):
        cols = slice(c * col_w, (c + 1) * col_w)
        h1_ref[:, cols] = h_ref[:, cols] + _dot(o_ref[...], wo_ref[:, cols])
        rows = pl.ds(c * GMLP_CHUNK, GMLP_CHUNK)
        vg = jax.nn.gelu(vg_ref[rows, :].astype(F32))
        mu = jnp.mean(vg, axis=-1, keepdims=True)
        vc = vg - mu
        vn = vc * lax.rsqrt(jnp.mean(vc * vc, axis=-1, keepdims=True) + EPS) * gln_ref[...] + bln_ref[...]
        if emit_vn:
            vn_refs[0][rows, :] = vn
        vn16 = vn.astype(BF16)
        mixed = jnp.concatenate(
            [_dot(w[g], vn16[:, g * GMLP_GROUP_DIM:(g + 1) * GMLP_GROUP_DIM]) for g in range(GMLP_GROUPS)],
            axis=1) + bias_ref[...]
        mm = jax.nn.gelu(u_ref[rows, :].astype(F32)) * mixed
        m_parts.append(_rms(mm, gout_ref[...]).astype(BF16))
    h1_ref[...] += _dot(jnp.concatenate(m_parts, axis=0), wm_ref[...])


def _mixout(h, o, z, g_ln, b_ln, w_mix, bias, g_out, w_out, *, tm, blk, emit_vn):
    m = z.shape[0]
    vec = pl.BlockSpec((1, GMLP_WIDTH), lambda i: (0, 0))
    out_specs = [pl.BlockSpec((tm, D_MODEL), lambda i: (i, 0))]
    out_shape = [jax.ShapeDtypeStruct((m, D_MODEL), F32)]
    if emit_vn:
        out_specs.append(pl.BlockSpec((tm, GMLP_WIDTH), lambda i: (i, 0)))
        out_shape.append(jax.ShapeDtypeStruct((m, GMLP_WIDTH), F32))
    return pl.pallas_call(
        functools.partial(_mixout_kernel, tm=tm, blk=blk, emit_vn=emit_vn),
        grid=(m // tm,),
        in_specs=[
            pl.BlockSpec((tm, D_MODEL), lambda i: (i, 0)),
            pl.BlockSpec((tm, GLA_WIDTH), lambda i: (i, 0)),
            pl.BlockSpec((tm, GMLP_WIDTH), lambda i: (i, Z_U_OFF // GMLP_WIDTH)),
            pl.BlockSpec((tm, GMLP_WIDTH), lambda i: (i, Z_VG_OFF // GMLP_WIDTH)),
            vec, vec,
            pl.BlockSpec((GMLP_GROUPS, GMLP_CHUNK, GMLP_CHUNK), lambda i: (0, 0, 0)),
            pl.BlockSpec((GMLP_CHUNK, GMLP_WIDTH), lambda i: (0, 0)),
            vec,
            pl.BlockSpec((GLA_WIDTH, D_MODEL), lambda i: (0, 0)),
            pl.BlockSpec((GMLP_WIDTH, D_MODEL), lambda i: (1, 0)),
        ],
        out_specs=out_specs,
        out_shape=out_shape,
        compiler_params=_params("parallel"),
        name="mixout",
    )(h, o, z, z, g_ln, b_ln, w_mix, bias, g_out, w_out, w_out)


def _ffn_kernel(h_ref, g_ref, wg_ref, wu_ref, wo_ref, o_ref, hb_scr, r_scr):
    j = pl.program_id(1)

    def tile(hb, r):
        gate = r * _dot(hb, wg_ref[...])
        up = r * _dot(hb, wu_ref[...])
        return _dot((jax.nn.silu(gate) * up).astype(BF16), wo_ref[...])

    @pl.when(j == 0)
    def _():
        h = h_ref[...]
        hb = (h * g_ref[...]).astype(BF16)
        hb_scr[...] = hb
        r = lax.rsqrt(jnp.mean(h * h, axis=-1, keepdims=True) + EPS)
        r_scr[...] = r
        o_ref[...] = h + tile(hb, r)

    @pl.when(j > 0)
    def _():
        o_ref[...] += tile(hb_scr[...], r_scr[...])


def _ffn(h1, g_ffn, w_in, w_out, tm, tf):
    m = h1.shape[0]
    nf = D_FF // tf
    return pl.pallas_call(
        _ffn_kernel,
        grid=(m // tm, nf),
        in_specs=[
            pl.BlockSpec((tm, D_MODEL), lambda i, j: (i, 0)),
            pl.BlockSpec((1, D_MODEL), lambda i, j: (0, 0)),
            pl.BlockSpec((D_MODEL, tf), lambda i, j: (0, j)),
            pl.BlockSpec((D_MODEL, tf), lambda i, j: (0, nf + j)),
            pl.BlockSpec((tf, D_MODEL), lambda i, j: (j, 0)),
        ],
        out_specs=pl.BlockSpec((tm, D_MODEL), lambda i, j: (i, 0)),
        out_shape=jax.ShapeDtypeStruct((m, D_MODEL), F32),
        scratch_shapes=[pltpu.VMEM((tm, D_MODEL), BF16), pltpu.VMEM((tm, 1), F32)],
        compiler_params=_params("parallel", "arbitrary"),
        name="ffn",
    )(h1, g_ffn, w_in, w_in, w_out)


PLE_COL_CHUNK = 512


def _ple_kernel(h_ref, p_ref, wp_ref, gp_ref, gg_ref, wg_ref, gf_ref, y_ref):
    h = h_ref[...]
    r = lax.rsqrt(jnp.mean(h * h, axis=-1, keepdims=True) + EPS)
    hb = (h * gg_ref[...]).astype(BF16)
    pe = _rms(_dot(p_ref[...].astype(BF16), wp_ref[...]), gp_ref[...])
    ssq = jnp.zeros((h.shape[0], 1), F32)
    for c in range(D_MODEL // PLE_COL_CHUNK):
        cols = slice(c * PLE_COL_CHUNK, (c + 1) * PLE_COL_CHUNK)
        t = h[:, cols] + pe[:, cols] * jax.nn.sigmoid(r * _dot(hb, wg_ref[:, cols]))
        ssq = ssq + jnp.sum(t * t, axis=-1, keepdims=True)
        y_ref[:, cols] = t
    y_ref[...] = y_ref[...] * lax.rsqrt(ssq / D_MODEL + EPS) * gf_ref[...]


def _ple(h, p, w_ple, g_ple, g_gate, w_gate, g_final, tm):
    m = h.shape[0]
    vec = pl.BlockSpec((1, D_MODEL), lambda i: (0, 0))
    return pl.pallas_call(
        _ple_kernel,
        grid=(m // tm,),
        in_specs=[
            pl.BlockSpec((tm, D_MODEL), lambda i: (i, 0)),
            pl.BlockSpec((tm, PLE_DIM), lambda i: (i, 0)),
            pl.BlockSpec((PLE_DIM, D_MODEL), lambda i: (0, 0)),
            vec, vec,
            pl.BlockSpec((D_MODEL, D_MODEL), lambda i: (0, 0)),
            vec,
        ],
        out_specs=pl.BlockSpec((tm, D_MODEL), lambda i: (i, 0)),
        out_shape=jax.ShapeDtypeStruct((m, D_MODEL), F32),
        compiler_params=_params("parallel"),
        name="ple",
    )(h, p, w_ple, g_ple, g_gate, w_gate, g_final)


CAST_NAMES = ("w_out", "w_ffn_in", "w_ffn_out", "w_ple_gate")


def _layer(x, p, s0, wts, cast=()):
    n_seq, seq_len = p.shape[:2]
    x = x.reshape(n_seq * seq_len, D_MODEL)
    p = p.reshape(n_seq * seq_len, PLE_DIM)
    z, alow = _inproj(x, wts["g_mix"], wts["wt_in"], tm=1024, tn=1024)
    o, s_new, *cast16 = _gla(z, alow, wts["wa2_pad"], wts["b_a"], wts["g_gla_norm"], s0,
                             n_seq=n_seq, seq_len=seq_len, cast=cast)
    if cast:
        wts = dict(wts, **dict(zip(CAST_NAMES, cast16)))
    blk = min(seq_len, GMLP_CHUNK)
    reps = GMLP_CHUNK // blk
    sel = jnp.asarray(np.tile(np.eye(blk, dtype=np.float32), (reps, 1)))
    w_mix = jnp.einsum("ra,gab,cb->grc", sel, wts["w_s"][:, :blk, :blk], sel, precision=lax.Precision.HIGHEST)
    bias = jnp.tile(jnp.repeat(wts["b_s"][:, :blk].T, GMLP_GROUP_DIM, axis=1), (reps, 1))
    emit_vn = s0 is not None
    h1, *vn = _mixout(x, o, z, wts["g_gmlp_ln"], wts["b_gmlp_ln"], w_mix, bias, wts["g_gmlp_out"], wts["w_out"],
                      tm=512, blk=blk, emit_vn=emit_vn)
    h2 = _ffn(h1, wts["g_ffn"], wts["w_ffn_in"], wts["w_ffn_out"], tm=1024, tf=512)
    y = _ple(h2, p, wts["w_ple"], wts["g_ple"], wts["g_ple_gate"], wts["w_ple_gate"], wts["g_final"], tm=512)
    return y.reshape(n_seq, seq_len, D_MODEL), s_new, (vn[0] if emit_vn else None), wts


def kernel(x_prompt, x_sample, state_gla, p_prompt, p_sample, g_mix, w_in, w_a2, b_a, g_gla_norm, g_gmlp_ln,
           b_gmlp_ln, w_s, b_s, g_gmlp_out, w_out, g_ffn, w_ffn_in, w_ffn_out, w_ple, g_ple, g_ple_gate,
           w_ple_gate, g_final):
    dec_batch, dec_seq, _ = x_sample.shape
    depth = w_in.shape[0]
    assert depth == 1, "one layer: the prompt / sample passes below are not chained over depth"
    row = lambda a: a.reshape(1, -1)
    wts = dict(
        g_mix=row(g_mix[0]),
        wt_in=jnp.swapaxes(w_in, 1, 2)[0].astype(BF16),
        wa2_pad=jnp.pad(w_a2[0], ((0, V7X_LANES - GLA_GATE_RANK), (0, 0))),
        b_a=row(b_a[0]),
        g_gla_norm=row(g_gla_norm[0]),
        g_gmlp_ln=row(g_gmlp_ln[0]),
        b_gmlp_ln=row(b_gmlp_ln[0]),
        w_s=w_s[0],
        b_s=b_s[0],
        g_gmlp_out=row(g_gmlp_out[0]),
        g_ffn=row(g_ffn[0]),
        w_ple=w_ple[0].astype(BF16),
        g_ple=row(g_ple[0]),
        g_ple_gate=row(g_ple_gate[0]),
        g_final=row(g_final),
    )
    yp, sp, _, wts = _layer(x_prompt, p_prompt[0], None, wts,
                            cast=(w_out[0], w_ffn_in[0], w_ffn_out[0], w_ple_gate[0]))
    ys, ss, vs, _ = _layer(x_sample, p_sample[0], state_gla[0], wts)
    return (yp, ys, sp[None], ss[None], vs.reshape(1, dec_batch, dec_seq, GMLP_WIDTH))
```

```python
import functools

import numpy as np

import jax
import jax.numpy as jnp
from jax import lax
from jax.experimental import pallas as pl
from jax.experimental.pallas import tpu as pltpu

F32 = jnp.float32
BF16 = jnp.bfloat16

D_MODEL = 2048
GLA_WIDTH = 1024
GMLP_WIDTH = 1024
GLA_HEADS = 4
GLA_DV = 256
GLA_DK = 128
GLA_KEY_WIDTH = 512
GLA_GATE_RANK = 16
GLA_TAU = 16.0
GLA_CHUNK = 64
GMLP_GROUPS = 8
GMLP_GROUP_DIM = 128
GMLP_CHUNK = 128
D_FF = 5632
PLE_DIM = 256
EPS = 1e-6
IN_OFFSETS = (0, 512, 1024, 2048, 3072, 3088, 4112, 5136)
Z_WIDTH = 5120
Z_GLA_WIDTH = 3072
Z_U_OFF = 3072
Z_VG_OFF = 4096

V7X_LANES = 128
V7X_VMEM_BYTES = 64 * 1024 * 1024
VMEM_LIMIT_BYTES = 60 * 1024 * 1024
N_CAST_BLOCKS = 32


def _params(*sem):
    return pltpu.CompilerParams(dimension_semantics=sem, vmem_limit_bytes=VMEM_LIMIT_BYTES)


def _rms(x, g):
    return x * lax.rsqrt(jnp.mean(x * x, axis=-1, keepdims=True) + EPS) * g


def _dot(a, b):
    return jnp.dot(a, b, preferred_element_type=F32)


def _dot_tn(a, b):
    return lax.dot_general(a, b, (((0,), (0,)), ((), ())), preferred_element_type=F32)


def _dot_nt(a, b):
    return lax.dot_general(a, b, (((1,), (1,)), ((), ())), preferred_element_type=F32)


def _cast_specs(cast, n_blocks, step_of):
    specs = []
    for w in cast:
        rows = w.shape[0] // n_blocks
        assert rows * n_blocks == w.shape[0] and rows % 16 == 0
        specs.append(pl.BlockSpec((rows, w.shape[1]),
                                  lambda *ids: (jnp.minimum(step_of(*ids), n_blocks - 1), 0)))
    return specs


def _cast_blocks(cast_in, cast_out):
    for src_ref, dst_ref in zip(cast_in, cast_out):
        dst_ref[...] = src_ref[...].astype(BF16)


def _inproj_kernel(*refs, emit, n_cast):
    x_ref, g_ref, w_ref, wl_ref = refs[:4]
    cast_in = refs[4:4 + n_cast]
    n_out = 4 if emit else 2
    z_ref, alow_ref = refs[4 + n_cast:6 + n_cast]
    emit_refs = refs[6 + n_cast:4 + n_cast + n_out]
    cast_out = refs[4 + n_cast + n_out:4 + 2 * n_cast + n_out]
    xb_scr, r_scr = refs[4 + 2 * n_cast + n_out:]
    j = pl.program_id(1)

    def weight(ref, emit_ref):
        if not emit:
            return ref[...]
        w16 = ref[...].astype(BF16)
        emit_ref[...] = w16
        return w16

    def z_tile(xb, r):
        return r * _dot_nt(xb, weight(w_ref, emit_refs[0] if emit else None))

    @pl.when(j == 0)
    def _():
        _cast_blocks(cast_in, cast_out)
        x = x_ref[...]
        xb = (x * g_ref[...]).astype(BF16)
        xb_scr[...] = xb
        r = lax.rsqrt(jnp.mean(x * x, axis=-1, keepdims=True) + EPS)
        r_scr[...] = r
        alow_ref[...] = r * _dot_nt(xb, weight(wl_ref, emit_refs[1] if emit else None))
        z_ref[...] = z_tile(xb, r).astype(BF16)

    @pl.when(j > 0)
    def _():
        _cast_blocks(cast_in, cast_out)
        z_ref[...] = z_tile(xb_scr[...], r_scr[...]).astype(BF16)


def _inproj(x, g_mix, w, wl, *, tm, tn, emit, cast=()):
    m = x.shape[0]
    n_a = Z_GLA_WIDTH // tn
    n_j = Z_WIDTH // tn
    if emit:
        def w_row(j):
            return pl.multiple_of(jnp.where(j < n_a, j * tn, IN_OFFSETS[5] + (j - n_a) * tn), 16)
        w_spec = pl.BlockSpec((pl.Element(tn), pl.Element(D_MODEL)), lambda i, j: (w_row(j), 0))
        wl_spec = pl.BlockSpec((V7X_LANES, D_MODEL), lambda i, j: (IN_OFFSETS[4] // V7X_LANES, 0))
        assert m == tm, "every weight tile must be visited exactly once"
    else:
        w_spec = pl.BlockSpec((tn, D_MODEL), lambda i, j: (j, 0))
        wl_spec = pl.BlockSpec((V7X_LANES, D_MODEL), lambda i, j: (0, 0))
    cast_specs = _cast_specs(cast, N_CAST_BLOCKS, lambda i, j: i * n_j + j)
    assert not cast or (m // tm) * n_j >= N_CAST_BLOCKS
    out_specs = [
        pl.BlockSpec((tm, tn), lambda i, j: (i, j)),
        pl.BlockSpec((tm, V7X_LANES), lambda i, j: (i, 0)),
    ]
    out_shape = [
        jax.ShapeDtypeStruct((m, Z_WIDTH), BF16),
        jax.ShapeDtypeStruct((m, V7X_LANES), F32),
    ]
    if emit:
        out_specs += [pl.BlockSpec((tn, D_MODEL), lambda i, j: (j, 0)),
                      pl.BlockSpec((V7X_LANES, D_MODEL), lambda i, j: (0, 0))]
        out_shape += [jax.ShapeDtypeStruct((Z_WIDTH, D_MODEL), BF16),
                      jax.ShapeDtypeStruct((V7X_LANES, D_MODEL), BF16)]
    return pl.pallas_call(
        functools.partial(_inproj_kernel, emit=emit, n_cast=len(cast)),
        grid=(m // tm, n_j),
        in_specs=[
            pl.BlockSpec((tm, D_MODEL), lambda i, j: (i, 0)),
            pl.BlockSpec((1, D_MODEL), lambda i, j: (0, 0)),
            w_spec,
            wl_spec,
        ] + cast_specs,
        out_specs=out_specs + cast_specs,
        out_shape=out_shape + [jax.ShapeDtypeStruct(w.shape, BF16) for w in cast],
        scratch_shapes=[pltpu.VMEM((tm, D_MODEL), BF16), pltpu.VMEM((tm, 1), F32)],
        compiler_params=_params("arbitrary" if cast else "parallel", "arbitrary"),
        name="inproj",
    )(x, g_mix, w, wl, *cast)


N_GLA_IN = 10


def _group_mask(R, C):
    t = np.arange(R)[:, None]
    s = np.arange(R)[None, :]
    return ((t // C == s // C) & (s <= t)).astype(np.float32)


def _gla_kernel(*refs, C, R, sequential, n_cast):
    q_ref, k_ref, v_ref, r_ref, alow_ref, wa2_ref, ba_ref, gn_ref, tri_ref, mask_ref = refs[:N_GLA_IN]
    if sequential:
        cast_in = refs[N_GLA_IN:N_GLA_IN + n_cast]
        o_ref, s_out_ref = refs[N_GLA_IN + n_cast:N_GLA_IN + n_cast + 2]
        cast_out = refs[N_GLA_IN + n_cast + 2:N_GLA_IN + 2 * n_cast + 2]
        s_scr, b_scr, e_scr = refs[N_GLA_IN + 2 * n_cast + 2:]
        _cast_blocks(cast_in, cast_out)
    else:
        s0_ref, o_ref, s_out_ref, b_scr, e_scr = refs[N_GLA_IN:]
    n = R // C
    mid = C // 2
    W = GLA_KEY_WIDTH

    x = _dot(alow_ref[...].astype(BF16), wa2_ref[...].astype(BF16)) + ba_ref[...]
    lb = (jnp.minimum(x, 0.0) - jnp.log(1.0 + jnp.exp(-jnp.abs(x)))) * (1.0 / GLA_TAU)

    tri = tri_ref[...]
    lb_hi = lb.astype(BF16)
    lb_lo = (lb - lb_hi.astype(F32)).astype(BF16)
    b = _dot(tri, lb_hi) + _dot(tri, lb_lo)
    b3 = b.reshape(n, C, W)
    b_mid = jnp.broadcast_to(b3[:, mid:mid + 1, :], (n, C, W)).reshape(R, W)
    b_last = jnp.broadcast_to(b3[:, C - 1:C, :], (n, C, W)).reshape(R, W)

    q = q_ref[...].astype(F32) * (GLA_DK ** -0.5)
    k = k_ref[...].astype(F32)
    q1 = (q * jnp.exp(b)).astype(BF16)
    q2 = (q * jnp.exp(b - b_mid)).astype(BF16)
    k2 = (k * jnp.exp(b_mid - b)).astype(BF16)
    k3 = k * jnp.exp(b_last - b)

    e_scr[...] = jnp.zeros((V7X_LANES, GLA_DK), F32)
    for h in range(GLA_HEADS):
        b_scr[h] = b[:, h * GLA_DK:(h + 1) * GLA_DK]
        e_scr[h * n:(h + 1) * n, :] = jnp.exp(b_scr[h, pl.ds(C - 1, n, stride=C), :])
    d_cols = e_scr[...].T

    causal = mask_ref[...] > 0.0
    if sequential:
        @pl.when(pl.program_id(1) == 0)
        def _():
            s_scr[...] = jnp.zeros((GLA_HEADS, GLA_DK, GLA_DV), F32)
    else:
        grp = lax.broadcasted_iota(jnp.int32, (R, 1), 0) >> (C.bit_length() - 1)

    for h in range(GLA_HEADS):
        ks = slice(h * GLA_DK, (h + 1) * GLA_DK)
        vs = slice(h * GLA_DV, (h + 1) * GLA_DV)
        v = v_ref[:, vs]
        att = jnp.where(causal, _dot_nt(q2[:, ks], k2[:, ks]), 0.0).astype(BF16)
        o = _dot(att, v)
        q1h = q1[:, ks]
        k3h = k3[:, ks]
        if sequential:
            s = s_scr[h]
            o_parts = []
            for g in range(n):
                rows = slice(g * C, (g + 1) * C)
                o_parts.append(o[rows] + _dot(q1h[rows], s.astype(BF16)))
                s = d_cols[:, h * n + g:h * n + g + 1] * s + _dot_tn(k3h[rows].astype(BF16), v[rows])
            s_scr[h] = s
            s_out_ref[0, h] = s
            o = jnp.concatenate(o_parts, axis=0)
        else:
            for g in range(n):
                s = s0_ref[g, h]
                o = o + jnp.where(grp == g, _dot(q1h, s.astype(BF16)), 0.0)
                k3g = jnp.where(grp == g, k3h, 0.0).astype(BF16)
                s_out_ref[g, h] = d_cols[:, h * n + g:h * n + g + 1] * s + _dot_tn(k3g, v)
        o = _rms(o, gn_ref[:, vs]) * jax.nn.silu(r_ref[:, vs].astype(F32))
        o_ref[:, vs] = o.astype(BF16)


def _gla(z, alow, *, wa2_pad, b_a, g_norm, s0, n_seq, seq_len, cast=()):
    m = z.shape[0]
    sequential = s0 is None
    assert sequential or not cast
    if sequential:
        C, R = GLA_CHUNK, 4 * GLA_CHUNK
        nt = seq_len // R
        grid = (n_seq, nt)
        rowblk = lambda a, c: a * nt + c
        s_block = (1, GLA_HEADS, GLA_DK, GLA_DV)
    else:
        C, R = seq_len, 64
        grid = (m // R, 1)
        rowblk = lambda a, c: a
        s_block = (R // C, GLA_HEADS, GLA_DK, GLA_DV)
    s_map = lambda a, c: (a, 0, 0, 0)
    mask = _group_mask(R, C)
    const = lambda a, c: (0, 0)
    in_specs = [
        pl.BlockSpec((R, GLA_KEY_WIDTH), lambda a, c: (rowblk(a, c), 0)),
        pl.BlockSpec((R, GLA_KEY_WIDTH), lambda a, c: (rowblk(a, c), 1)),
        pl.BlockSpec((R, GLA_WIDTH), lambda a, c: (rowblk(a, c), 1)),
        pl.BlockSpec((R, GLA_WIDTH), lambda a, c: (rowblk(a, c), 2)),
        pl.BlockSpec((R, V7X_LANES), lambda a, c: (rowblk(a, c), 0)),
        pl.BlockSpec((V7X_LANES, GLA_KEY_WIDTH), const),
        pl.BlockSpec((1, GLA_KEY_WIDTH), const),
        pl.BlockSpec((1, GLA_WIDTH), const),
        pl.BlockSpec((R, R), const),
        pl.BlockSpec((R, R), const),
    ]
    args = [z, z, z, z, alow, wa2_pad, b_a, g_norm, jnp.asarray(mask, BF16), jnp.asarray(mask, F32)]
    assert len(args) == N_GLA_IN
    scratch = [pltpu.VMEM((GLA_HEADS, R, GLA_DK), F32), pltpu.VMEM((V7X_LANES, GLA_DK), F32)]
    out_specs = [
        pl.BlockSpec((R, GLA_WIDTH), lambda a, c: (rowblk(a, c), 0)),
        pl.BlockSpec(s_block, s_map),
    ]
    out_shape = [
        jax.ShapeDtypeStruct((m, GLA_WIDTH), BF16),
        jax.ShapeDtypeStruct((n_seq, GLA_HEADS, GLA_DK, GLA_DV), F32),
    ]
    if sequential:
        scratch = [pltpu.VMEM((GLA_HEADS, GLA_DK, GLA_DV), F32)] + scratch
        assert not cast or grid[0] * grid[1] >= N_CAST_BLOCKS
        cast_specs = _cast_specs(cast, N_CAST_BLOCKS, rowblk)
        in_specs += cast_specs
        out_specs += cast_specs
        out_shape += [jax.ShapeDtypeStruct(w.shape, BF16) for w in cast]
        args.extend(cast)
    else:
        in_specs.append(pl.BlockSpec(s_block, s_map))
        args.append(s0)
    return pl.pallas_call(
        functools.partial(_gla_kernel, C=C, R=R, sequential=sequential, n_cast=len(cast)),
        grid=grid,
        in_specs=in_specs,
        out_specs=out_specs,
        out_shape=out_shape,
        scratch_shapes=scratch,
        compiler_params=_params("parallel", "arbitrary"),
        name="gla_seq" if sequential else "gla_step",
    )(*args)


def _mixout_tile(h_ref, o_ref, u_ref, vg_ref, gln_ref, bln_ref, ws_ref, bias_ref, gout_ref, wo_ref, wm_ref,
                 h1_ref, vn_ref, *, tm, blk):
    n_chunks = tm // GMLP_CHUNK
    col_w = D_MODEL // n_chunks
    shift = blk.bit_length() - 1
    row = lax.broadcasted_iota(jnp.int32, (GMLP_CHUNK, GMLP_CHUNK), 0)
    col = lax.broadcasted_iota(jnp.int32, (GMLP_CHUNK, GMLP_CHUNK), 1)
    causal = jnp.logical_and((row >> shift) == (col >> shift), col <= row)
    w = [jnp.where(causal, ws_ref[g], 0.0).astype(BF16) for g in range(GMLP_GROUPS)]
    m_parts = []
    for c in range(n_chunks):
        cols = slice(c * col_w, (c + 1) * col_w)
        h1_ref[:, cols] = h_ref[:, cols] + _dot(o_ref[...], wo_ref[:, cols])
        rows = pl.ds(c * GMLP_CHUNK, GMLP_CHUNK)
        vg = jax.nn.gelu(vg_ref[rows, :].astype(F32))
        mu = jnp.mean(vg, axis=-1, keepdims=True)
        vc = vg - mu
        vn = vc * lax.rsqrt(jnp.mean(vc * vc, axis=-1, keepdims=True) + EPS) * gln_ref[...] + bln_ref[...]
        if vn_ref is not None:
            vn_ref[rows, :] = vn
        vn16 = vn.astype(BF16)
        mixed = jnp.concatenate(
            [_dot(w[g], vn16[:, g * GMLP_GROUP_DIM:(g + 1) * GMLP_GROUP_DIM]) for g in range(GMLP_GROUPS)],
            axis=1) + bias_ref[...]
        mm = jax.nn.gelu(u_ref[rows, :].astype(F32)) * mixed
        m_parts.append(_rms(mm, gout_ref[...]).astype(BF16))
    h1_ref[...] += _dot(jnp.concatenate(m_parts, axis=0), wm_ref[...])


def _mixout_kernel(hp_ref, hs_ref, op_ref, os_ref, up_ref, vgp_ref, us_ref, vgs_ref, gln_ref, bln_ref,
                   wsp_ref, biasp_ref, wss_ref, biass_ref, gout_ref, wo_ref, wm_ref, h1_ref, vn_ref,
                   *, tm, n_p, blk_p, blk_s):
    shared = (gln_ref, bln_ref)
    tail = (gout_ref, wo_ref, wm_ref, h1_ref)
    i = pl.program_id(0)

    @pl.when(i < n_p)
    def _():
        _mixout_tile(hp_ref, op_ref, up_ref, vgp_ref, *shared, wsp_ref, biasp_ref, *tail, None, tm=tm, blk=blk_p)

    @pl.when(i >= n_p)
    def _():
        _mixout_tile(hs_ref, os_ref, us_ref, vgs_ref, *shared, wss_ref, biass_ref, *tail, vn_ref, tm=tm, blk=blk_s)


def _gmlp_operands(w_s, b_s, blk):
    reps = GMLP_CHUNK // blk
    sel = jnp.asarray(np.tile(np.eye(blk, dtype=np.float32), (reps, 1)))
    w_mix = jnp.einsum("ra,gab,cb->grc", sel, w_s[:, :blk, :blk], sel, precision=lax.Precision.HIGHEST)
    bias = jnp.tile(jnp.repeat(b_s[:, :blk].T, GMLP_GROUP_DIM, axis=1), (reps, 1))
    return w_mix, bias


def _mixout(hp, hs, op, os_, zp, zs, g_ln, b_ln, w_s, b_s, g_out, w_out, *, tm, len_p, len_s):
    n_p, n_s = hp.shape[0] // tm, hs.shape[0] // tm
    blk_p, blk_s = min(len_p, GMLP_CHUNK), min(len_s, GMLP_CHUNK)
    pidx = lambda i: jnp.minimum(i, n_p - 1)
    sidx = lambda i: jnp.maximum(i - n_p, 0)
    once = pl.Buffered(1)
    vec = pl.BlockSpec((1, GMLP_WIDTH), lambda i: (0, 0))
    mixw = pl.BlockSpec((GMLP_GROUPS, GMLP_CHUNK, GMLP_CHUNK), lambda i: (0, 0, 0))
    biasb = pl.BlockSpec((GMLP_CHUNK, GMLP_WIDTH), lambda i: (0, 0))
    return pl.pallas_call(
        functools.partial(_mixout_kernel, tm=tm, n_p=n_p, blk_p=blk_p, blk_s=blk_s),
        grid=(n_p + n_s,),
        in_specs=[
            pl.BlockSpec((tm, D_MODEL), lambda i: (pidx(i), 0)),
            pl.BlockSpec((tm, D_MODEL), lambda i: (sidx(i), 0), pipeline_mode=once),
            pl.BlockSpec((tm, GLA_WIDTH), lambda i: (pidx(i), 0)),
            pl.BlockSpec((tm, GLA_WIDTH), lambda i: (sidx(i), 0), pipeline_mode=once),
            pl.BlockSpec((tm, GMLP_WIDTH), lambda i: (pidx(i), Z_U_OFF // GMLP_WIDTH)),
            pl.BlockSpec((tm, GMLP_WIDTH), lambda i: (pidx(i), Z_VG_OFF // GMLP_WIDTH)),
            pl.BlockSpec((tm, GMLP_WIDTH), lambda i: (sidx(i), Z_U_OFF // GMLP_WIDTH), pipeline_mode=once),
            pl.BlockSpec((tm, GMLP_WIDTH), lambda i: (sidx(i), Z_VG_OFF // GMLP_WIDTH), pipeline_mode=once),
            vec, vec,
            mixw, biasb, mixw, biasb,
            vec,
            pl.BlockSpec((GLA_WIDTH, D_MODEL), lambda i: (0, 0), pipeline_mode=once),
            pl.BlockSpec((GMLP_WIDTH, D_MODEL), lambda i: (1, 0), pipeline_mode=once),
        ],
        out_specs=[
            pl.BlockSpec((tm, D_MODEL), lambda i: (i, 0)),
            pl.BlockSpec((tm, GMLP_WIDTH), lambda i: (sidx(i), 0)),
        ],
        out_shape=[
            jax.ShapeDtypeStruct((hp.shape[0] + hs.shape[0], D_MODEL), F32),
            jax.ShapeDtypeStruct((hs.shape[0], GMLP_WIDTH), F32),
        ],
        compiler_params=_params("arbitrary"),
        name="mixout",
    )(hp, hs, op, os_, zp, zp, zs, zs, g_ln, b_ln, *_gmlp_operands(w_s, b_s, blk_p), *_gmlp_operands(w_s, b_s, blk_s),
      g_out, w_out, w_out)


def _ffn_kernel(h_ref, g_ref, wg_ref, wu_ref, wo_ref, o_ref, hb_scr, r_scr):
    j = pl.program_id(1)

    def tile(hb, r):
        gate = r * _dot(hb, wg_ref[...])
        up = r * _dot(hb, wu_ref[...])
        return _dot((jax.nn.silu(gate) * up).astype(BF16), wo_ref[...])

    @pl.when(j == 0)
    def _():
        h = h_ref[...]
        hb = (h * g_ref[...]).astype(BF16)
        hb_scr[...] = hb
        r = lax.rsqrt(jnp.mean(h * h, axis=-1, keepdims=True) + EPS)
        r_scr[...] = r
        o_ref[...] = h + tile(hb, r)

    @pl.when(j > 0)
    def _():
        o_ref[...] += tile(hb_scr[...], r_scr[...])


def _ffn(h1, g_ffn, w_in, w_out, tm, tf):
    m = h1.shape[0]
    nf = D_FF // tf
    return pl.pallas_call(
        _ffn_kernel,
        grid=(m // tm, nf),
        in_specs=[
            pl.BlockSpec((tm, D_MODEL), lambda i, j: (i, 0)),
            pl.BlockSpec((1, D_MODEL), lambda i, j: (0, 0)),
            pl.BlockSpec((D_MODEL, tf), lambda i, j: (0, j)),
            pl.BlockSpec((D_MODEL, tf), lambda i, j: (0, nf + j)),
            pl.BlockSpec((tf, D_MODEL), lambda i, j: (j, 0)),
        ],
        out_specs=pl.BlockSpec((tm, D_MODEL), lambda i, j: (i, 0)),
        out_shape=jax.ShapeDtypeStruct((m, D_MODEL), F32),
        scratch_shapes=[pltpu.VMEM((tm, D_MODEL), BF16), pltpu.VMEM((tm, 1), F32)],
        compiler_params=_params("parallel", "arbitrary"),
        name="ffn",
    )(h1, g_ffn, w_in, w_in, w_out)


PLE_COL_CHUNK = 512


def _ple_tile(h_ref, p_ref, wp_ref, gp_ref, gg_ref, wg_ref, gf_ref, y_ref):
    h = h_ref[...]
    r = lax.rsqrt(jnp.mean(h * h, axis=-1, keepdims=True) + EPS)
    hb = (h * gg_ref[...]).astype(BF16)
    pe = _rms(_dot(p_ref[...].astype(BF16), wp_ref[...]), gp_ref[...])
    ssq = jnp.zeros((h.shape[0], 1), F32)
    for c in range(D_MODEL // PLE_COL_CHUNK):
        cols = slice(c * PLE_COL_CHUNK, (c + 1) * PLE_COL_CHUNK)
        t = h[:, cols] + pe[:, cols] * jax.nn.sigmoid(r * _dot(hb, wg_ref[:, cols]))
        ssq = ssq + jnp.sum(t * t, axis=-1, keepdims=True)
        y_ref[:, cols] = t
    y_ref[...] = y_ref[...] * lax.rsqrt(ssq / D_MODEL + EPS) * gf_ref[...]


def _ple_kernel(h_ref, pp_ref, ps_ref, wp_ref, gp_ref, gg_ref, wg_ref, gf_ref, yp_ref, ys_ref, *, n_p):
    params = (wp_ref, gp_ref, gg_ref, wg_ref, gf_ref)
    i = pl.program_id(0)

    @pl.when(i < n_p)
    def _():
        _ple_tile(h_ref, pp_ref, *params, yp_ref)

    @pl.when(i >= n_p)
    def _():
        _ple_tile(h_ref, ps_ref, *params, ys_ref)


def _ple(h, pp, ps, w_ple, g_ple, g_gate, w_gate, g_final, tm):
    n_p, n_s = pp.shape[0] // tm, ps.shape[0] // tm
    pidx = lambda i: (jnp.minimum(i, n_p - 1), 0)
    sidx = lambda i: (jnp.maximum(i - n_p, 0), 0)
    vec = pl.BlockSpec((1, D_MODEL), lambda i: (0, 0))
    return pl.pallas_call(
        functools.partial(_ple_kernel, n_p=n_p),
        grid=(n_p + n_s,),
        in_specs=[
            pl.BlockSpec((tm, D_MODEL), lambda i: (i, 0)),
            pl.BlockSpec((tm, PLE_DIM), pidx),
            pl.BlockSpec((tm, PLE_DIM), sidx),
            pl.BlockSpec((PLE_DIM, D_MODEL), lambda i: (0, 0)),
            vec, vec,
            pl.BlockSpec((D_MODEL, D_MODEL), lambda i: (0, 0), pipeline_mode=pl.Buffered(1)),
            vec,
        ],
        out_specs=[pl.BlockSpec((tm, D_MODEL), pidx), pl.BlockSpec((tm, D_MODEL), sidx)],
        out_shape=[jax.ShapeDtypeStruct((pp.shape[0], D_MODEL), F32),
                   jax.ShapeDtypeStruct((ps.shape[0], D_MODEL), F32)],
        compiler_params=_params("arbitrary"),
        name="ple",
    )(h, pp, ps, w_ple, g_ple, g_gate, w_gate, g_final)


def kernel(x_prompt, x_sample, state_gla, p_prompt, p_sample, g_mix, w_in, w_a2, b_a, g_gla_norm, g_gmlp_ln,
           b_gmlp_ln, w_s, b_s, g_gmlp_out, w_out, g_ffn, w_ffn_in, w_ffn_out, w_ple, g_ple, g_ple_gate,
           w_ple_gate, g_final):
    batch, seq, _ = x_prompt.shape
    dec_batch, dec_seq, _ = x_sample.shape
    depth = w_in.shape[0]
    assert depth == 1, "one layer: the prompt / sample passes below are not chained over depth"
    row = lambda a: a.reshape(1, -1)
    wts = dict(
        g_mix=row(g_mix[0]),
        wa2_pad=jnp.pad(w_a2[0], ((0, V7X_LANES - GLA_GATE_RANK), (0, 0))),
        b_a=row(b_a[0]),
        g_gla_norm=row(g_gla_norm[0]),
        g_gmlp_ln=row(g_gmlp_ln[0]),
        b_gmlp_ln=row(b_gmlp_ln[0]),
        w_s=w_s[0],
        b_s=b_s[0],
        g_gmlp_out=row(g_gmlp_out[0]),
        g_ffn=row(g_ffn[0]),
        w_ple=w_ple[0].astype(BF16),
        g_ple=row(g_ple[0]),
        g_ple_gate=row(g_ple_gate[0]),
        g_final=row(g_final),
    )
    xp = x_prompt.reshape(batch * seq, D_MODEL)
    xs = x_sample.reshape(dec_batch * dec_seq, D_MODEL)
    wt_in = jnp.swapaxes(w_in, 1, 2)[0]
    zs, alow_s, wt16, wl16 = _inproj(xs, wts["g_mix"], wt_in, wt_in, tm=xs.shape[0], tn=1024, emit=True)
    zp, alow_p, wts["w_out"], wts["w_ple_gate"] = _inproj(
        xp, wts["g_mix"], wt16, wl16, tm=1024, tn=1024, emit=False, cast=(w_out[0], w_ple_gate[0]))
    gla = functools.partial(_gla, wa2_pad=wts["wa2_pad"], b_a=wts["b_a"], g_norm=wts["g_gla_norm"])
    op, sp, wts["w_ffn_in"], wts["w_ffn_out"] = gla(zp, alow_p, s0=None, n_seq=batch, seq_len=seq,
                                                     cast=(w_ffn_in[0], w_ffn_out[0]))
    os_, ss = gla(zs, alow_s, s0=state_gla[0], n_seq=dec_batch, seq_len=dec_seq)
    h1, vs = _mixout(xp, xs, op, os_, zp, zs, wts["g_gmlp_ln"], wts["b_gmlp_ln"], wts["w_s"], wts["b_s"],
                     wts["g_gmlp_out"], wts["w_out"], tm=512, len_p=seq, len_s=dec_seq)
    h2 = _ffn(h1, wts["g_ffn"], wts["w_ffn_in"], wts["w_ffn_out"], tm=1024, tf=512)
    yp, ys = _ple(h2, p_prompt[0].reshape(batch * seq, PLE_DIM), p_sample[0].reshape(dec_batch * dec_seq, PLE_DIM),
                  wts["w_ple"], wts["g_ple"], wts["g_ple_gate"], wts["w_ple_gate"], wts["g_final"], tm=512)
    return (yp.reshape(batch, seq, D_MODEL), ys.reshape(dec_batch, dec_seq, D_MODEL), sp[None], ss[None],
            vs.reshape(1, dec_batch, dec_seq, GMLP_WIDTH))
```

```python
import functools

import numpy as np

import jax
import jax.numpy as jnp
from jax import lax
from jax.experimental import pallas as pl
from jax.experimental.pallas import tpu as pltpu

F32 = jnp.float32
BF16 = jnp.bfloat16

D_MODEL = 2048
GLA_WIDTH = 1024
GMLP_WIDTH = 1024
GLA_HEADS = 4
GLA_DV = 256
GLA_DK = 128
GLA_KEY_WIDTH = 512
GLA_GATE_RANK = 16
GLA_TAU = 16.0
GLA_CHUNK = 64
GMLP_GROUPS = 8
GMLP_GROUP_DIM = 128
GMLP_CHUNK = 128
D_FF = 5632
PLE_DIM = 256
EPS = 1e-6
IN_OFFSETS = (0, 512, 1024, 2048, 3072, 3088, 4112, 5136)
Z_WIDTH = 5120
Z_GLA_WIDTH = 3072
Z_U_OFF = 3072
Z_VG_OFF = 4096

V7X_LANES = 128
V7X_VMEM_BYTES = 64 * 1024 * 1024
VMEM_LIMIT_BYTES = 60 * 1024 * 1024
N_CAST_BLOCKS = 32


def _params(*sem):
    return pltpu.CompilerParams(dimension_semantics=sem, vmem_limit_bytes=VMEM_LIMIT_BYTES)


def _rms(x, g):
    return x * lax.rsqrt(jnp.mean(x * x, axis=-1, keepdims=True) + EPS) * g


def _dot(a, b):
    return jnp.dot(a, b, preferred_element_type=F32)


def _dot_tn(a, b):
    return lax.dot_general(a, b, (((0,), (0,)), ((), ())), preferred_element_type=F32)


def _dot_nt(a, b):
    return lax.dot_general(a, b, (((1,), (1,)), ((), ())), preferred_element_type=F32)


def _cast_specs(cast, n_blocks, step_of):
    specs = []
    for w in cast:
        rows = w.shape[0] // n_blocks
        assert rows * n_blocks == w.shape[0] and rows % 16 == 0
        specs.append(pl.BlockSpec((rows, w.shape[1]),
                                  lambda *ids: (jnp.minimum(step_of(*ids), n_blocks - 1), 0)))
    return specs


def _cast_blocks(cast_in, cast_out):
    for src_ref, dst_ref in zip(cast_in, cast_out):
        dst_ref[...] = src_ref[...].astype(BF16)


def _inproj_kernel(*refs, emit, n_cast):
    x_ref, g_ref, w_ref, wl_ref = refs[:4]
    cast_in = refs[4:4 + n_cast]
    n_out = 4 if emit else 2
    z_ref, alow_ref = refs[4 + n_cast:6 + n_cast]
    emit_refs = refs[6 + n_cast:4 + n_cast + n_out]
    cast_out = refs[4 + n_cast + n_out:4 + 2 * n_cast + n_out]
    xb_scr, r_scr = refs[4 + 2 * n_cast + n_out:]
    j = pl.program_id(1)

    def weight(ref, emit_ref):
        if not emit:
            return ref[...]
        w16 = ref[...].astype(BF16)
        emit_ref[...] = w16
        return w16

    def z_tile(xb, r):
        return r * _dot_nt(xb, weight(w_ref, emit_refs[0] if emit else None))

    @pl.when(j == 0)
    def _():
        _cast_blocks(cast_in, cast_out)
        x = x_ref[...]
        xb = (x * g_ref[...]).astype(BF16)
        xb_scr[...] = xb
        r = lax.rsqrt(jnp.mean(x * x, axis=-1, keepdims=True) + EPS)
        r_scr[...] = r
        alow_ref[...] = r * _dot_nt(xb, weight(wl_ref, emit_refs[1] if emit else None))
        z_ref[...] = z_tile(xb, r).astype(BF16)

    @pl.when(j > 0)
    def _():
        _cast_blocks(cast_in, cast_out)
        z_ref[...] = z_tile(xb_scr[...], r_scr[...]).astype(BF16)


def _inproj(x, g_mix, w, wl, *, tm, tn, emit, cast=()):
    m = x.shape[0]
    n_a = Z_GLA_WIDTH // tn
    n_j = Z_WIDTH // tn
    if emit:
        def w_row(j):
            return pl.multiple_of(jnp.where(j < n_a, j * tn, IN_OFFSETS[5] + (j - n_a) * tn), 16)
        w_spec = pl.BlockSpec((pl.Element(tn), pl.Element(D_MODEL)), lambda i, j: (w_row(j), 0))
        wl_spec = pl.BlockSpec((V7X_LANES, D_MODEL), lambda i, j: (IN_OFFSETS[4] // V7X_LANES, 0))
        assert m == tm, "every weight tile must be visited exactly once"
    else:
        w_spec = pl.BlockSpec((tn, D_MODEL), lambda i, j: (j, 0))
        wl_spec = pl.BlockSpec((V7X_LANES, D_MODEL), lambda i, j: (0, 0))
    cast_specs = _cast_specs(cast, N_CAST_BLOCKS, lambda i, j: i * n_j + j)
    assert not cast or (m // tm) * n_j >= N_CAST_BLOCKS
    out_specs = [
        pl.BlockSpec((tm, tn), lambda i, j: (i, j)),
        pl.BlockSpec((tm, V7X_LANES), lambda i, j: (i, 0)),
    ]
    out_shape = [
        jax.ShapeDtypeStruct((m, Z_WIDTH), BF16),
        jax.ShapeDtypeStruct((m, V7X_LANES), F32),
    ]
    if emit:
        out_specs += [pl.BlockSpec((tn, D_MODEL), lambda i, j: (j, 0)),
                      pl.BlockSpec((V7X_LANES, D_MODEL), lambda i, j: (0, 0))]
        out_shape += [jax.ShapeDtypeStruct((Z_WIDTH, D_MODEL), BF16),
                      jax.ShapeDtypeStruct((V7X_LANES, D_MODEL), BF16)]
    return pl.pallas_call(
        functools.partial(_inproj_kernel, emit=emit, n_cast=len(cast)),
        grid=(m // tm, n_j),
        in_specs=[
            pl.BlockSpec((tm, D_MODEL), lambda i, j: (i, 0)),
            pl.BlockSpec((1, D_MODEL), lambda i, j: (0, 0)),
            w_spec,
            wl_spec,
        ] + cast_specs,
        out_specs=out_specs + cast_specs,
        out_shape=out_shape + [jax.ShapeDtypeStruct(w.shape, BF16) for w in cast],
        scratch_shapes=[pltpu.VMEM((tm, D_MODEL), BF16), pltpu.VMEM((tm, 1), F32)],
        compiler_params=_params("arbitrary" if cast else "parallel", "arbitrary"),
        name="inproj",
    )(x, g_mix, w, wl, *cast)


N_GLA_IN = 10


def _group_mask(R, C):
    t = np.arange(R)[:, None]
    s = np.arange(R)[None, :]
    return ((t // C == s // C) & (s <= t)).astype(np.float32)


def _gla_kernel(*refs, C, R, sequential, n_cast):
    q_ref, k_ref, v_ref, r_ref, alow_ref, wa2_ref, ba_ref, gn_ref, tri_ref, mask_ref = refs[:N_GLA_IN]
    if sequential:
        cast_in = refs[N_GLA_IN:N_GLA_IN + n_cast]
        o_ref, s_out_ref = refs[N_GLA_IN + n_cast:N_GLA_IN + n_cast + 2]
        cast_out = refs[N_GLA_IN + n_cast + 2:N_GLA_IN + 2 * n_cast + 2]
        s_scr, b_scr, e_scr = refs[N_GLA_IN + 2 * n_cast + 2:]
        _cast_blocks(cast_in, cast_out)
    else:
        s0_ref, o_ref, s_out_ref, b_scr, e_scr = refs[N_GLA_IN:]
    n = R // C
    mid = C // 2
    W = GLA_KEY_WIDTH

    x = _dot(alow_ref[...].astype(BF16), wa2_ref[...].astype(BF16)) + ba_ref[...]
    lb = (jnp.minimum(x, 0.0) - jnp.log(1.0 + jnp.exp(-jnp.abs(x)))) * (1.0 / GLA_TAU)

    tri = tri_ref[...]
    lb_hi = lb.astype(BF16)
    lb_lo = (lb - lb_hi.astype(F32)).astype(BF16)
    b = _dot(tri, lb_hi) + _dot(tri, lb_lo)
    b3 = b.reshape(n, C, W)
    b_mid = jnp.broadcast_to(b3[:, mid:mid + 1, :], (n, C, W)).reshape(R, W)
    b_last = jnp.broadcast_to(b3[:, C - 1:C, :], (n, C, W)).reshape(R, W)

    q = q_ref[...].astype(F32) * (GLA_DK ** -0.5)
    k = k_ref[...].astype(F32)
    q1 = (q * jnp.exp(b)).astype(BF16)
    q2 = (q * jnp.exp(b - b_mid)).astype(BF16)
    k2 = (k * jnp.exp(b_mid - b)).astype(BF16)
    k3 = k * jnp.exp(b_last - b)

    e_scr[...] = jnp.zeros((V7X_LANES, GLA_DK), F32)
    for h in range(GLA_HEADS):
        b_scr[h] = b[:, h * GLA_DK:(h + 1) * GLA_DK]
        e_scr[h * n:(h + 1) * n, :] = jnp.exp(b_scr[h, pl.ds(C - 1, n, stride=C), :])
    d_cols = e_scr[...].T

    causal = mask_ref[...] > 0.0
    if sequential:
        @pl.when(pl.program_id(1) == 0)
        def _():
            s_scr[...] = jnp.zeros((GLA_HEADS, GLA_DK, GLA_DV), F32)
    else:
        grp = lax.broadcasted_iota(jnp.int32, (R, 1), 0) >> (C.bit_length() - 1)

    for h in range(GLA_HEADS):
        ks = slice(h * GLA_DK, (h + 1) * GLA_DK)
        vs = slice(h * GLA_DV, (h + 1) * GLA_DV)
        v = v_ref[:, vs]
        att = jnp.where(causal, _dot_nt(q2[:, ks], k2[:, ks]), 0.0).astype(BF16)
        o = _dot(att, v)
        q1h = q1[:, ks]
        k3h = k3[:, ks]
        if sequential:
            s = s_scr[h]
            o_parts = []
            for g in range(n):
                rows = slice(g * C, (g + 1) * C)
                o_parts.append(o[rows] + _dot(q1h[rows], s.astype(BF16)))
                s = d_cols[:, h * n + g:h * n + g + 1] * s + _dot_tn(k3h[rows].astype(BF16), v[rows])
            s_scr[h] = s
            s_out_ref[0, h] = s
            o = jnp.concatenate(o_parts, axis=0)
        else:
            for g in range(n):
                s = s0_ref[g, h]
                o = o + jnp.where(grp == g, _dot(q1h, s.astype(BF16)), 0.0)
                k3g = jnp.where(grp == g, k3h, 0.0).astype(BF16)
                s_out_ref[g, h] = d_cols[:, h * n + g:h * n + g + 1] * s + _dot_tn(k3g, v)
        o = _rms(o, gn_ref[:, vs]) * jax.nn.silu(r_ref[:, vs].astype(F32))
        o_ref[:, vs] = o.astype(BF16)


def _gla(z, alow, *, wa2_pad, b_a, g_norm, s0, n_seq, seq_len, cast=()):
    m = z.shape[0]
    sequential = s0 is None
    assert sequential or not cast
    if sequential:
        C, R = GLA_CHUNK, 4 * GLA_CHUNK
        nt = seq_len // R
        grid = (n_seq, nt)
        rowblk = lambda a, c: a * nt + c
        s_block = (1, GLA_HEADS, GLA_DK, GLA_DV)
    else:
        C, R = seq_len, 64
        grid = (m // R, 1)
        rowblk = lambda a, c: a
        s_block = (R // C, GLA_HEADS, GLA_DK, GLA_DV)
    s_map = lambda a, c: (a, 0, 0, 0)
    mask = _group_mask(R, C)
    const = lambda a, c: (0, 0)
    in_specs = [
        pl.BlockSpec((R, GLA_KEY_WIDTH), lambda a, c: (rowblk(a, c), 0)),
        pl.BlockSpec((R, GLA_KEY_WIDTH), lambda a, c: (rowblk(a, c), 1)),
        pl.BlockSpec((R, GLA_WIDTH), lambda a, c: (rowblk(a, c), 1)),
        pl.BlockSpec((R, GLA_WIDTH), lambda a, c: (rowblk(a, c), 2)),
        pl.BlockSpec((R, V7X_LANES), lambda a, c: (rowblk(a, c), 0)),
        pl.BlockSpec((V7X_LANES, GLA_KEY_WIDTH), const),
        pl.BlockSpec((1, GLA_KEY_WIDTH), const),
        pl.BlockSpec((1, GLA_WIDTH), const),
        pl.BlockSpec((R, R), const),
        pl.BlockSpec((R, R), const),
    ]
    args = [z, z, z, z, alow, wa2_pad, b_a, g_norm, jnp.asarray(mask, BF16), jnp.asarray(mask, F32)]
    assert len(args) == N_GLA_IN
    scratch = [pltpu.VMEM((GLA_HEADS, R, GLA_DK), F32), pltpu.VMEM((V7X_LANES, GLA_DK), F32)]
    out_specs = [
        pl.BlockSpec((R, GLA_WIDTH), lambda a, c: (rowblk(a, c), 0)),
        pl.BlockSpec(s_block, s_map),
    ]
    out_shape = [
        jax.ShapeDtypeStruct((m, GLA_WIDTH), BF16),
        jax.ShapeDtypeStruct((n_seq, GLA_HEADS, GLA_DK, GLA_DV), F32),
    ]
    if sequential:
        scratch = [pltpu.VMEM((GLA_HEADS, GLA_DK, GLA_DV), F32)] + scratch
        assert not cast or grid[0] * grid[1] >= N_CAST_BLOCKS
        cast_specs = _cast_specs(cast, N_CAST_BLOCKS, rowblk)
        in_specs += cast_specs
        out_specs += cast_specs
        out_shape += [jax.ShapeDtypeStruct(w.shape, BF16) for w in cast]
        args.extend(cast)
    else:
        in_specs.append(pl.BlockSpec(s_block, s_map))
        args.append(s0)
    return pl.pallas_call(
        functools.partial(_gla_kernel, C=C, R=R, sequential=sequential, n_cast=len(cast)),
        grid=grid,
        in_specs=in_specs,
        out_specs=out_specs,
        out_shape=out_shape,
        scratch_shapes=scratch,
        compiler_params=_params("parallel", "arbitrary"),
        name="gla_seq" if sequential else "gla_step",
    )(*args)


def _mixout_tile(h_ref, o_ref, u_ref, vg_ref, gln_ref, bln_ref, ws_ref, bias_ref, gout_ref, wo_ref, wm_ref,
                 h1_ref, vn_ref, *, tm, blk):
    n_chunks = tm // GMLP_CHUNK
    col_w = D_MODEL // n_chunks
    shift = blk.bit_length() - 1
    row = lax.broadcasted_iota(jnp.int32, (GMLP_CHUNK, GMLP_CHUNK), 0)
    col = lax.broadcasted_iota(jnp.int32, (GMLP_CHUNK, GMLP_CHUNK), 1)
    causal = jnp.logical_and((row >> shift) == (col >> shift), col <= row)
    w = [jnp.where(causal, ws_ref[g], 0.0).astype(BF16) for g in range(GMLP_GROUPS)]
    m_parts = []
    for c in range(n_chunks):
        cols = slice(c * col_w, (c + 1) * col_w)
        h1_ref[:, cols] = h_ref[:, cols] + _dot(o_ref[...], wo_ref[:, cols])
        rows = pl.ds(c * GMLP_CHUNK, GMLP_CHUNK)
        vg = jax.nn.gelu(vg_ref[rows, :].astype(F32))
        mu = jnp.mean(vg, axis=-1, keepdims=True)
        vc = vg - mu
        vn = vc * lax.rsqrt(jnp.mean(vc * vc, axis=-1, keepdims=True) + EPS) * gln_ref[...] + bln_ref[...]
        if vn_ref is not None:
            vn_ref[rows, :] = vn
        vn16 = vn.astype(BF16)
        mixed = jnp.concatenate(
            [_dot(w[g], vn16[:, g * GMLP_GROUP_DIM:(g + 1) * GMLP_GROUP_DIM]) for g in range(GMLP_GROUPS)],
            axis=1) + bias_ref[...]
        mm = jax.nn.gelu(u_ref[rows, :].astype(F32)) * mixed
        m_parts.append(_rms(mm, gout_ref[...]).astype(BF16))
    h1_ref[...] += _dot(jnp.concatenate(m_parts, axis=0), wm_ref[...])


def _mixout_kernel(hp_ref, hs_ref, op_ref, os_ref, up_ref, vgp_ref, us_ref, vgs_ref, gln_ref, bln_ref,
                   wsp_ref, biasp_ref, wss_ref, biass_ref, gout_ref, wo_ref, wm_ref, h1_ref, vn_ref,
                   *, tm, n_p, blk_p, blk_s):
    shared = (gln_ref, bln_ref)
    tail = (gout_ref, wo_ref, wm_ref, h1_ref)
    i = pl.program_id(0)

    @pl.when(i < n_p)
    def _():
        _mixout_tile(hp_ref, op_ref, up_ref, vgp_ref, *shared, wsp_ref, biasp_ref, *tail, None, tm=tm, blk=blk_p)

    @pl.when(i >= n_p)
    def _():
        _mixout_tile(hs_ref, os_ref, us_ref, vgs_ref, *shared, wss_ref, biass_ref, *tail, vn_ref, tm=tm, blk=blk_s)


def _gmlp_operands(w_s, b_s, blk):
    reps = GMLP_CHUNK // blk
    sel = jnp.asarray(np.tile(np.eye(blk, dtype=np.float32), (reps, 1)))
    w_mix = jnp.einsum("ra,gab,cb->grc", sel, w_s[:, :blk, :blk], sel, precision=lax.Precision.HIGHEST)
    bias = jnp.tile(jnp.repeat(b_s[:, :blk].T, GMLP_GROUP_DIM, axis=1), (reps, 1))
    return w_mix, bias


def _mixout(hp, hs, op, os_, zp, zs, g_ln, b_ln, w_s, b_s, g_out, w_out, *, tm, len_p, len_s):
    n_p, n_s = hp.shape[0] // tm, hs.shape[0] // tm
    blk_p, blk_s = min(len_p, GMLP_CHUNK), min(len_s, GMLP_CHUNK)
    pidx = lambda i: jnp.minimum(i, n_p - 1)
    sidx = lambda i: jnp.maximum(i - n_p, 0)
    once = pl.Buffered(1)
    vec = pl.BlockSpec((1, GMLP_WIDTH), lambda i: (0, 0))
    mixw = pl.BlockSpec((GMLP_GROUPS, GMLP_CHUNK, GMLP_CHUNK), lambda i: (0, 0, 0))
    biasb = pl.BlockSpec((GMLP_CHUNK, GMLP_WIDTH), lambda i: (0, 0))
    return pl.pallas_call(
        functools.partial(_mixout_kernel, tm=tm, n_p=n_p, blk_p=blk_p, blk_s=blk_s),
        grid=(n_p + n_s,),
        in_specs=[
            pl.BlockSpec((tm, D_MODEL), lambda i: (pidx(i), 0)),
            pl.BlockSpec((tm, D_MODEL), lambda i: (sidx(i), 0)),
            pl.BlockSpec((tm, GLA_WIDTH), lambda i: (pidx(i), 0)),
            pl.BlockSpec((tm, GLA_WIDTH), lambda i: (sidx(i), 0)),
            pl.BlockSpec((tm, GMLP_WIDTH), lambda i: (pidx(i), Z_U_OFF // GMLP_WIDTH)),
            pl.BlockSpec((tm, GMLP_WIDTH), lambda i: (pidx(i), Z_VG_OFF // GMLP_WIDTH)),
            pl.BlockSpec((tm, GMLP_WIDTH), lambda i: (sidx(i), Z_U_OFF // GMLP_WIDTH)),
            pl.BlockSpec((tm, GMLP_WIDTH), lambda i: (sidx(i), Z_VG_OFF // GMLP_WIDTH)),
            vec, vec,
            mixw, biasb, mixw, biasb,
            vec,
            pl.BlockSpec((GLA_WIDTH, D_MODEL), lambda i: (0, 0), pipeline_mode=once),
            pl.BlockSpec((GMLP_WIDTH, D_MODEL), lambda i: (1, 0), pipeline_mode=once),
        ],
        out_specs=[
            pl.BlockSpec((tm, D_MODEL), lambda i: (i, 0)),
            pl.BlockSpec((tm, GMLP_WIDTH), lambda i: (sidx(i), 0)),
        ],
        out_shape=[
            jax.ShapeDtypeStruct((hp.shape[0] + hs.shape[0], D_MODEL), F32),
            jax.ShapeDtypeStruct((hs.shape[0], GMLP_WIDTH), F32),
        ],
        compiler_params=_params("arbitrary"),
        name="mixout",
    )(hp, hs, op, os_, zp, zp, zs, zs, g_ln, b_ln, *_gmlp_operands(w_s, b_s, blk_p), *_gmlp_operands(w_s, b_s, blk_s),
      g_out, w_out, w_out)


def _ffn_kernel(h_ref, g_ref, wg_ref, wu_ref, wo_ref, o_ref, hb_scr, r_scr):
    j = pl.program_id(1)

    def tile(hb, r):
        gate = r * _dot(hb, wg_ref[...])
        up = r * _dot(hb, wu_ref[...])
        return _dot((jax.nn.silu(gate) * up).astype(BF16), wo_ref[...])

    @pl.when(j == 0)
    def _():
        h = h_ref[...]
        hb = (h * g_ref[...]).astype(BF16)
        hb_scr[...] = hb
        r = lax.rsqrt(jnp.mean(h * h, axis=-1, keepdims=True) + EPS)
        r_scr[...] = r
        o_ref[...] = h + tile(hb, r)

    @pl.when(j > 0)
    def _():
        o_ref[...] += tile(hb_scr[...], r_scr[...])


def _ffn(h1, g_ffn, w_in, w_out, tm, tf):
    m = h1.shape[0]
    nf = D_FF // tf
    return pl.pallas_call(
        _ffn_kernel,
        grid=(m // tm, nf),
        in_specs=[
            pl.BlockSpec((tm, D_MODEL), lambda i, j: (i, 0)),
            pl.BlockSpec((1, D_MODEL), lambda i, j: (0, 0)),
            pl.BlockSpec((D_MODEL, tf), lambda i, j: (0, j)),
            pl.BlockSpec((D_MODEL, tf), lambda i, j: (0, nf + j)),
            pl.BlockSpec((tf, D_MODEL), lambda i, j: (j, 0)),
        ],
        out_specs=pl.BlockSpec((tm, D_MODEL), lambda i, j: (i, 0)),
        out_shape=jax.ShapeDtypeStruct((m, D_MODEL), F32),
        scratch_shapes=[pltpu.VMEM((tm, D_MODEL), BF16), pltpu.VMEM((tm, 1), F32)],
        compiler_params=_params("parallel", "arbitrary"),
        name="ffn",
    )(h1, g_ffn, w_in, w_in, w_out)


PLE_COL_CHUNK = 512


def _ple_tile(h_ref, p_ref, wp_ref, gp_ref, gg_ref, wg_ref, gf_ref, y_ref):
    h = h_ref[...]
    r = lax.rsqrt(jnp.mean(h * h, axis=-1, keepdims=True) + EPS)
    hb = (h * gg_ref[...]).astype(BF16)
    pe = _rms(_dot(p_ref[...].astype(BF16), wp_ref[...]), gp_ref[...])
    ssq = jnp.zeros((h.shape[0], 1), F32)
    for c in range(D_MODEL // PLE_COL_CHUNK):
        cols = slice(c * PLE_COL_CHUNK, (c + 1) * PLE_COL_CHUNK)
        t = h[:, cols] + pe[:, cols] * jax.nn.sigmoid(r * _dot(hb, wg_ref[:, cols]))
        ssq = ssq + jnp.sum(t * t, axis=-1, keepdims=True)
        y_ref[:, cols] = t
    y_ref[...] = y_ref[...] * lax.rsqrt(ssq / D_MODEL + EPS) * gf_ref[...]


def _ple_kernel(h_ref, pp_ref, ps_ref, wp_ref, gp_ref, gg_ref, wg_ref, gf_ref, yp_ref, ys_ref, *, n_p):
    params = (wp_ref, gp_ref, gg_ref, wg_ref, gf_ref)
    i = pl.program_id(0)

    @pl.when(i < n_p)
    def _():
        _ple_tile(h_ref, pp_ref, *params, yp_ref)

    @pl.when(i >= n_p)
    def _():
        _ple_tile(h_ref, ps_ref, *params, ys_ref)


def _ple(h, pp, ps, w_ple, g_ple, g_gate, w_gate, g_final, tm):
    n_p, n_s = pp.shape[0] // tm, ps.shape[0] // tm
    pidx = lambda i: (jnp.minimum(i, n_p - 1), 0)
    sidx = lambda i: (jnp.maximum(i - n_p, 0), 0)
    vec = pl.BlockSpec((1, D_MODEL), lambda i: (0, 0))
    return pl.pallas_call(
        functools.partial(_ple_kernel, n_p=n_p),
        grid=(n_p + n_s,),
        in_specs=[
            pl.BlockSpec((tm, D_MODEL), lambda i: (i, 0)),
            pl.BlockSpec((tm, PLE_DIM), pidx),
            pl.BlockSpec((tm, PLE_DIM), sidx),
            pl.BlockSpec((PLE_DIM, D_MODEL), lambda i: (0, 0)),
            vec, vec,
            pl.BlockSpec((D_MODEL, D_MODEL), lambda i: (0, 0), pipeline_mode=pl.Buffered(1)),
            vec,
        ],
        out_specs=[pl.BlockSpec((tm, D_MODEL), pidx), pl.BlockSpec((tm, D_MODEL), sidx)],
        out_shape=[jax.ShapeDtypeStruct((pp.shape[0], D_MODEL), F32),
                   jax.ShapeDtypeStruct((ps.shape[0], D_MODEL), F32)],
        compiler_params=_params("arbitrary"),
        name="ple",
    )(h, pp, ps, w_ple, g_ple, g_gate, w_gate, g_final)


def kernel(x_prompt, x_sample, state_gla, p_prompt, p_sample, g_mix, w_in, w_a2, b_a, g_gla_norm, g_gmlp_ln,
           b_gmlp_ln, w_s, b_s, g_gmlp_out, w_out, g_ffn, w_ffn_in, w_ffn_out, w_ple, g_ple, g_ple_gate,
           w_ple_gate, g_final):
    batch, seq, _ = x_prompt.shape
    dec_batch, dec_seq, _ = x_sample.shape
    depth = w_in.shape[0]
    assert depth == 1, "one layer: the prompt / sample passes below are not chained over depth"
    row = lambda a: a.reshape(1, -1)
    wts = dict(
        g_mix=row(g_mix[0]),
        wa2_pad=jnp.pad(w_a2[0], ((0, V7X_LANES - GLA_GATE_RANK), (0, 0))),
        b_a=row(b_a[0]),
        g_gla_norm=row(g_gla_norm[0]),
        g_gmlp_ln=row(g_gmlp_ln[0]),
        b_gmlp_ln=row(b_gmlp_ln[0]),
        w_s=w_s[0],
        b_s=b_s[0],
        g_gmlp_out=row(g_gmlp_out[0]),
        g_ffn=row(g_ffn[0]),
        w_ple=w_ple[0].astype(BF16),
        g_ple=row(g_ple[0]),
        g_ple_gate=row(g_ple_gate[0]),
        g_final=row(g_final),
    )
    xp = x_prompt.reshape(batch * seq, D_MODEL)
    xs = x_sample.reshape(dec_batch * dec_seq, D_MODEL)
    wt_in = jnp.swapaxes(w_in, 1, 2)[0]
    zs, alow_s, wt16, wl16 = _inproj(xs, wts["g_mix"], wt_in, wt_in, tm=xs.shape[0], tn=1024, emit=True)
    zp, alow_p, wts["w_out"], wts["w_ple_gate"] = _inproj(
        xp, wts["g_mix"], wt16, wl16, tm=1024, tn=1024, emit=False, cast=(w_out[0], w_ple_gate[0]))
    gla = functools.partial(_gla, wa2_pad=wts["wa2_pad"], b_a=wts["b_a"], g_norm=wts["g_gla_norm"])
    op, sp, wts["w_ffn_in"], wts["w_ffn_out"] = gla(zp, alow_p, s0=None, n_seq=batch, seq_len=seq,
                                                     cast=(w_ffn_in[0], w_ffn_out[0]))
    os_, ss = gla(zs, alow_s, s0=state_gla[0], n_seq=dec_batch, seq_len=dec_seq)
    h1, vs = _mixout(xp, xs, op, os_, zp, zs, wts["g_gmlp_ln"], wts["b_gmlp_ln"], wts["w_s"], wts["b_s"],
                     wts["g_gmlp_out"], wts["w_out"], tm=512, len_p=seq, len_s=dec_seq)
    h2 = _ffn(h1, wts["g_ffn"], wts["w_ffn_in"], wts["w_ffn_out"], tm=1024, tf=512)
    yp, ys = _ple(h2, p_prompt[0].reshape(batch * seq, PLE_DIM), p_sample[0].reshape(dec_batch * dec_seq, PLE_DIM),
                  wts["w_ple"], wts["g_ple"], wts["g_ple_gate"], wts["w_ple_gate"], wts["g_final"], tm=512)
    return (yp.reshape(batch, seq, D_MODEL), ys.reshape(dec_batch, dec_seq, D_MODEL), sp[None], ss[None],
            vs.reshape(1, dec_batch, dec_seq, GMLP_WIDTH))
```

```python
import functools

import numpy as np

import jax
import jax.numpy as jnp
from jax import lax
from jax.experimental import pallas as pl
from jax.experimental.pallas import tpu as pltpu

F32 = jnp.float32
BF16 = jnp.bfloat16

D_MODEL = 2048
GLA_WIDTH = 1024
GMLP_WIDTH = 1024
GLA_HEADS = 4
GLA_DV = 256
GLA_DK = 128
GLA_KEY_WIDTH = 512
GLA_GATE_RANK = 16
GLA_TAU = 16.0
GLA_CHUNK = 64
GMLP_GROUPS = 8
GMLP_GROUP_DIM = 128
GMLP_CHUNK = 128
D_FF = 5632
PLE_DIM = 256
EPS = 1e-6
IN_OFFSETS = (0, 512, 1024, 2048, 3072, 3088, 4112, 5136)
Z_WIDTH = 5120
Z_GLA_WIDTH = 3072
Z_U_OFF = 3072
Z_VG_OFF = 4096

V7X_LANES = 128
V7X_VMEM_BYTES = 64 * 1024 * 1024
VMEM_LIMIT_BYTES = 60 * 1024 * 1024
N_CAST_BLOCKS = 16


def _params(*sem):
    return pltpu.CompilerParams(dimension_semantics=sem, vmem_limit_bytes=VMEM_LIMIT_BYTES)


def _rms(x, g):
    return x * lax.rsqrt(jnp.mean(x * x, axis=-1, keepdims=True) + EPS) * g


def _dot(a, b):
    return jnp.dot(a, b, preferred_element_type=F32)


def _dot_tn(a, b):
    return lax.dot_general(a, b, (((0,), (0,)), ((), ())), preferred_element_type=F32)


def _dot_nt(a, b):
    return lax.dot_general(a, b, (((1,), (1,)), ((), ())), preferred_element_type=F32)


def _cast_specs(cast, n_blocks, step_of):
    specs = []
    for w in cast:
        rows = w.shape[0] // n_blocks
        assert rows * n_blocks == w.shape[0] and rows % 16 == 0
        specs.append(pl.BlockSpec((rows, w.shape[1]),
                                  lambda *ids: (jnp.minimum(step_of(*ids), n_blocks - 1), 0)))
    return specs


def _cast_blocks(cast_in, cast_out):
    for src_ref, dst_ref in zip(cast_in, cast_out):
        dst_ref[...] = src_ref[...].astype(BF16)


def _inproj_kernel(*refs, emit, n_cast):
    x_ref, g_ref, w_ref, wl_ref = refs[:4]
    cast_in = refs[4:4 + n_cast]
    n_out = 4 if emit else 2
    z_ref, alow_ref = refs[4 + n_cast:6 + n_cast]
    emit_refs = refs[6 + n_cast:4 + n_cast + n_out]
    cast_out = refs[4 + n_cast + n_out:4 + 2 * n_cast + n_out]
    xb_scr, r_scr = refs[4 + 2 * n_cast + n_out:]
    j = pl.program_id(1)

    def weight(ref, emit_ref):
        if not emit:
            return ref[...]
        w16 = ref[...].astype(BF16)
        emit_ref[...] = w16
        return w16

    def z_tile(xb, r):
        return r * _dot_nt(xb, weight(w_ref, emit_refs[0] if emit else None))

    @pl.when(j == 0)
    def _():
        _cast_blocks(cast_in, cast_out)
        x = x_ref[...]
        xb = (x * g_ref[...]).astype(BF16)
        xb_scr[...] = xb
        r = lax.rsqrt(jnp.mean(x * x, axis=-1, keepdims=True) + EPS)
        r_scr[...] = r
        alow_ref[...] = r * _dot_nt(xb, weight(wl_ref, emit_refs[1] if emit else None))
        z_ref[...] = z_tile(xb, r).astype(BF16)

    @pl.when(j > 0)
    def _():
        _cast_blocks(cast_in, cast_out)
        z_ref[...] = z_tile(xb_scr[...], r_scr[...]).astype(BF16)


def _inproj(x, g_mix, w, wl, *, tm, tn, emit, cast=()):
    m = x.shape[0]
    n_a = Z_GLA_WIDTH // tn
    n_j = Z_WIDTH // tn
    if emit:
        def w_row(j):
            return pl.multiple_of(jnp.where(j < n_a, j * tn, IN_OFFSETS[5] + (j - n_a) * tn), 16)
        w_spec = pl.BlockSpec((pl.Element(tn), pl.Element(D_MODEL)), lambda i, j: (w_row(j), 0))
        wl_spec = pl.BlockSpec((V7X_LANES, D_MODEL), lambda i, j: (IN_OFFSETS[4] // V7X_LANES, 0))
        assert m == tm, "every weight tile must be visited exactly once"
    else:
        w_spec = pl.BlockSpec((tn, D_MODEL), lambda i, j: (j, 0))
        wl_spec = pl.BlockSpec((V7X_LANES, D_MODEL), lambda i, j: (0, 0))
    cast_specs = _cast_specs(cast, N_CAST_BLOCKS, lambda i, j: i * n_j + j)
    assert not cast or (m // tm) * n_j >= N_CAST_BLOCKS
    out_specs = [
        pl.BlockSpec((tm, tn), lambda i, j: (i, j)),
        pl.BlockSpec((tm, V7X_LANES), lambda i, j: (i, 0)),
    ]
    out_shape = [
        jax.ShapeDtypeStruct((m, Z_WIDTH), BF16),
        jax.ShapeDtypeStruct((m, V7X_LANES), F32),
    ]
    if emit:
        out_specs += [pl.BlockSpec((tn, D_MODEL), lambda i, j: (j, 0)),
                      pl.BlockSpec((V7X_LANES, D_MODEL), lambda i, j: (0, 0))]
        out_shape += [jax.ShapeDtypeStruct((Z_WIDTH, D_MODEL), BF16),
                      jax.ShapeDtypeStruct((V7X_LANES, D_MODEL), BF16)]
    return pl.pallas_call(
        functools.partial(_inproj_kernel, emit=emit, n_cast=len(cast)),
        grid=(m // tm, n_j),
        in_specs=[
            pl.BlockSpec((tm, D_MODEL), lambda i, j: (i, 0)),
            pl.BlockSpec((1, D_MODEL), lambda i, j: (0, 0)),
            w_spec,
            wl_spec,
        ] + cast_specs,
        out_specs=out_specs + cast_specs,
        out_shape=out_shape + [jax.ShapeDtypeStruct(w.shape, BF16) for w in cast],
        scratch_shapes=[pltpu.VMEM((tm, D_MODEL), BF16), pltpu.VMEM((tm, 1), F32)],
        compiler_params=_params("arbitrary" if cast else "parallel", "arbitrary"),
        name="inproj",
    )(x, g_mix, w, wl, *cast)


N_GLA_IN = 10


def _group_mask(R, C):
    t = np.arange(R)[:, None]
    s = np.arange(R)[None, :]
    return ((t // C == s // C) & (s <= t)).astype(np.float32)


def _gla_kernel(*refs, C, R, sequential, n_cast):
    q_ref, k_ref, v_ref, r_ref, alow_ref, wa2_ref, ba_ref, gn_ref, tri_ref, mask_ref = refs[:N_GLA_IN]
    if sequential:
        cast_in = refs[N_GLA_IN:N_GLA_IN + n_cast]
        o_ref, s_out_ref = refs[N_GLA_IN + n_cast:N_GLA_IN + n_cast + 2]
        cast_out = refs[N_GLA_IN + n_cast + 2:N_GLA_IN + 2 * n_cast + 2]
        s_scr, b_scr, e_scr = refs[N_GLA_IN + 2 * n_cast + 2:]
        _cast_blocks(cast_in, cast_out)
    else:
        s0_ref, o_ref, s_out_ref, b_scr, e_scr = refs[N_GLA_IN:]
    n = R // C
    mid = C // 2
    W = GLA_KEY_WIDTH

    x = _dot(alow_ref[...].astype(BF16), wa2_ref[...].astype(BF16)) + ba_ref[...]
    lb = (jnp.minimum(x, 0.0) - jnp.log(1.0 + jnp.exp(-jnp.abs(x)))) * (1.0 / GLA_TAU)

    tri = tri_ref[...]
    lb_hi = lb.astype(BF16)
    lb_lo = (lb - lb_hi.astype(F32)).astype(BF16)
    b = _dot(tri, lb_hi) + _dot(tri, lb_lo)
    b3 = b.reshape(n, C, W)
    b_mid = jnp.broadcast_to(b3[:, mid:mid + 1, :], (n, C, W)).reshape(R, W)
    b_last = jnp.broadcast_to(b3[:, C - 1:C, :], (n, C, W)).reshape(R, W)

    q = q_ref[...].astype(F32) * (GLA_DK ** -0.5)
    k = k_ref[...].astype(F32)
    q1 = (q * jnp.exp(b)).astype(BF16)
    q2 = (q * jnp.exp(b - b_mid)).astype(BF16)
    k2 = (k * jnp.exp(b_mid - b)).astype(BF16)
    k3 = k * jnp.exp(b_last - b)

    e_scr[...] = jnp.zeros((V7X_LANES, GLA_DK), F32)
    for h in range(GLA_HEADS):
        b_scr[h] = b[:, h * GLA_DK:(h + 1) * GLA_DK]
        e_scr[h * n:(h + 1) * n, :] = jnp.exp(b_scr[h, pl.ds(C - 1, n, stride=C), :])
    d_cols = e_scr[...].T

    causal = mask_ref[...] > 0.0
    if sequential:
        @pl.when(pl.program_id(1) == 0)
        def _():
            s_scr[...] = jnp.zeros((GLA_HEADS, GLA_DK, GLA_DV), F32)
    else:
        grp = lax.broadcasted_iota(jnp.int32, (R, 1), 0) >> (C.bit_length() - 1)

    for h in range(GLA_HEADS):
        ks = slice(h * GLA_DK, (h + 1) * GLA_DK)
        vs = slice(h * GLA_DV, (h + 1) * GLA_DV)
        v = v_ref[:, vs]
        att = jnp.where(causal, _dot_nt(q2[:, ks], k2[:, ks]), 0.0).astype(BF16)
        o = _dot(att, v)
        q1h = q1[:, ks]
        k3h = k3[:, ks]
        if sequential:
            s = s_scr[h]
            o_parts = []
            for g in range(n):
                rows = slice(g * C, (g + 1) * C)
                o_parts.append(o[rows] + _dot(q1h[rows], s.astype(BF16)))
                s = d_cols[:, h * n + g:h * n + g + 1] * s + _dot_tn(k3h[rows].astype(BF16), v[rows])
            s_scr[h] = s
            s_out_ref[0, h] = s
            o = jnp.concatenate(o_parts, axis=0)
        else:
            for g in range(n):
                s = s0_ref[g, h]
                o = o + jnp.where(grp == g, _dot(q1h, s.astype(BF16)), 0.0)
                k3g = jnp.where(grp == g, k3h, 0.0).astype(BF16)
                s_out_ref[g, h] = d_cols[:, h * n + g:h * n + g + 1] * s + _dot_tn(k3g, v)
        o = _rms(o, gn_ref[:, vs]) * jax.nn.silu(r_ref[:, vs].astype(F32))
        o_ref[:, vs] = o.astype(BF16)


def _gla(z, alow, *, wa2_pad, b_a, g_norm, s0, n_seq, seq_len, cast=()):
    m = z.shape[0]
    sequential = s0 is None
    assert sequential or not cast
    if sequential:
        C, R = GLA_CHUNK, 8 * GLA_CHUNK
        nt = seq_len // R
        grid = (n_seq, nt)
        rowblk = lambda a, c: a * nt + c
        s_block = (1, GLA_HEADS, GLA_DK, GLA_DV)
    else:
        C, R = seq_len, 128
        grid = (m // R, 1)
        rowblk = lambda a, c: a
        s_block = (R // C, GLA_HEADS, GLA_DK, GLA_DV)
    s_map = lambda a, c: (a, 0, 0, 0)
    mask = _group_mask(R, C)
    const = lambda a, c: (0, 0)
    in_specs = [
        pl.BlockSpec((R, GLA_KEY_WIDTH), lambda a, c: (rowblk(a, c), 0)),
        pl.BlockSpec((R, GLA_KEY_WIDTH), lambda a, c: (rowblk(a, c), 1)),
        pl.BlockSpec((R, GLA_WIDTH), lambda a, c: (rowblk(a, c), 1)),
        pl.BlockSpec((R, GLA_WIDTH), lambda a, c: (rowblk(a, c), 2)),
        pl.BlockSpec((R, V7X_LANES), lambda a, c: (rowblk(a, c), 0)),
        pl.BlockSpec((V7X_LANES, GLA_KEY_WIDTH), const),
        pl.BlockSpec((1, GLA_KEY_WIDTH), const),
        pl.BlockSpec((1, GLA_WIDTH), const),
        pl.BlockSpec((R, R), const),
        pl.BlockSpec((R, R), const),
    ]
    args = [z, z, z, z, alow, wa2_pad, b_a, g_norm, jnp.asarray(mask, BF16), jnp.asarray(mask, F32)]
    assert len(args) == N_GLA_IN
    scratch = [pltpu.VMEM((GLA_HEADS, R, GLA_DK), F32), pltpu.VMEM((V7X_LANES, GLA_DK), F32)]
    out_specs = [
        pl.BlockSpec((R, GLA_WIDTH), lambda a, c: (rowblk(a, c), 0)),
        pl.BlockSpec(s_block, s_map),
    ]
    out_shape = [
        jax.ShapeDtypeStruct((m, GLA_WIDTH), BF16),
        jax.ShapeDtypeStruct((n_seq, GLA_HEADS, GLA_DK, GLA_DV), F32),
    ]
    if sequential:
        scratch = [pltpu.VMEM((GLA_HEADS, GLA_DK, GLA_DV), F32)] + scratch
        assert not cast or grid[0] * grid[1] >= N_CAST_BLOCKS
        cast_specs = _cast_specs(cast, N_CAST_BLOCKS, rowblk)
        in_specs += cast_specs
        out_specs += cast_specs
        out_shape += [jax.ShapeDtypeStruct(w.shape, BF16) for w in cast]
        args.extend(cast)
    else:
        in_specs.append(pl.BlockSpec(s_block, s_map))
        args.append(s0)
    return pl.pallas_call(
        functools.partial(_gla_kernel, C=C, R=R, sequential=sequential, n_cast=len(cast)),
        grid=grid,
        in_specs=in_specs,
        out_specs=out_specs,
        out_shape=out_shape,
        scratch_shapes=scratch,
        compiler_params=_params("parallel", "arbitrary"),
        name="gla_seq" if sequential else "gla_step",
    )(*args)


def _mixout_tile(h_ref, o_ref, u_ref, vg_ref, gln_ref, bln_ref, ws_ref, bias_ref, gout_ref, wo_ref, wm_ref,
                 h1_ref, vn_ref, *, tm, blk):
    n_chunks = tm // GMLP_CHUNK
    col_w = D_MODEL // n_chunks
    shift = blk.bit_length() - 1
    row = lax.broadcasted_iota(jnp.int32, (GMLP_CHUNK, GMLP_CHUNK), 0)
    col = lax.broadcasted_iota(jnp.int32, (GMLP_CHUNK, GMLP_CHUNK), 1)
    causal = jnp.logical_and((row >> shift) == (col >> shift), col <= row)
    w = [jnp.where(causal, ws_ref[g], 0.0).astype(BF16) for g in range(GMLP_GROUPS)]
    m_parts = []
    for c in range(n_chunks):
        cols = slice(c * col_w, (c + 1) * col_w)
        h1_ref[:, cols] = h_ref[:, cols] + _dot(o_ref[...], wo_ref[:, cols])
        rows = pl.ds(c * GMLP_CHUNK, GMLP_CHUNK)
        vg = jax.nn.gelu(vg_ref[rows, :].astype(F32))
        mu = jnp.mean(vg, axis=-1, keepdims=True)
        vc = vg - mu
        vn = vc * lax.rsqrt(jnp.mean(vc * vc, axis=-1, keepdims=True) + EPS) * gln_ref[...] + bln_ref[...]
        if vn_ref is not None:
            vn_ref[rows, :] = vn
        vn16 = vn.astype(BF16)
        mixed = jnp.concatenate(
            [_dot(w[g], vn16[:, g * GMLP_GROUP_DIM:(g + 1) * GMLP_GROUP_DIM]) for g in range(GMLP_GROUPS)],
            axis=1) + bias_ref[...]
        mm = jax.nn.gelu(u_ref[rows, :].astype(F32)) * mixed
        m_parts.append(_rms(mm, gout_ref[...]).astype(BF16))
    h1_ref[...] += _dot(jnp.concatenate(m_parts, axis=0), wm_ref[...])


def _mixout_kernel(hp_ref, hs_ref, op_ref, os_ref, up_ref, vgp_ref, us_ref, vgs_ref, gln_ref, bln_ref,
                   wsp_ref, biasp_ref, wss_ref, biass_ref, gout_ref, wo_ref, wm_ref, h1_ref, vn_ref,
                   *, tm, n_p, blk_p, blk_s):
    shared = (gln_ref, bln_ref)
    tail = (gout_ref, wo_ref, wm_ref, h1_ref)
    i = pl.program_id(0)

    @pl.when(i < n_p)
    def _():
        _mixout_tile(hp_ref, op_ref, up_ref, vgp_ref, *shared, wsp_ref, biasp_ref, *tail, None, tm=tm, blk=blk_p)

    @pl.when(i >= n_p)
    def _():
        _mixout_tile(hs_ref, os_ref, us_ref, vgs_ref, *shared, wss_ref, biass_ref, *tail, vn_ref, tm=tm, blk=blk_s)


def _gmlp_operands(w_s, b_s, blk):
    reps = GMLP_CHUNK // blk
    sel = jnp.asarray(np.tile(np.eye(blk, dtype=np.float32), (reps, 1)))
    w_mix = jnp.einsum("ra,gab,cb->grc", sel, w_s[:, :blk, :blk], sel, precision=lax.Precision.HIGHEST)
    bias = jnp.tile(jnp.repeat(b_s[:, :blk].T, GMLP_GROUP_DIM, axis=1), (reps, 1))
    return w_mix, bias


def _mixout(hp, hs, op, os_, zp, zs, g_ln, b_ln, w_s, b_s, g_out, w_out, *, tm, len_p, len_s):
    n_p, n_s = hp.shape[0] // tm, hs.shape[0] // tm
    blk_p, blk_s = min(len_p, GMLP_CHUNK), min(len_s, GMLP_CHUNK)
    pidx = lambda i: jnp.minimum(i, n_p - 1)
    sidx = lambda i: jnp.maximum(i - n_p, 0)
    once = pl.Buffered(1)
    vec = pl.BlockSpec((1, GMLP_WIDTH), lambda i: (0, 0))
    mixw = pl.BlockSpec((GMLP_GROUPS, GMLP_CHUNK, GMLP_CHUNK), lambda i: (0, 0, 0))
    biasb = pl.BlockSpec((GMLP_CHUNK, GMLP_WIDTH), lambda i: (0, 0))
    return pl.pallas_call(
        functools.partial(_mixout_kernel, tm=tm, n_p=n_p, blk_p=blk_p, blk_s=blk_s),
        grid=(n_p + n_s,),
        in_specs=[
            pl.BlockSpec((tm, D_MODEL), lambda i: (pidx(i), 0)),
            pl.BlockSpec((tm, D_MODEL), lambda i: (sidx(i), 0)),
            pl.BlockSpec((tm, GLA_WIDTH), lambda i: (pidx(i), 0)),
            pl.BlockSpec((tm, GLA_WIDTH), lambda i: (sidx(i), 0)),
            pl.BlockSpec((tm, GMLP_WIDTH), lambda i: (pidx(i), Z_U_OFF // GMLP_WIDTH)),
            pl.BlockSpec((tm, GMLP_WIDTH), lambda i: (pidx(i), Z_VG_OFF // GMLP_WIDTH)),
            pl.BlockSpec((tm, GMLP_WIDTH), lambda i: (sidx(i), Z_U_OFF // GMLP_WIDTH)),
            pl.BlockSpec((tm, GMLP_WIDTH), lambda i: (sidx(i), Z_VG_OFF // GMLP_WIDTH)),
            vec, vec,
            mixw, biasb, mixw, biasb,
            vec,
            pl.BlockSpec((GLA_WIDTH, D_MODEL), lambda i: (0, 0), pipeline_mode=once),
            pl.BlockSpec((GMLP_WIDTH, D_MODEL), lambda i: (1, 0), pipeline_mode=once),
        ],
        out_specs=[
            pl.BlockSpec((tm, D_MODEL), lambda i: (i, 0)),
            pl.BlockSpec((tm, GMLP_WIDTH), lambda i: (sidx(i), 0)),
        ],
        out_shape=[
            jax.ShapeDtypeStruct((hp.shape[0] + hs.shape[0], D_MODEL), F32),
            jax.ShapeDtypeStruct((hs.shape[0], GMLP_WIDTH), F32),
        ],
        compiler_params=_params("arbitrary"),
        name="mixout",
    )(hp, hs, op, os_, zp, zp, zs, zs, g_ln, b_ln, *_gmlp_operands(w_s, b_s, blk_p), *_gmlp_operands(w_s, b_s, blk_s),
      g_out, w_out, w_out)


def _ffn_kernel(h_ref, g_ref, wg_ref, wu_ref, wo_ref, o_ref, hb_scr, r_scr):
    j = pl.program_id(1)

    def tile(hb, r):
        gate = r * _dot(hb, wg_ref[...])
        up = r * _dot(hb, wu_ref[...])
        return _dot((jax.nn.silu(gate) * up).astype(BF16), wo_ref[...])

    @pl.when(j == 0)
    def _():
        h = h_ref[...]
        hb = (h * g_ref[...]).astype(BF16)
        hb_scr[...] = hb
        r = lax.rsqrt(jnp.mean(h * h, axis=-1, keepdims=True) + EPS)
        r_scr[...] = r
        o_ref[...] = h + tile(hb, r)

    @pl.when(j > 0)
    def _():
        o_ref[...] += tile(hb_scr[...], r_scr[...])


def _ffn(h1, g_ffn, w_in, w_out, tm, tf):
    m = h1.shape[0]
    nf = D_FF // tf
    return pl.pallas_call(
        _ffn_kernel,
        grid=(m // tm, nf),
        in_specs=[
            pl.BlockSpec((tm, D_MODEL), lambda i, j: (i, 0)),
            pl.BlockSpec((1, D_MODEL), lambda i, j: (0, 0)),
            pl.BlockSpec((D_MODEL, tf), lambda i, j: (0, j)),
            pl.BlockSpec((D_MODEL, tf), lambda i, j: (0, nf + j)),
            pl.BlockSpec((tf, D_MODEL), lambda i, j: (j, 0)),
        ],
        out_specs=pl.BlockSpec((tm, D_MODEL), lambda i, j: (i, 0)),
        out_shape=jax.ShapeDtypeStruct((m, D_MODEL), F32),
        scratch_shapes=[pltpu.VMEM((tm, D_MODEL), BF16), pltpu.VMEM((tm, 1), F32)],
        compiler_params=_params("parallel", "arbitrary"),
        name="ffn",
    )(h1, g_ffn, w_in, w_in, w_out)


PLE_COL_CHUNK = 512


def _ple_tile(h_ref, p_ref, wp_ref, gp_ref, gg_ref, wg_ref, gf_ref, y_ref):
    h = h_ref[...]
    r = lax.rsqrt(jnp.mean(h * h, axis=-1, keepdims=True) + EPS)
    hb = (h * gg_ref[...]).astype(BF16)
    pe = _rms(_dot(p_ref[...].astype(BF16), wp_ref[...]), gp_ref[...])
    ssq = jnp.zeros((h.shape[0], 1), F32)
    for c in range(D_MODEL // PLE_COL_CHUNK):
        cols = slice(c * PLE_COL_CHUNK, (c + 1) * PLE_COL_CHUNK)
        t = h[:, cols] + pe[:, cols] * jax.nn.sigmoid(r * _dot(hb, wg_ref[:, cols]))
        ssq = ssq + jnp.sum(t * t, axis=-1, keepdims=True)
        y_ref[:, cols] = t
    y_ref[...] = y_ref[...] * lax.rsqrt(ssq / D_MODEL + EPS) * gf_ref[...]


def _ple_kernel(h_ref, pp_ref, ps_ref, wp_ref, gp_ref, gg_ref, wg_ref, gf_ref, yp_ref, ys_ref, *, n_p):
    params = (wp_ref, gp_ref, gg_ref, wg_ref, gf_ref)
    i = pl.program_id(0)

    @pl.when(i < n_p)
    def _():
        _ple_tile(h_ref, pp_ref, *params, yp_ref)

    @pl.when(i >= n_p)
    def _():
        _ple_tile(h_ref, ps_ref, *params, ys_ref)


def _ple(h, pp, ps, w_ple, g_ple, g_gate, w_gate, g_final, tm):
    n_p, n_s = pp.shape[0] // tm, ps.shape[0] // tm
    pidx = lambda i: (jnp.minimum(i, n_p - 1), 0)
    sidx = lambda i: (jnp.maximum(i - n_p, 0), 0)
    vec = pl.BlockSpec((1, D_MODEL), lambda i: (0, 0))
    return pl.pallas_call(
        functools.partial(_ple_kernel, n_p=n_p),
        grid=(n_p + n_s,),
        in_specs=[
            pl.BlockSpec((tm, D_MODEL), lambda i: (i, 0)),
            pl.BlockSpec((tm, PLE_DIM), pidx),
            pl.BlockSpec((tm, PLE_DIM), sidx),
            pl.BlockSpec((PLE_DIM, D_MODEL), lambda i: (0, 0)),
            vec, vec,
            pl.BlockSpec((D_MODEL, D_MODEL), lambda i: (0, 0), pipeline_mode=pl.Buffered(1)),
            vec,
        ],
        out_specs=[pl.BlockSpec((tm, D_MODEL), pidx), pl.BlockSpec((tm, D_MODEL), sidx)],
        out_shape=[jax.ShapeDtypeStruct((pp.shape[0], D_MODEL), F32),
                   jax.ShapeDtypeStruct((ps.shape[0], D_MODEL), F32)],
        compiler_params=_params("arbitrary"),
        name="ple",
    )(h, pp, ps, w_ple, g_ple, g_gate, w_gate, g_final)


def kernel(x_prompt, x_sample, state_gla, p_prompt, p_sample, g_mix, w_in, w_a2, b_a, g_gla_norm, g_gmlp_ln,
           b_gmlp_ln, w_s, b_s, g_gmlp_out, w_out, g_ffn, w_ffn_in, w_ffn_out, w_ple, g_ple, g_ple_gate,
           w_ple_gate, g_final):
    batch, seq, _ = x_prompt.shape
    dec_batch, dec_seq, _ = x_sample.shape
    depth = w_in.shape[0]
    assert depth == 1, "one layer: the prompt / sample passes below are not chained over depth"
    row = lambda a: a.reshape(1, -1)
    wts = dict(
        g_mix=row(g_mix[0]),
        wa2_pad=jnp.pad(w_a2[0], ((0, V7X_LANES - GLA_GATE_RANK), (0, 0))),
        b_a=row(b_a[0]),
        g_gla_norm=row(g_gla_norm[0]),
        g_gmlp_ln=row(g_gmlp_ln[0]),
        b_gmlp_ln=row(b_gmlp_ln[0]),
        w_s=w_s[0],
        b_s=b_s[0],
        g_gmlp_out=row(g_gmlp_out[0]),
        g_ffn=row(g_ffn[0]),
        w_ple=w_ple[0].astype(BF16),
        g_ple=row(g_ple[0]),
        g_ple_gate=row(g_ple_gate[0]),
        g_final=row(g_final),
    )
    xp = x_prompt.reshape(batch * seq, D_MODEL)
    xs = x_sample.reshape(dec_batch * dec_seq, D_MODEL)
    wt_in = jnp.swapaxes(w_in, 1, 2)[0]
    zs, alow_s, wt16, wl16 = _inproj(xs, wts["g_mix"], wt_in, wt_in, tm=xs.shape[0], tn=1024, emit=True)
    zp, alow_p, wts["w_out"], wts["w_ple_gate"] = _inproj(
        xp, wts["g_mix"], wt16, wl16, tm=1024, tn=1024, emit=False, cast=(w_out[0], w_ple_gate[0]))
    gla = functools.partial(_gla, wa2_pad=wts["wa2_pad"], b_a=wts["b_a"], g_norm=wts["g_gla_norm"])
    op, sp, wts["w_ffn_in"], wts["w_ffn_out"] = gla(zp, alow_p, s0=None, n_seq=batch, seq_len=seq,
                                                     cast=(w_ffn_in[0], w_ffn_out[0]))
    os_, ss = gla(zs, alow_s, s0=state_gla[0], n_seq=dec_batch, seq_len=dec_seq)
    h1, vs = _mixout(xp, xs, op, os_, zp, zs, wts["g_gmlp_ln"], wts["b_gmlp_ln"], wts["w_s"], wts["b_s"],
                     wts["g_gmlp_out"], wts["w_out"], tm=512, len_p=seq, len_s=dec_seq)
    h2 = _ffn(h1, wts["g_ffn"], wts["w_ffn_in"], wts["w_ffn_out"], tm=1024, tf=512)
    yp, ys = _ple(h2, p_prompt[0].reshape(batch * seq, PLE_DIM), p_sample[0].reshape(dec_batch * dec_seq, PLE_DIM),
                  wts["w_ple"], wts["g_ple"], wts["g_ple_gate"], wts["w_ple_gate"], wts["g_final"], tm=512)
    return (yp.reshape(batch, seq, D_MODEL), ys.reshape(dec_batch, dec_seq, D_MODEL), sp[None], ss[None],
            vs.reshape(1, dec_batch, dec_seq, GMLP_WIDTH))
```

```python
import functools

import numpy as np

import jax
import jax.numpy as jnp
from jax import lax
from jax.experimental import pallas as pl
from jax.experimental.pallas import tpu as pltpu

F32 = jnp.float32
BF16 = jnp.bfloat16

D_MODEL = 2048
GLA_WIDTH = 1024
GMLP_WIDTH = 1024
GLA_HEADS = 4
GLA_DV = 256
GLA_DK = 128
GLA_KEY_WIDTH = 512
GLA_GATE_RANK = 16
GLA_TAU = 16.0
GLA_CHUNK = 64
GMLP_GROUPS = 8
GMLP_GROUP_DIM = 128
GMLP_CHUNK = 128
D_FF = 5632
PLE_DIM = 256
EPS = 1e-6
IN_OFFSETS = (0, 512, 1024, 2048, 3072, 3088, 4112, 5136)
Z_WIDTH = 5120
Z_GLA_WIDTH = 3072
Z_U_OFF = 3072
Z_VG_OFF = 4096

V7X_LANES = 128
V7X_BF16_ROWS = 16
V7X_VMEM_BYTES = 64 * 1024 * 1024
VMEM_LIMIT_BYTES = V7X_VMEM_BYTES - 4 * 1024 * 1024
INPROJ_ROWS, INPROJ_COLS = 1024, 1024
MIX_ROWS = 512
FFN_ROWS, FFN_COLS = 1024, 512
PLE_ROWS, PLE_COL_CHUNK = 512, 1024
GLA_SEQ_ROWS = 8 * GLA_CHUNK
GLA_STEP_ROWS = 128
N_CAST_BLOCKS = 16


def _params(*sem):
    return pltpu.CompilerParams(dimension_semantics=sem, vmem_limit_bytes=VMEM_LIMIT_BYTES)


def _rms(x, g):
    return x * lax.rsqrt(jnp.mean(x * x, axis=-1, keepdims=True) + EPS) * g


def _dot(a, b):
    return jnp.dot(a, b, preferred_element_type=F32)


def _dot_tn(a, b):
    return lax.dot_general(a, b, (((0,), (0,)), ((), ())), preferred_element_type=F32)


def _dot_nt(a, b):
    return lax.dot_general(a, b, (((1,), (1,)), ((), ())), preferred_element_type=F32)


def _cast_specs(cast, n_blocks, step_of):
    specs = []
    for w in cast:
        rows = w.shape[0] // n_blocks
        assert rows * n_blocks == w.shape[0] and rows % V7X_BF16_ROWS == 0
        specs.append(pl.BlockSpec((rows, w.shape[1]),
                                  lambda *ids: (jnp.minimum(step_of(*ids), n_blocks - 1), 0)))
    return specs


def _cast_blocks(cast_in, cast_out):
    for src_ref, dst_ref in zip(cast_in, cast_out):
        dst_ref[...] = src_ref[...].astype(BF16)


def _inproj_kernel(*refs, emit, n_cast):
    x_ref, g_ref, w_ref, wl_ref = refs[:4]
    cast_in = refs[4:4 + n_cast]
    n_out = 4 if emit else 2
    z_ref, alow_ref = refs[4 + n_cast:6 + n_cast]
    emit_refs = refs[6 + n_cast:4 + n_cast + n_out]
    cast_out = refs[4 + n_cast + n_out:4 + 2 * n_cast + n_out]
    xb_scr, r_scr = refs[4 + 2 * n_cast + n_out:]
    j = pl.program_id(1)

    def weight(ref, emit_ref):
        if not emit:
            return ref[...]
        w16 = ref[...].astype(BF16)
        emit_ref[...] = w16
        return w16

    def z_tile(xb, r):
        return r * _dot_nt(xb, weight(w_ref, emit_refs[0] if emit else None))

    @pl.when(j == 0)
    def _():
        _cast_blocks(cast_in, cast_out)
        x = x_ref[...]
        xb = (x * g_ref[...]).astype(BF16)
        xb_scr[...] = xb
        r = lax.rsqrt(jnp.mean(x * x, axis=-1, keepdims=True) + EPS)
        r_scr[...] = r
        alow_ref[...] = r * _dot_nt(xb, weight(wl_ref, emit_refs[1] if emit else None))
        z_ref[...] = z_tile(xb, r).astype(BF16)

    @pl.when(j > 0)
    def _():
        _cast_blocks(cast_in, cast_out)
        z_ref[...] = z_tile(xb_scr[...], r_scr[...]).astype(BF16)


def _inproj(x, g_mix, w, wl, *, tm, tn, emit, cast=()):
    m = x.shape[0]
    n_a = Z_GLA_WIDTH // tn
    n_j = Z_WIDTH // tn
    if emit:
        def w_row(j):
            return pl.multiple_of(jnp.where(j < n_a, j * tn, IN_OFFSETS[5] + (j - n_a) * tn), V7X_BF16_ROWS)
        w_spec = pl.BlockSpec((pl.Element(tn), pl.Element(D_MODEL)), lambda i, j: (w_row(j), 0))
        wl_spec = pl.BlockSpec((V7X_LANES, D_MODEL), lambda i, j: (IN_OFFSETS[4] // V7X_LANES, 0))
        assert m == tm, "every weight tile must be visited exactly once"
    else:
        w_spec = pl.BlockSpec((tn, D_MODEL), lambda i, j: (j, 0))
        wl_spec = pl.BlockSpec((V7X_LANES, D_MODEL), lambda i, j: (0, 0))
    cast_specs = _cast_specs(cast, N_CAST_BLOCKS, lambda i, j: i * n_j + j)
    assert not cast or (m // tm) * n_j >= N_CAST_BLOCKS
    out_specs = [
        pl.BlockSpec((tm, tn), lambda i, j: (i, j)),
        pl.BlockSpec((tm, V7X_LANES), lambda i, j: (i, 0)),
    ]
    out_shape = [
        jax.ShapeDtypeStruct((m, Z_WIDTH), BF16),
        jax.ShapeDtypeStruct((m, V7X_LANES), F32),
    ]
    if emit:
        out_specs += [pl.BlockSpec((tn, D_MODEL), lambda i, j: (j, 0)),
                      pl.BlockSpec((V7X_LANES, D_MODEL), lambda i, j: (0, 0))]
        out_shape += [jax.ShapeDtypeStruct((Z_WIDTH, D_MODEL), BF16),
                      jax.ShapeDtypeStruct((V7X_LANES, D_MODEL), BF16)]
    return pl.pallas_call(
        functools.partial(_inproj_kernel, emit=emit, n_cast=len(cast)),
        grid=(m // tm, n_j),
        in_specs=[
            pl.BlockSpec((tm, D_MODEL), lambda i, j: (i, 0)),
            pl.BlockSpec((1, D_MODEL), lambda i, j: (0, 0)),
            w_spec,
            wl_spec,
        ] + cast_specs,
        out_specs=out_specs + cast_specs,
        out_shape=out_shape + [jax.ShapeDtypeStruct(w.shape, BF16) for w in cast],
        scratch_shapes=[pltpu.VMEM((tm, D_MODEL), BF16), pltpu.VMEM((tm, 1), F32)],
        compiler_params=_params("arbitrary" if cast else "parallel", "arbitrary"),
        name="inproj",
    )(x, g_mix, w, wl, *cast)


N_GLA_IN = 10


def _group_mask(R, C):
    t = np.arange(R)[:, None]
    s = np.arange(R)[None, :]
    return ((t // C == s // C) & (s <= t)).astype(np.float32)


def _gla_kernel(*refs, C, R, sequential, n_cast):
    q_ref, k_ref, v_ref, r_ref, alow_ref, wa2_ref, ba_ref, gn_ref, tri_ref, mask_ref = refs[:N_GLA_IN]
    if sequential:
        cast_in = refs[N_GLA_IN:N_GLA_IN + n_cast]
        o_ref, s_out_ref = refs[N_GLA_IN + n_cast:N_GLA_IN + n_cast + 2]
        cast_out = refs[N_GLA_IN + n_cast + 2:N_GLA_IN + 2 * n_cast + 2]
        s_scr, b_scr, e_scr = refs[N_GLA_IN + 2 * n_cast + 2:]
        _cast_blocks(cast_in, cast_out)
    else:
        s0_ref, o_ref, s_out_ref, b_scr, e_scr = refs[N_GLA_IN:]
    n = R // C
    mid = C // 2
    W = GLA_KEY_WIDTH

    x = _dot(alow_ref[...].astype(BF16), wa2_ref[...].astype(BF16)) + ba_ref[...]
    lb = (jnp.minimum(x, 0.0) - jnp.log(1.0 + jnp.exp(-jnp.abs(x)))) * (1.0 / GLA_TAU)

    tri = tri_ref[...]
    lb_hi = lb.astype(BF16)
    lb_lo = (lb - lb_hi.astype(F32)).astype(BF16)
    b = _dot(tri, lb_hi) + _dot(tri, lb_lo)
    b3 = b.reshape(n, C, W)
    b_mid = jnp.broadcast_to(b3[:, mid:mid + 1, :], (n, C, W)).reshape(R, W)
    b_last = jnp.broadcast_to(b3[:, C - 1:C, :], (n, C, W)).reshape(R, W)

    q = q_ref[...].astype(F32) * (GLA_DK ** -0.5)
    k = k_ref[...].astype(F32)
    q1 = (q * jnp.exp(b)).astype(BF16)
    q2 = (q * jnp.exp(b - b_mid)).astype(BF16)
    k2 = (k * jnp.exp(b_mid - b)).astype(BF16)
    k3 = k * jnp.exp(b_last - b)

    e_scr[...] = jnp.zeros((V7X_LANES, GLA_DK), F32)
    for h in range(GLA_HEADS):
        b_scr[h] = b[:, h * GLA_DK:(h + 1) * GLA_DK]
        e_scr[h * n:(h + 1) * n, :] = jnp.exp(b_scr[h, pl.ds(C - 1, n, stride=C), :])
    d_cols = e_scr[...].T

    causal = mask_ref[...] > 0.0
    if sequential:
        @pl.when(pl.program_id(1) == 0)
        def _():
            s_scr[...] = jnp.zeros((GLA_HEADS, GLA_DK, GLA_DV), F32)
    else:
        grp = lax.broadcasted_iota(jnp.int32, (R, 1), 0) >> (C.bit_length() - 1)

    for h in range(GLA_HEADS):
        ks = slice(h * GLA_DK, (h + 1) * GLA_DK)
        vs = slice(h * GLA_DV, (h + 1) * GLA_DV)
        v = v_ref[:, vs]
        att = jnp.where(causal, _dot_nt(q2[:, ks], k2[:, ks]), 0.0).astype(BF16)
        o = _dot(att, v)
        q1h = q1[:, ks]
        k3h = k3[:, ks]
        if sequential:
            s = s_scr[h]
            o_parts = []
            for g in range(n):
                rows = slice(g * C, (g + 1) * C)
                o_parts.append(o[rows] + _dot(q1h[rows], s.astype(BF16)))
                s = d_cols[:, h * n + g:h * n + g + 1] * s + _dot_tn(k3h[rows].astype(BF16), v[rows])
            s_scr[h] = s
            s_out_ref[0, h] = s
            o = jnp.concatenate(o_parts, axis=0)
        else:
            for g in range(n):
                s = s0_ref[g, h]
                o = o + jnp.where(grp == g, _dot(q1h, s.astype(BF16)), 0.0)
                k3g = jnp.where(grp == g, k3h, 0.0).astype(BF16)
                s_out_ref[g, h] = d_cols[:, h * n + g:h * n + g + 1] * s + _dot_tn(k3g, v)
        o = _rms(o, gn_ref[:, vs]) * jax.nn.silu(r_ref[:, vs].astype(F32))
        o_ref[:, vs] = o.astype(BF16)


def _gla(z, alow, *, wa2_pad, b_a, g_norm, s0, n_seq, seq_len, cast=()):
    m = z.shape[0]
    sequential = s0 is None
    assert sequential or not cast
    if sequential:
        C, R = GLA_CHUNK, GLA_SEQ_ROWS
        nt = seq_len // R
        grid = (n_seq, nt)
        rowblk = lambda a, c: a * nt + c
        s_block = (1, GLA_HEADS, GLA_DK, GLA_DV)
    else:
        C, R = seq_len, GLA_STEP_ROWS
        grid = (m // R, 1)
        rowblk = lambda a, c: a
        s_block = (R // C, GLA_HEADS, GLA_DK, GLA_DV)
    s_map = lambda a, c: (a, 0, 0, 0)
    mask = _group_mask(R, C)
    const = lambda a, c: (0, 0)
    in_specs = [
        pl.BlockSpec((R, GLA_KEY_WIDTH), lambda a, c: (rowblk(a, c), 0)),
        pl.BlockSpec((R, GLA_KEY_WIDTH), lambda a, c: (rowblk(a, c), 1)),
        pl.BlockSpec((R, GLA_WIDTH), lambda a, c: (rowblk(a, c), 1)),
        pl.BlockSpec((R, GLA_WIDTH), lambda a, c: (rowblk(a, c), 2)),
        pl.BlockSpec((R, V7X_LANES), lambda a, c: (rowblk(a, c), 0)),
        pl.BlockSpec((V7X_LANES, GLA_KEY_WIDTH), const),
        pl.BlockSpec((1, GLA_KEY_WIDTH), const),
        pl.BlockSpec((1, GLA_WIDTH), const),
        pl.BlockSpec((R, R), const),
        pl.BlockSpec((R, R), const),
    ]
    args = [z, z, z, z, alow, wa2_pad, b_a, g_norm, jnp.asarray(mask, BF16), jnp.asarray(mask, F32)]
    assert len(args) == N_GLA_IN
    scratch = [pltpu.VMEM((GLA_HEADS, R, GLA_DK), F32), pltpu.VMEM((V7X_LANES, GLA_DK), F32)]
    out_specs = [
        pl.BlockSpec((R, GLA_WIDTH), lambda a, c: (rowblk(a, c), 0)),
        pl.BlockSpec(s_block, s_map),
    ]
    out_shape = [
        jax.ShapeDtypeStruct((m, GLA_WIDTH), BF16),
        jax.ShapeDtypeStruct((n_seq, GLA_HEADS, GLA_DK, GLA_DV), F32),
    ]
    if sequential:
        scratch = [pltpu.VMEM((GLA_HEADS, GLA_DK, GLA_DV), F32)] + scratch
        assert not cast or grid[0] * grid[1] >= N_CAST_BLOCKS
        cast_specs = _cast_specs(cast, N_CAST_BLOCKS, rowblk)
        in_specs += cast_specs
        out_specs += cast_specs
        out_shape += [jax.ShapeDtypeStruct(w.shape, BF16) for w in cast]
        args.extend(cast)
    else:
        in_specs.append(pl.BlockSpec(s_block, s_map))
        args.append(s0)
    return pl.pallas_call(
        functools.partial(_gla_kernel, C=C, R=R, sequential=sequential, n_cast=len(cast)),
        grid=grid,
        in_specs=in_specs,
        out_specs=out_specs,
        out_shape=out_shape,
        scratch_shapes=scratch,
        compiler_params=_params("parallel", "arbitrary"),
        name="gla_seq" if sequential else "gla_step",
    )(*args)


def _mixout_tile(h_ref, o_ref, u_ref, vg_ref, gln_ref, bln_ref, ws_ref, bias_ref, gout_ref, wo_ref, wm_ref,
                 h1_ref, vn_ref, *, tm, blk):
    n_chunks = tm // GMLP_CHUNK
    col_w = D_MODEL // n_chunks
    shift = blk.bit_length() - 1
    row = lax.broadcasted_iota(jnp.int32, (GMLP_CHUNK, GMLP_CHUNK), 0)
    col = lax.broadcasted_iota(jnp.int32, (GMLP_CHUNK, GMLP_CHUNK), 1)
    causal = jnp.logical_and((row >> shift) == (col >> shift), col <= row)
    w = [jnp.where(causal, ws_ref[g], 0.0).astype(BF16) for g in range(GMLP_GROUPS)]
    m_parts = []
    for c in range(n_chunks):
        cols = slice(c * col_w, (c + 1) * col_w)
        h1_ref[:, cols] = h_ref[:, cols] + _dot(o_ref[...], wo_ref[:, cols])
        rows = pl.ds(c * GMLP_CHUNK, GMLP_CHUNK)
        vg = jax.nn.gelu(vg_ref[rows, :].astype(F32))
        mu = jnp.mean(vg, axis=-1, keepdims=True)
        vc = vg - mu
        vn = vc * lax.rsqrt(jnp.mean(vc * vc, axis=-1, keepdims=True) + EPS) * gln_ref[...] + bln_ref[...]
        if vn_ref is not None:
            vn_ref[rows, :] = vn
        vn16 = vn.astype(BF16)
        mixed = jnp.concatenate(
            [_dot(w[g], vn16[:, g * GMLP_GROUP_DIM:(g + 1) * GMLP_GROUP_DIM]) for g in range(GMLP_GROUPS)],
            axis=1) + bias_ref[...]
        mm = jax.nn.gelu(u_ref[rows, :].astype(F32)) * mixed
        m_parts.append(_rms(mm, gout_ref[...]).astype(BF16))
    h1_ref[...] += _dot(jnp.concatenate(m_parts, axis=0), wm_ref[...])


def _mixout_kernel(hp_ref, hs_ref, op_ref, os_ref, up_ref, vgp_ref, us_ref, vgs_ref, gln_ref, bln_ref,
                   wsp_ref, biasp_ref, wss_ref, biass_ref, gout_ref, wo_ref, wm_ref, h1_ref, vn_ref,
                   *, tm, n_p, blk_p, blk_s):
    shared = (gln_ref, bln_ref)
    tail = (gout_ref, wo_ref, wm_ref, h1_ref)
    i = pl.program_id(0)

    @pl.when(i < n_p)
    def _():
        _mixout_tile(hp_ref, op_ref, up_ref, vgp_ref, *shared, wsp_ref, biasp_ref, *tail, None, tm=tm, blk=blk_p)

    @pl.when(i >= n_p)
    def _():
        _mixout_tile(hs_ref, os_ref, us_ref, vgs_ref, *shared, wss_ref, biass_ref, *tail, vn_ref, tm=tm, blk=blk_s)


def _gmlp_operands(w_s, b_s, blk):
    reps = GMLP_CHUNK // blk
    sel = jnp.asarray(np.tile(np.eye(blk, dtype=np.float32), (reps, 1)))
    w_mix = jnp.einsum("ra,gab,cb->grc", sel, w_s[:, :blk, :blk], sel, precision=lax.Precision.HIGHEST)
    bias = jnp.tile(jnp.repeat(b_s[:, :blk].T, GMLP_GROUP_DIM, axis=1), (reps, 1))
    return w_mix, bias


def _mixout(hp, hs, op, os_, zp, zs, g_ln, b_ln, w_s, b_s, g_out, w_out, *, tm, len_p, len_s):
    n_p, n_s = hp.shape[0] // tm, hs.shape[0] // tm
    blk_p, blk_s = min(len_p, GMLP_CHUNK), min(len_s, GMLP_CHUNK)
    pidx = lambda i: jnp.minimum(i, n_p - 1)
    sidx = lambda i: jnp.maximum(i - n_p, 0)
    once = pl.Buffered(1)
    vec = pl.BlockSpec((1, GMLP_WIDTH), lambda i: (0, 0))
    mixw = pl.BlockSpec((GMLP_GROUPS, GMLP_CHUNK, GMLP_CHUNK), lambda i: (0, 0, 0))
    biasb = pl.BlockSpec((GMLP_CHUNK, GMLP_WIDTH), lambda i: (0, 0))
    return pl.pallas_call(
        functools.partial(_mixout_kernel, tm=tm, n_p=n_p, blk_p=blk_p, blk_s=blk_s),
        grid=(n_p + n_s,),
        in_specs=[
            pl.BlockSpec((tm, D_MODEL), lambda i: (pidx(i), 0)),
            pl.BlockSpec((tm, D_MODEL), lambda i: (sidx(i), 0)),
            pl.BlockSpec((tm, GLA_WIDTH), lambda i: (pidx(i), 0)),
            pl.BlockSpec((tm, GLA_WIDTH), lambda i: (sidx(i), 0)),
            pl.BlockSpec((tm, GMLP_WIDTH), lambda i: (pidx(i), Z_U_OFF // GMLP_WIDTH)),
            pl.BlockSpec((tm, GMLP_WIDTH), lambda i: (pidx(i), Z_VG_OFF // GMLP_WIDTH)),
            pl.BlockSpec((tm, GMLP_WIDTH), lambda i: (sidx(i), Z_U_OFF // GMLP_WIDTH)),
            pl.BlockSpec((tm, GMLP_WIDTH), lambda i: (sidx(i), Z_VG_OFF // GMLP_WIDTH)),
            vec, vec,
            mixw, biasb, mixw, biasb,
            vec,
            pl.BlockSpec((GLA_WIDTH, D_MODEL), lambda i: (0, 0), pipeline_mode=once),
            pl.BlockSpec((GMLP_WIDTH, D_MODEL), lambda i: (1, 0), pipeline_mode=once),
        ],
        out_specs=[
            pl.BlockSpec((tm, D_MODEL), lambda i: (i, 0)),
            pl.BlockSpec((tm, GMLP_WIDTH), lambda i: (sidx(i), 0)),
        ],
        out_shape=[
            jax.ShapeDtypeStruct((hp.shape[0] + hs.shape[0], D_MODEL), F32),
            jax.ShapeDtypeStruct((hs.shape[0], GMLP_WIDTH), F32),
        ],
        compiler_params=_params("arbitrary"),
        name="mixout",
    )(hp, hs, op, os_, zp, zp, zs, zs, g_ln, b_ln, *_gmlp_operands(w_s, b_s, blk_p), *_gmlp_operands(w_s, b_s, blk_s),
      g_out, w_out, w_out)


def _ffn_kernel(h_ref, g_ref, wg_ref, wu_ref, wo_ref, o_ref, hb_scr, r_scr):
    j = pl.program_id(1)

    def tile(hb, r):
        gate = r * _dot(hb, wg_ref[...])
        up = r * _dot(hb, wu_ref[...])
        return _dot((jax.nn.silu(gate) * up).astype(BF16), wo_ref[...])

    @pl.when(j == 0)
    def _():
        h = h_ref[...]
        hb = (h * g_ref[...]).astype(BF16)
        hb_scr[...] = hb
        r = lax.rsqrt(jnp.mean(h * h, axis=-1, keepdims=True) + EPS)
        r_scr[...] = r
        o_ref[...] = h + tile(hb, r)

    @pl.when(j > 0)
    def _():
        o_ref[...] += tile(hb_scr[...], r_scr[...])


def _ffn(h1, g_ffn, w_in, w_out, tm, tf):
    m = h1.shape[0]
    nf = D_FF // tf
    return pl.pallas_call(
        _ffn_kernel,
        grid=(m // tm, nf),
        in_specs=[
            pl.BlockSpec((tm, D_MODEL), lambda i, j: (i, 0)),
            pl.BlockSpec((1, D_MODEL), lambda i, j: (0, 0)),
            pl.BlockSpec((D_MODEL, tf), lambda i, j: (0, j)),
            pl.BlockSpec((D_MODEL, tf), lambda i, j: (0, nf + j)),
            pl.BlockSpec((tf, D_MODEL), lambda i, j: (j, 0)),
        ],
        out_specs=pl.BlockSpec((tm, D_MODEL), lambda i, j: (i, 0)),
        out_shape=jax.ShapeDtypeStruct((m, D_MODEL), F32),
        scratch_shapes=[pltpu.VMEM((tm, D_MODEL), BF16), pltpu.VMEM((tm, 1), F32)],
        compiler_params=_params("parallel", "arbitrary"),
        name="ffn",
    )(h1, g_ffn, w_in, w_in, w_out)


def _ple_tile(h_ref, p_ref, wp_ref, gp_ref, gg_ref, wg_ref, gf_ref, y_ref):
    h = h_ref[...]
    r = lax.rsqrt(jnp.mean(h * h, axis=-1, keepdims=True) + EPS)
    hb = (h * gg_ref[...]).astype(BF16)
    pe = _rms(_dot(p_ref[...].astype(BF16), wp_ref[...]), gp_ref[...])
    ssq = jnp.zeros((h.shape[0], 1), F32)
    for c in range(D_MODEL // PLE_COL_CHUNK):
        cols = slice(c * PLE_COL_CHUNK, (c + 1) * PLE_COL_CHUNK)
        t = h[:, cols] + pe[:, cols] * jax.nn.sigmoid(r * _dot(hb, wg_ref[:, cols]))
        ssq = ssq + jnp.sum(t * t, axis=-1, keepdims=True)
        y_ref[:, cols] = t
    y_ref[...] = y_ref[...] * lax.rsqrt(ssq / D_MODEL + EPS) * gf_ref[...]


def _ple_kernel(h_ref, pp_ref, ps_ref, wp_ref, gp_ref, gg_ref, wg_ref, gf_ref, yp_ref, ys_ref, *, n_p):
    params = (wp_ref, gp_ref, gg_ref, wg_ref, gf_ref)
    i = pl.program_id(0)

    @pl.when(i < n_p)
    def _():
        _ple_tile(h_ref, pp_ref, *params, yp_ref)

    @pl.when(i >= n_p)
    def _():
        _ple_tile(h_ref, ps_ref, *params, ys_ref)


def _ple(h, pp, ps, w_ple, g_ple, g_gate, w_gate, g_final, tm):
    n_p, n_s = pp.shape[0] // tm, ps.shape[0] // tm
    pidx = lambda i: (jnp.minimum(i, n_p - 1), 0)
    sidx = lambda i: (jnp.maximum(i - n_p, 0), 0)
    vec = pl.BlockSpec((1, D_MODEL), lambda i: (0, 0))
    return pl.pallas_call(
        functools.partial(_ple_kernel, n_p=n_p),
        grid=(n_p + n_s,),
        in_specs=[
            pl.BlockSpec((tm, D_MODEL), lambda i: (i, 0)),
            pl.BlockSpec((tm, PLE_DIM), pidx),
            pl.BlockSpec((tm, PLE_DIM), sidx),
            pl.BlockSpec((PLE_DIM, D_MODEL), lambda i: (0, 0)),
            vec, vec,
            pl.BlockSpec((D_MODEL, D_MODEL), lambda i: (0, 0), pipeline_mode=pl.Buffered(1)),
            vec,
        ],
        out_specs=[pl.BlockSpec((tm, D_MODEL), pidx), pl.BlockSpec((tm, D_MODEL), sidx)],
        out_shape=[jax.ShapeDtypeStruct((pp.shape[0], D_MODEL), F32),
                   jax.ShapeDtypeStruct((ps.shape[0], D_MODEL), F32)],
        compiler_params=_params("arbitrary"),
        name="ple",
    )(h, pp, ps, w_ple, g_ple, g_gate, w_gate, g_final)


def kernel(x_prompt, x_sample, state_gla, p_prompt, p_sample, g_mix, w_in, w_a2, b_a, g_gla_norm, g_gmlp_ln,
           b_gmlp_ln, w_s, b_s, g_gmlp_out, w_out, g_ffn, w_ffn_in, w_ffn_out, w_ple, g_ple, g_ple_gate,
           w_ple_gate, g_final):
    batch, seq, _ = x_prompt.shape
    dec_batch, dec_seq, _ = x_sample.shape
    depth = w_in.shape[0]
    assert depth == 1, "one layer: the prompt / sample passes below are not chained over depth"
    row = lambda a: a.reshape(1, -1)
    wts = dict(
        g_mix=row(g_mix[0]),
        wa2_pad=jnp.pad(w_a2[0], ((0, V7X_LANES - GLA_GATE_RANK), (0, 0))),
        b_a=row(b_a[0]),
        g_gla_norm=row(g_gla_norm[0]),
        g_gmlp_ln=row(g_gmlp_ln[0]),
        b_gmlp_ln=row(b_gmlp_ln[0]),
        w_s=w_s[0],
        b_s=b_s[0],
        g_gmlp_out=row(g_gmlp_out[0]),
        g_ffn=row(g_ffn[0]),
        w_ple=w_ple[0].astype(BF16),
        g_ple=row(g_ple[0]),
        g_ple_gate=row(g_ple_gate[0]),
        g_final=row(g_final),
    )
    xp = x_prompt.reshape(batch * seq, D_MODEL)
    xs = x_sample.reshape(dec_batch * dec_seq, D_MODEL)
    wt_in = jnp.swapaxes(w_in, 1, 2)[0]
    zs, alow_s, wt16, wl16 = _inproj(xs, wts["g_mix"], wt_in, wt_in, tm=xs.shape[0], tn=INPROJ_COLS, emit=True)
    zp, alow_p, wts["w_out"], wts["w_ple_gate"] = _inproj(
        xp, wts["g_mix"], wt16, wl16, tm=INPROJ_ROWS, tn=INPROJ_COLS, emit=False, cast=(w_out[0], w_ple_gate[0]))
    gla = functools.partial(_gla, wa2_pad=wts["wa2_pad"], b_a=wts["b_a"], g_norm=wts["g_gla_norm"])
    op, sp, wts["w_ffn_in"], wts["w_ffn_out"] = gla(zp, alow_p, s0=None, n_seq=batch, seq_len=seq,
                                                     cast=(w_ffn_in[0], w_ffn_out[0]))
    os_, ss = gla(zs, alow_s, s0=state_gla[0], n_seq=dec_batch, seq_len=dec_seq)
    h1, vs = _mixout(xp, xs, op, os_, zp, zs, wts["g_gmlp_ln"], wts["b_gmlp_ln"], wts["w_s"], wts["b_s"],
                     wts["g_gmlp_out"], wts["w_out"], tm=MIX_ROWS, len_p=seq, len_s=dec_seq)
    h2 = _ffn(h1, wts["g_ffn"], wts["w_ffn_in"], wts["w_ffn_out"], tm=FFN_ROWS, tf=FFN_COLS)
    yp, ys = _ple(h2, p_prompt[0].reshape(batch * seq, PLE_DIM), p_sample[0].reshape(dec_batch * dec_seq, PLE_DIM),
                  wts["w_ple"], wts["g_ple"], wts["g_ple_gate"], wts["w_ple_gate"], wts["g_final"], tm=PLE_ROWS)
    return (yp.reshape(batch, seq, D_MODEL), ys.reshape(dec_batch, dec_seq, D_MODEL), sp[None], ss[None],
            vs.reshape(1, dec_batch, dec_seq, GMLP_WIDTH))
```

```python
import functools

import numpy as np

import jax
import jax.numpy as jnp
from jax import lax
from jax.experimental import pallas as pl
from jax.experimental.pallas import tpu as pltpu

F32 = jnp.float32
BF16 = jnp.bfloat16

D_MODEL = 2048
GLA_WIDTH = 1024
GMLP_WIDTH = 1024
GLA_HEADS = 4
GLA_DV = 256
GLA_DK = 128
GLA_KEY_WIDTH = 512
GLA_GATE_RANK = 16
GLA_TAU = 16.0
GLA_CHUNK = 64
GMLP_GROUPS = 8
GMLP_GROUP_DIM = 128
GMLP_CHUNK = 128
D_FF = 5632
PLE_DIM = 256
EPS = 1e-6
IN_OFFSETS = (0, 512, 1024, 2048, 3072, 3088, 4112, 5136)
Z_WIDTH = 5120
Z_GLA_WIDTH = 3072
Z_U_OFF = 3072
Z_VG_OFF = 4096

V7X_LANES = 128
V7X_BF16_ROWS = 16
V7X_VMEM_BYTES = 64 * 1024 * 1024
VMEM_LIMIT_BYTES = V7X_VMEM_BYTES - 4 * 1024 * 1024
INPROJ_ROWS, INPROJ_COLS = 1024, 1024
MIX_ROWS = 512
FFN_ROWS, FFN_COLS = 1024, 512
PLE_ROWS, PLE_COL_CHUNK = 512, 1024
GLA_SEQ_ROWS = 4 * GLA_CHUNK
N_CAST_BLOCKS = 32


def _params(*sem):
    return pltpu.CompilerParams(dimension_semantics=sem, vmem_limit_bytes=VMEM_LIMIT_BYTES)


def _rms(x, g):
    return x * lax.rsqrt(jnp.mean(x * x, axis=-1, keepdims=True) + EPS) * g


def _dot(a, b):
    return jnp.dot(a, b, preferred_element_type=F32)


def _dot_tn(a, b):
    return lax.dot_general(a, b, (((0,), (0,)), ((), ())), preferred_element_type=F32)


def _dot_nt(a, b):
    return lax.dot_general(a, b, (((1,), (1,)), ((), ())), preferred_element_type=F32)


def _cast_specs(cast, n_blocks, step_of):
    specs = []
    for w in cast:
        rows = w.shape[0] // n_blocks
        assert rows * n_blocks == w.shape[0] and rows % V7X_BF16_ROWS == 0
        specs.append(pl.BlockSpec((rows, w.shape[1]),
                                  lambda *ids: (jnp.minimum(step_of(*ids), n_blocks - 1), 0)))
    return specs


def _cast_blocks(cast_in, cast_out):
    for src_ref, dst_ref in zip(cast_in, cast_out):
        dst_ref[...] = src_ref[...].astype(BF16)


def _inproj_kernel(*refs, emit, n_cast):
    x_ref, g_ref, w_ref, wl_ref = refs[:4]
    cast_in = refs[4:4 + n_cast]
    n_out = 4 if emit else 2
    z_ref, alow_ref = refs[4 + n_cast:6 + n_cast]
    emit_refs = refs[6 + n_cast:4 + n_cast + n_out]
    cast_out = refs[4 + n_cast + n_out:4 + 2 * n_cast + n_out]
    xb_scr, r_scr = refs[4 + 2 * n_cast + n_out:]
    j = pl.program_id(1)

    def weight(ref, emit_ref):
        if not emit:
            return ref[...]
        w16 = ref[...].astype(BF16)
        emit_ref[...] = w16
        return w16

    def z_tile(xb, r):
        return r * _dot_nt(xb, weight(w_ref, emit_refs[0] if emit else None))

    @pl.when(j == 0)
    def _():
        _cast_blocks(cast_in, cast_out)
        x = x_ref[...]
        xb = (x * g_ref[...]).astype(BF16)
        xb_scr[...] = xb
        r = lax.rsqrt(jnp.mean(x * x, axis=-1, keepdims=True) + EPS)
        r_scr[...] = r
        alow_ref[...] = r * _dot_nt(xb, weight(wl_ref, emit_refs[1] if emit else None))
        z_ref[...] = z_tile(xb, r).astype(BF16)

    @pl.when(j > 0)
    def _():
        _cast_blocks(cast_in, cast_out)
        z_ref[...] = z_tile(xb_scr[...], r_scr[...]).astype(BF16)


def _inproj(x, g_mix, w, wl, *, tm, tn, emit, cast=()):
    m = x.shape[0]
    n_a = Z_GLA_WIDTH // tn
    n_j = Z_WIDTH // tn
    if emit:
        def w_row(j):
            return pl.multiple_of(jnp.where(j < n_a, j * tn, IN_OFFSETS[5] + (j - n_a) * tn), V7X_BF16_ROWS)
        w_spec = pl.BlockSpec((pl.Element(tn), pl.Element(D_MODEL)), lambda i, j: (w_row(j), 0))
        wl_spec = pl.BlockSpec((V7X_LANES, D_MODEL), lambda i, j: (IN_OFFSETS[4] // V7X_LANES, 0))
        assert m == tm, "every weight tile must be visited exactly once"
    else:
        w_spec = pl.BlockSpec((tn, D_MODEL), lambda i, j: (j, 0))
        wl_spec = pl.BlockSpec((V7X_LANES, D_MODEL), lambda i, j: (0, 0))
    cast_specs = _cast_specs(cast, N_CAST_BLOCKS, lambda i, j: i * n_j + j)
    assert not cast or (m // tm) * n_j >= N_CAST_BLOCKS
    out_specs = [
        pl.BlockSpec((tm, tn), lambda i, j: (i, j)),
        pl.BlockSpec((tm, V7X_LANES), lambda i, j: (i, 0)),
    ]
    out_shape = [
        jax.ShapeDtypeStruct((m, Z_WIDTH), BF16),
        jax.ShapeDtypeStruct((m, V7X_LANES), F32),
    ]
    if emit:
        out_specs += [pl.BlockSpec((tn, D_MODEL), lambda i, j: (j, 0)),
                      pl.BlockSpec((V7X_LANES, D_MODEL), lambda i, j: (0, 0))]
        out_shape += [jax.ShapeDtypeStruct((Z_WIDTH, D_MODEL), BF16),
                      jax.ShapeDtypeStruct((V7X_LANES, D_MODEL), BF16)]
    return pl.pallas_call(
        functools.partial(_inproj_kernel, emit=emit, n_cast=len(cast)),
        grid=(m // tm, n_j),
        in_specs=[
            pl.BlockSpec((tm, D_MODEL), lambda i, j: (i, 0)),
            pl.BlockSpec((1, D_MODEL), lambda i, j: (0, 0)),
            w_spec,
            wl_spec,
        ] + cast_specs,
        out_specs=out_specs + cast_specs,
        out_shape=out_shape + [jax.ShapeDtypeStruct(w.shape, BF16) for w in cast],
        scratch_shapes=[pltpu.VMEM((tm, D_MODEL), BF16), pltpu.VMEM((tm, 1), F32)],
        compiler_params=_params("arbitrary" if cast else "parallel", "arbitrary"),
        name="inproj",
    )(x, g_mix, w, wl, *cast)


def _group_mask(R, C):
    t = np.arange(R)[:, None]
    s = np.arange(R)[None, :]
    return ((t // C == s // C) & (s <= t)).astype(np.float32)


def _gla_tile(q_ref, k_ref, v_ref, r_ref, alow_ref, tri_ref, mask_ref, wa2_ref, ba_ref, gn_ref,
              o_ref, s_out_ref, b_scr, e_scr, *, C, s_scr=None, s0_ref=None):
    sequential = s_scr is not None
    R = q_ref.shape[0]
    n = R // C
    mid = C // 2
    W = GLA_KEY_WIDTH

    x = _dot(alow_ref[...].astype(BF16), wa2_ref[...].astype(BF16)) + ba_ref[...]
    lb = (jnp.minimum(x, 0.0) - jnp.log(1.0 + jnp.exp(-jnp.abs(x)))) * (1.0 / GLA_TAU)

    tri = tri_ref[...]
    lb_hi = lb.astype(BF16)
    lb_lo = (lb - lb_hi.astype(F32)).astype(BF16)
    b = _dot(tri, lb_hi) + _dot(tri, lb_lo)
    b3 = b.reshape(n, C, W)
    b_mid = jnp.broadcast_to(b3[:, mid:mid + 1, :], (n, C, W)).reshape(R, W)
    b_last = jnp.broadcast_to(b3[:, C - 1:C, :], (n, C, W)).reshape(R, W)

    q = q_ref[...].astype(F32) * (GLA_DK ** -0.5)
    k = k_ref[...].astype(F32)
    q1 = (q * jnp.exp(b)).astype(BF16)
    q2 = (q * jnp.exp(b - b_mid)).astype(BF16)
    k2 = (k * jnp.exp(b_mid - b)).astype(BF16)
    k3 = k * jnp.exp(b_last - b)

    e_scr[...] = jnp.zeros((V7X_LANES, GLA_DK), F32)
    for h in range(GLA_HEADS):
        b_scr[h] = b[:, h * GLA_DK:(h + 1) * GLA_DK]
        e_scr[h * n:(h + 1) * n, :] = jnp.exp(b_scr[h, pl.ds(C - 1, n, stride=C), :])
    d_cols = e_scr[...].T

    causal = mask_ref[...] > 0.0
    if not sequential:
        grp = lax.broadcasted_iota(jnp.int32, (R, 1), 0) >> (C.bit_length() - 1)

    for h in range(GLA_HEADS):
        ks = slice(h * GLA_DK, (h + 1) * GLA_DK)
        vs = slice(h * GLA_DV, (h + 1) * GLA_DV)
        v = v_ref[:, vs]
        att = jnp.where(causal, _dot_nt(q2[:, ks], k2[:, ks]), 0.0).astype(BF16)
        o = _dot(att, v)
        q1h = q1[:, ks]
        k3h = k3[:, ks]
        if sequential:
            s = s_scr[h]
            o_parts = []
            for g in range(n):
                rows = slice(g * C, (g + 1) * C)
                o_parts.append(o[rows] + _dot(q1h[rows], s.astype(BF16)))
                s = d_cols[:, h * n + g:h * n + g + 1] * s + _dot_tn(k3h[rows].astype(BF16), v[rows])
            s_scr[h] = s
            s_out_ref[0, h] = s
            o = jnp.concatenate(o_parts, axis=0)
        else:
            for g in range(n):
                s = s0_ref[g, h]
                o = o + jnp.where(grp == g, _dot(q1h, s.astype(BF16)), 0.0)
                k3g = jnp.where(grp == g, k3h, 0.0).astype(BF16)
                s_out_ref[g, h] = d_cols[:, h * n + g:h * n + g + 1] * s + _dot_tn(k3g, v)
        o = _rms(o, gn_ref[:, vs]) * jax.nn.silu(r_ref[:, vs].astype(F32))
        o_ref[:, vs] = o.astype(BF16)


N_GLA_TILE_IN = 7


def _gla_kernel(*refs, C_seq, C_step, n_cast):
    n = N_GLA_TILE_IN
    seq_in, step_in = refs[:n], refs[n:2 * n]
    s0_ref, wa2_ref, ba_ref, gn_ref = refs[2 * n:2 * n + 4]
    cast_in = refs[2 * n + 4:2 * n + 4 + n_cast]
    outs = refs[2 * n + 4 + n_cast:]
    o_seq, s_seq, o_step, s_step = outs[:4]
    cast_out = outs[4:4 + n_cast]
    s_scr, b_seq, e_seq, b_step, e_step = outs[4 + n_cast:]
    shared = (wa2_ref, ba_ref, gn_ref)

    @pl.when(pl.program_id(1) == 0)
    def _():
        s_scr[...] = jnp.zeros((GLA_HEADS, GLA_DK, GLA_DV), F32)

    _cast_blocks(cast_in, cast_out)
    _gla_tile(*seq_in, *shared, o_seq, s_seq, b_seq, e_seq, C=C_seq, s_scr=s_scr)
    _gla_tile(*step_in, *shared, o_step, s_step, b_step, e_step, C=C_step, s0_ref=s0_ref)


def _gla(zp, alow_p, zs, alow_s, s0, wa2_pad, b_a, g_norm, *, n_seq_p, len_p, n_seq_s, len_s, cast=()):
    R_p = GLA_SEQ_ROWS
    nt = len_p // R_p
    n_steps = n_seq_p * nt
    R_s = zs.shape[0] // n_steps
    assert R_s * n_steps == zs.shape[0] and R_s % len_s == 0 and R_s % V7X_BF16_ROWS == 0
    assert not cast or n_steps >= N_CAST_BLOCKS
    step_of = lambda a, c: a * nt + c
    const = lambda a, c: (0, 0)

    def tile_specs(R):
        return [
            pl.BlockSpec((R, GLA_KEY_WIDTH), lambda a, c: (step_of(a, c), 0)),
            pl.BlockSpec((R, GLA_KEY_WIDTH), lambda a, c: (step_of(a, c), 1)),
            pl.BlockSpec((R, GLA_WIDTH), lambda a, c: (step_of(a, c), 1)),
            pl.BlockSpec((R, GLA_WIDTH), lambda a, c: (step_of(a, c), 2)),
            pl.BlockSpec((R, V7X_LANES), lambda a, c: (step_of(a, c), 0)),
            pl.BlockSpec((R, R), const),
            pl.BlockSpec((R, R), const),
        ]

    def tile_args(z, alow, R, C):
        mask = _group_mask(R, C)
        return [z, z, z, z, alow, jnp.asarray(mask, BF16), jnp.asarray(mask, F32)]

    s_block_p = (1, GLA_HEADS, GLA_DK, GLA_DV)
    s_block_s = (R_s // len_s, GLA_HEADS, GLA_DK, GLA_DV)
    state_p = pl.BlockSpec(s_block_p, lambda a, c: (a, 0, 0, 0))
    state_s = pl.BlockSpec(s_block_s, lambda a, c: (step_of(a, c), 0, 0, 0))
    cast_specs = _cast_specs(cast, N_CAST_BLOCKS, step_of)
    scratch = [pltpu.VMEM((GLA_HEADS, GLA_DK, GLA_DV), F32)]
    for R in (R_p, R_s):
        scratch += [pltpu.VMEM((GLA_HEADS, R, GLA_DK), F32), pltpu.VMEM((V7X_LANES, GLA_DK), F32)]
    return pl.pallas_call(
        functools.partial(_gla_kernel, C_seq=GLA_CHUNK, C_step=len_s, n_cast=len(cast)),
        grid=(n_seq_p, nt),
        in_specs=tile_specs(R_p) + tile_specs(R_s) + [
            state_s,
            pl.BlockSpec((V7X_LANES, GLA_KEY_WIDTH), const),
            pl.BlockSpec((1, GLA_KEY_WIDTH), const),
            pl.BlockSpec((1, GLA_WIDTH), const),
        ] + cast_specs,
        out_specs=[
            pl.BlockSpec((R_p, GLA_WIDTH), lambda a, c: (step_of(a, c), 0)),
            state_p,
            pl.BlockSpec((R_s, GLA_WIDTH), lambda a, c: (step_of(a, c), 0)),
            state_s,
        ] + cast_specs,
        out_shape=[
            jax.ShapeDtypeStruct((zp.shape[0], GLA_WIDTH), BF16),
            jax.ShapeDtypeStruct((n_seq_p, GLA_HEADS, GLA_DK, GLA_DV), F32),
            jax.ShapeDtypeStruct((zs.shape[0], GLA_WIDTH), BF16),
            jax.ShapeDtypeStruct((n_seq_s, GLA_HEADS, GLA_DK, GLA_DV), F32),
        ] + [jax.ShapeDtypeStruct(w.shape, BF16) for w in cast],
        scratch_shapes=scratch,
        compiler_params=_params("parallel", "arbitrary"),
        name="gla",
    )(*tile_args(zp, alow_p, R_p, GLA_CHUNK), *tile_args(zs, alow_s, R_s, len_s), s0, wa2_pad, b_a, g_norm, *cast)


def _mixout_tile(h_ref, o_ref, u_ref, vg_ref, gln_ref, bln_ref, ws_ref, bias_ref, gout_ref, wo_ref, wm_ref,
                 h1_ref, vn_ref, *, tm, blk):
    n_chunks = tm // GMLP_CHUNK
    col_w = D_MODEL // n_chunks
    shift = blk.bit_length() - 1
    row = lax.broadcasted_iota(jnp.int32, (GMLP_CHUNK, GMLP_CHUNK), 0)
    col = lax.broadcasted_iota(jnp.int32, (GMLP_CHUNK, GMLP_CHUNK), 1)
    causal = jnp.logical_and((row >> shift) == (col >> shift), col <= row)
    w = [jnp.where(causal, ws_ref[g], 0.0).astype(BF16) for g in range(GMLP_GROUPS)]
    m_parts = []
    for c in range(n_chunks):
        cols = slice(c * col_w, (c + 1) * col_w)
        h1_ref[:, cols] = h_ref[:, cols] + _dot(o_ref[...], wo_ref[:, cols])
        rows = pl.ds(c * GMLP_CHUNK, GMLP_CHUNK)
        vg = jax.nn.gelu(vg_ref[rows, :].astype(F32))
        mu = jnp.mean(vg, axis=-1, keepdims=True)
        vc = vg - mu
        vn = vc * lax.rsqrt(jnp.mean(vc * vc, axis=-1, keepdims=True) + EPS) * gln_ref[...] + bln_ref[...]
        if vn_ref is not None:
            vn_ref[rows, :] = vn
        vn16 = vn.astype(BF16)
        mixed = jnp.concatenate(
            [_dot(w[g], vn16[:, g * GMLP_GROUP_DIM:(g + 1) * GMLP_GROUP_DIM]) for g in range(GMLP_GROUPS)],
            axis=1) + bias_ref[...]
        mm = jax.nn.gelu(u_ref[rows, :].astype(F32)) * mixed
        m_parts.append(_rms(mm, gout_ref[...]).astype(BF16))
    h1_ref[...] += _dot(jnp.concatenate(m_parts, axis=0), wm_ref[...])


def _mixout_kernel(hp_ref, hs_ref, op_ref, os_ref, up_ref, vgp_ref, us_ref, vgs_ref, gln_ref, bln_ref,
                   wsp_ref, biasp_ref, wss_ref, biass_ref, gout_ref, wo_ref, wm_ref, h1_ref, vn_ref,
                   *, tm, n_p, blk_p, blk_s):
    shared = (gln_ref, bln_ref)
    tail = (gout_ref, wo_ref, wm_ref, h1_ref)
    i = pl.program_id(0)

    @pl.when(i < n_p)
    def _():
        _mixout_tile(hp_ref, op_ref, up_ref, vgp_ref, *shared, wsp_ref, biasp_ref, *tail, None, tm=tm, blk=blk_p)

    @pl.when(i >= n_p)
    def _():
        _mixout_tile(hs_ref, os_ref, us_ref, vgs_ref, *shared, wss_ref, biass_ref, *tail, vn_ref, tm=tm, blk=blk_s)


def _gmlp_operands(w_s, b_s, blk):
    reps = GMLP_CHUNK // blk
    sel = jnp.asarray(np.tile(np.eye(blk, dtype=np.float32), (reps, 1)))
    w_mix = jnp.einsum("ra,gab,cb->grc", sel, w_s[:, :blk, :blk], sel, precision=lax.Precision.HIGHEST)
    bias = jnp.tile(jnp.repeat(b_s[:, :blk].T, GMLP_GROUP_DIM, axis=1), (reps, 1))
    return w_mix, bias


def _mixout(hp, hs, op, os_, zp, zs, g_ln, b_ln, w_s, b_s, g_out, w_out, *, tm, len_p, len_s):
    n_p, n_s = hp.shape[0] // tm, hs.shape[0] // tm
    blk_p, blk_s = min(len_p, GMLP_CHUNK), min(len_s, GMLP_CHUNK)
    pidx = lambda i: jnp.minimum(i, n_p - 1)
    sidx = lambda i: jnp.maximum(i - n_p, 0)
    once = pl.Buffered(1)
    vec = pl.BlockSpec((1, GMLP_WIDTH), lambda i: (0, 0))
    mixw = pl.BlockSpec((GMLP_GROUPS, GMLP_CHUNK, GMLP_CHUNK), lambda i: (0, 0, 0))
    biasb = pl.BlockSpec((GMLP_CHUNK, GMLP_WIDTH), lambda i: (0, 0))
    return pl.pallas_call(
        functools.partial(_mixout_kernel, tm=tm, n_p=n_p, blk_p=blk_p, blk_s=blk_s),
        grid=(n_p + n_s,),
        in_specs=[
            pl.BlockSpec((tm, D_MODEL), lambda i: (pidx(i), 0)),
            pl.BlockSpec((tm, D_MODEL), lambda i: (sidx(i), 0)),
            pl.BlockSpec((tm, GLA_WIDTH), lambda i: (pidx(i), 0)),
            pl.BlockSpec((tm, GLA_WIDTH), lambda i: (sidx(i), 0)),
            pl.BlockSpec((tm, GMLP_WIDTH), lambda i: (pidx(i), Z_U_OFF // GMLP_WIDTH)),
            pl.BlockSpec((tm, GMLP_WIDTH), lambda i: (pidx(i), Z_VG_OFF // GMLP_WIDTH)),
            pl.BlockSpec((tm, GMLP_WIDTH), lambda i: (sidx(i), Z_U_OFF // GMLP_WIDTH)),
            pl.BlockSpec((tm, GMLP_WIDTH), lambda i: (sidx(i), Z_VG_OFF // GMLP_WIDTH)),
            vec, vec,
            mixw, biasb, mixw, biasb,
            vec,
            pl.BlockSpec((GLA_WIDTH, D_MODEL), lambda i: (0, 0), pipeline_mode=once),
            pl.BlockSpec((GMLP_WIDTH, D_MODEL), lambda i: (1, 0), pipeline_mode=once),
        ],
        out_specs=[
            pl.BlockSpec((tm, D_MODEL), lambda i: (i, 0)),
            pl.BlockSpec((tm, GMLP_WIDTH), lambda i: (sidx(i), 0)),
        ],
        out_shape=[
            jax.ShapeDtypeStruct((hp.shape[0] + hs.shape[0], D_MODEL), F32),
            jax.ShapeDtypeStruct((hs.shape[0], GMLP_WIDTH), F32),
        ],
        compiler_params=_params("arbitrary"),
        name="mixout",
    )(hp, hs, op, os_, zp, zp, zs, zs, g_ln, b_ln, *_gmlp_operands(w_s, b_s, blk_p), *_gmlp_operands(w_s, b_s, blk_s),
      g_out, w_out, w_out)


def _ffn_kernel(h_ref, g_ref, wg_ref, wu_ref, wo_ref, o_ref, hb_scr, r_scr):
    j = pl.program_id(1)

    def tile(hb, r):
        gate = r * _dot(hb, wg_ref[...])
        up = r * _dot(hb, wu_ref[...])
        return _dot((jax.nn.silu(gate) * up).astype(BF16), wo_ref[...])

    @pl.when(j == 0)
    def _():
        h = h_ref[...]
        hb = (h * g_ref[...]).astype(BF16)
        hb_scr[...] = hb
        r = lax.rsqrt(jnp.mean(h * h, axis=-1, keepdims=True) + EPS)
        r_scr[...] = r
        o_ref[...] = h + tile(hb, r)

    @pl.when(j > 0)
    def _():
        o_ref[...] += tile(hb_scr[...], r_scr[...])


def _ffn(h1, g_ffn, w_in, w_out, tm, tf):
    m = h1.shape[0]
    nf = D_FF // tf
    return pl.pallas_call(
        _ffn_kernel,
        grid=(m // tm, nf),
        in_specs=[
            pl.BlockSpec((tm, D_MODEL), lambda i, j: (i, 0)),
            pl.BlockSpec((1, D_MODEL), lambda i, j: (0, 0)),
            pl.BlockSpec((D_MODEL, tf), lambda i, j: (0, j)),
            pl.BlockSpec((D_MODEL, tf), lambda i, j: (0, nf + j)),
            pl.BlockSpec((tf, D_MODEL), lambda i, j: (j, 0)),
        ],
        out_specs=pl.BlockSpec((tm, D_MODEL), lambda i, j: (i, 0)),
        out_shape=jax.ShapeDtypeStruct((m, D_MODEL), F32),
        scratch_shapes=[pltpu.VMEM((tm, D_MODEL), BF16), pltpu.VMEM((tm, 1), F32)],
        compiler_params=_params("parallel", "arbitrary"),
        name="ffn",
    )(h1, g_ffn, w_in, w_in, w_out)


def _ple_tile(h_ref, p_ref, wp_ref, gp_ref, gg_ref, wg_ref, gf_ref, y_ref):
    h = h_ref[...]
    r = lax.rsqrt(jnp.mean(h * h, axis=-1, keepdims=True) + EPS)
    hb = (h * gg_ref[...]).astype(BF16)
    pe = _rms(_dot(p_ref[...].astype(BF16), wp_ref[...]), gp_ref[...])
    ssq = jnp.zeros((h.shape[0], 1), F32)
    for c in range(D_MODEL // PLE_COL_CHUNK):
        cols = slice(c * PLE_COL_CHUNK, (c + 1) * PLE_COL_CHUNK)
        t = h[:, cols] + pe[:, cols] * jax.nn.sigmoid(r * _dot(hb, wg_ref[:, cols]))
        ssq = ssq + jnp.sum(t * t, axis=-1, keepdims=True)
        y_ref[:, cols] = t
    y_ref[...] = y_ref[...] * lax.rsqrt(ssq / D_MODEL + EPS) * gf_ref[...]


def _ple_kernel(h_ref, pp_ref, ps_ref, wp_ref, gp_ref, gg_ref, wg_ref, gf_ref, yp_ref, ys_ref, *, n_p):
    params = (wp_ref, gp_ref, gg_ref, wg_ref, gf_ref)
    i = pl.program_id(0)

    @pl.when(i < n_p)
    def _():
        _ple_tile(h_ref, pp_ref, *params, yp_ref)

    @pl.when(i >= n_p)
    def _():
        _ple_tile(h_ref, ps_ref, *params, ys_ref)


def _ple(h, pp, ps, w_ple, g_ple, g_gate, w_gate, g_final, tm):
    n_p, n_s = pp.shape[0] // tm, ps.shape[0] // tm
    pidx = lambda i: (jnp.minimum(i, n_p - 1), 0)
    sidx = lambda i: (jnp.maximum(i - n_p, 0), 0)
    vec = pl.BlockSpec((1, D_MODEL), lambda i: (0, 0))
    return pl.pallas_call(
        functools.partial(_ple_kernel, n_p=n_p),
        grid=(n_p + n_s,),
        in_specs=[
            pl.BlockSpec((tm, D_MODEL), lambda i: (i, 0)),
            pl.BlockSpec((tm, PLE_DIM), pidx),
            pl.BlockSpec((tm, PLE_DIM), sidx),
            pl.BlockSpec((PLE_DIM, D_MODEL), lambda i: (0, 0)),
            vec, vec,
            pl.BlockSpec((D_MODEL, D_MODEL), lambda i: (0, 0), pipeline_mode=pl.Buffered(1)),
            vec,
        ],
        out_specs=[pl.BlockSpec((tm, D_MODEL), pidx), pl.BlockSpec((tm, D_MODEL), sidx)],
        out_shape=[jax.ShapeDtypeStruct((pp.shape[0], D_MODEL), F32),
                   jax.ShapeDtypeStruct((ps.shape[0], D_MODEL), F32)],
        compiler_params=_params("arbitrary"),
        name="ple",
    )(h, pp, ps, w_ple, g_ple, g_gate, w_gate, g_final)


def kernel(x_prompt, x_sample, state_gla, p_prompt, p_sample, g_mix, w_in, w_a2, b_a, g_gla_norm, g_gmlp_ln,
           b_gmlp_ln, w_s, b_s, g_gmlp_out, w_out, g_ffn, w_ffn_in, w_ffn_out, w_ple, g_ple, g_ple_gate,
           w_ple_gate, g_final):
    batch, seq, _ = x_prompt.shape
    dec_batch, dec_seq, _ = x_sample.shape
    depth = w_in.shape[0]
    assert depth == 1, "one layer: the prompt / sample passes below are not chained over depth"
    row = lambda a: a.reshape(1, -1)
    wts = dict(
        g_mix=row(g_mix[0]),
        wa2_pad=jnp.pad(w_a2[0], ((0, V7X_LANES - GLA_GATE_RANK), (0, 0))),
        b_a=row(b_a[0]),
        g_gla_norm=row(g_gla_norm[0]),
        g_gmlp_ln=row(g_gmlp_ln[0]),
        b_gmlp_ln=row(b_gmlp_ln[0]),
        w_s=w_s[0],
        b_s=b_s[0],
        g_gmlp_out=row(g_gmlp_out[0]),
        g_ffn=row(g_ffn[0]),
        w_ple=w_ple[0].astype(BF16),
        g_ple=row(g_ple[0]),
        g_ple_gate=row(g_ple_gate[0]),
        g_final=row(g_final),
    )
    xp = x_prompt.reshape(batch * seq, D_MODEL)
    xs = x_sample.reshape(dec_batch * dec_seq, D_MODEL)
    wt_in = jnp.swapaxes(w_in, 1, 2)[0]
    zs, alow_s, wt16, wl16 = _inproj(xs, wts["g_mix"], wt_in, wt_in, tm=xs.shape[0], tn=INPROJ_COLS, emit=True)
    zp, alow_p, wts["w_out"], wts["w_ple_gate"] = _inproj(
        xp, wts["g_mix"], wt16, wl16, tm=INPROJ_ROWS, tn=INPROJ_COLS, emit=False, cast=(w_out[0], w_ple_gate[0]))
    op, sp, os_, ss, wts["w_ffn_in"], wts["w_ffn_out"] = _gla(
        zp, alow_p, zs, alow_s, state_gla[0], wts["wa2_pad"], wts["b_a"], wts["g_gla_norm"],
        n_seq_p=batch, len_p=seq, n_seq_s=dec_batch, len_s=dec_seq, cast=(w_ffn_in[0], w_ffn_out[0]))
    h1, vs = _mixout(xp, xs, op, os_, zp, zs, wts["g_gmlp_ln"], wts["b_gmlp_ln"], wts["w_s"], wts["b_s"],
                     wts["g_gmlp_out"], wts["w_out"], tm=MIX_ROWS, len_p=seq, len_s=dec_seq)
    h2 = _ffn(h1, wts["g_ffn"], wts["w_ffn_in"], wts["w_ffn_out"], tm=FFN_ROWS, tf=FFN_COLS)
    yp, ys = _ple(h2, p_prompt[0].reshape(batch * seq, PLE_DIM), p_sample[0].reshape(dec_batch * dec_seq, PLE_DIM),
                  wts["w_ple"], wts["g_ple"], wts["g_ple_gate"], wts["w_ple_gate"], wts["g_final"], tm=PLE_ROWS)
    return (yp.reshape(batch, seq, D_MODEL), ys.reshape(dec_batch, dec_seq, D_MODEL), sp[None], ss[None],
            vs.reshape(1, dec_batch, dec_seq, GMLP_WIDTH))
```

```python
import functools

import numpy as np

import jax
import jax.numpy as jnp
from jax import lax
from jax.experimental import pallas as pl
from jax.experimental.pallas import tpu as pltpu

F32 = jnp.float32
BF16 = jnp.bfloat16

D_MODEL = 2048
GLA_WIDTH = 1024
GMLP_WIDTH = 1024
GLA_HEADS = 4
GLA_DV = 256
GLA_DK = 128
GLA_KEY_WIDTH = 512
GLA_GATE_RANK = 16
GLA_TAU = 16.0
GLA_CHUNK = 64
GMLP_GROUPS = 8
GMLP_GROUP_DIM = 128
GMLP_CHUNK = 128
D_FF = 5632
PLE_DIM = 256
EPS = 1e-6
IN_OFFSETS = (0, 512, 1024, 2048, 3072, 3088, 4112, 5136)
Z_WIDTH = 5120
Z_GLA_WIDTH = 3072
Z_U_OFF = 3072
Z_VG_OFF = 4096

V7X_LANES = 128
V7X_BF16_ROWS = 16
V7X_VMEM_BYTES = 64 * 1024 * 1024
VMEM_LIMIT_BYTES = V7X_VMEM_BYTES - 4 * 1024 * 1024
INPROJ_ROWS, INPROJ_COLS = 1024, 1024
MIX_ROWS = 512
FFN_ROWS, FFN_COLS = 1024, 512
PLE_ROWS, PLE_COL_CHUNK = 512, 1024
GLA_SEQ_ROWS = 4 * GLA_CHUNK
N_CAST_BLOCKS = 32


def _params(*sem):
    return pltpu.CompilerParams(dimension_semantics=sem, vmem_limit_bytes=VMEM_LIMIT_BYTES)


def _rms(x, g):
    return x * lax.rsqrt(jnp.mean(x * x, axis=-1, keepdims=True) + EPS) * g


def _dot(a, b):
    return jnp.dot(a, b, preferred_element_type=F32)


def _dot_tn(a, b):
    return lax.dot_general(a, b, (((0,), (0,)), ((), ())), preferred_element_type=F32)


def _dot_nt(a, b):
    return lax.dot_general(a, b, (((1,), (1,)), ((), ())), preferred_element_type=F32)


def _cast_specs(cast, n_blocks, step_of):
    specs = []
    for w in cast:
        rows = w.shape[0] // n_blocks
        assert rows * n_blocks == w.shape[0] and rows % V7X_BF16_ROWS == 0
        specs.append(pl.BlockSpec((rows, w.shape[1]),
                                  lambda *ids: (jnp.minimum(step_of(*ids), n_blocks - 1), 0)))
    return specs


def _cast_blocks(cast_in, cast_out):
    for src_ref, dst_ref in zip(cast_in, cast_out):
        dst_ref[...] = src_ref[...].astype(BF16)


def _inproj_kernel(*refs, emit, n_cast):
    x_ref, g_ref, w_ref, wl_ref = refs[:4]
    cast_in = refs[4:4 + n_cast]
    n_out = 4 if emit else 2
    z_ref, alow_ref = refs[4 + n_cast:6 + n_cast]
    emit_refs = refs[6 + n_cast:4 + n_cast + n_out]
    cast_out = refs[4 + n_cast + n_out:4 + 2 * n_cast + n_out]
    xb_scr, r_scr = refs[4 + 2 * n_cast + n_out:]
    j = pl.program_id(1)

    def weight(ref, emit_ref):
        if not emit:
            return ref[...]
        w16 = ref[...].astype(BF16)
        emit_ref[...] = w16
        return w16

    def z_tile(xb, r):
        return r * _dot_nt(xb, weight(w_ref, emit_refs[0] if emit else None))

    @pl.when(j == 0)
    def _():
        _cast_blocks(cast_in, cast_out)
        x = x_ref[...]
        xb = (x * g_ref[...]).astype(BF16)
        xb_scr[...] = xb
        r = lax.rsqrt(jnp.mean(x * x, axis=-1, keepdims=True) + EPS)
        r_scr[...] = r
        alow_ref[...] = r * _dot_nt(xb, weight(wl_ref, emit_refs[1] if emit else None))
        z_ref[...] = z_tile(xb, r).astype(BF16)

    @pl.when(j > 0)
    def _():
        _cast_blocks(cast_in, cast_out)
        z_ref[...] = z_tile(xb_scr[...], r_scr[...]).astype(BF16)


def _inproj(x, g_mix, w, wl, *, tm, tn, emit, cast=()):
    m = x.shape[0]
    n_a = Z_GLA_WIDTH // tn
    n_j = Z_WIDTH // tn
    if emit:
        def w_row(j):
            return pl.multiple_of(jnp.where(j < n_a, j * tn, IN_OFFSETS[5] + (j - n_a) * tn), V7X_BF16_ROWS)
        w_spec = pl.BlockSpec((pl.Element(tn), pl.Element(D_MODEL)), lambda i, j: (w_row(j), 0))
        wl_spec = pl.BlockSpec((V7X_LANES, D_MODEL), lambda i, j: (IN_OFFSETS[4] // V7X_LANES, 0))
        assert m == tm, "every weight tile must be visited exactly once"
    else:
        w_spec = pl.BlockSpec((tn, D_MODEL), lambda i, j: (j, 0))
        wl_spec = pl.BlockSpec((V7X_LANES, D_MODEL), lambda i, j: (0, 0))
    cast_specs = _cast_specs(cast, N_CAST_BLOCKS, lambda i, j: i * n_j + j)
    assert not cast or (m // tm) * n_j >= N_CAST_BLOCKS
    out_specs = [
        pl.BlockSpec((tm, tn), lambda i, j: (i, j)),
        pl.BlockSpec((tm, V7X_LANES), lambda i, j: (i, 0)),
    ]
    out_shape = [
        jax.ShapeDtypeStruct((m, Z_WIDTH), BF16),
        jax.ShapeDtypeStruct((m, V7X_LANES), F32),
    ]
    if emit:
        out_specs += [pl.BlockSpec((tn, D_MODEL), lambda i, j: (j, 0)),
                      pl.BlockSpec((V7X_LANES, D_MODEL), lambda i, j: (0, 0))]
        out_shape += [jax.ShapeDtypeStruct((Z_WIDTH, D_MODEL), BF16),
                      jax.ShapeDtypeStruct((V7X_LANES, D_MODEL), BF16)]
    return pl.pallas_call(
        functools.partial(_inproj_kernel, emit=emit, n_cast=len(cast)),
        grid=(m // tm, n_j),
        in_specs=[
            pl.BlockSpec((tm, D_MODEL), lambda i, j: (i, 0)),
            pl.BlockSpec((1, D_MODEL), lambda i, j: (0, 0)),
            w_spec,
            wl_spec,
        ] + cast_specs,
        out_specs=out_specs + cast_specs,
        out_shape=out_shape + [jax.ShapeDtypeStruct(w.shape, BF16) for w in cast],
        scratch_shapes=[pltpu.VMEM((tm, D_MODEL), BF16), pltpu.VMEM((tm, 1), F32)],
        compiler_params=_params("arbitrary" if cast else "parallel", "arbitrary"),
        name="inproj",
    )(x, g_mix, w, wl, *cast)


def _group_mask(R, C):
    t = np.arange(R)[:, None]
    s = np.arange(R)[None, :]
    return ((t // C == s // C) & (s <= t)).astype(np.float32)


def _gla_tile(q_ref, k_ref, v_ref, r_ref, alow_ref, tri_ref, mask_ref, wa2_ref, ba_ref, gn_ref,
              o_ref, s_out_ref, b_scr, e_scr, *, C, s_scr=None, s0_ref=None):
    sequential = s_scr is not None
    R = q_ref.shape[0]
    n = R // C
    mid = C // 2
    W = GLA_KEY_WIDTH

    x = _dot(alow_ref[...].astype(BF16), wa2_ref[...].astype(BF16)) + ba_ref[...]
    lb = (jnp.minimum(x, 0.0) - jnp.log(1.0 + jnp.exp(-jnp.abs(x)))) * (1.0 / GLA_TAU)

    tri = tri_ref[...]
    lb_hi = lb.astype(BF16)
    lb_lo = (lb - lb_hi.astype(F32)).astype(BF16)
    b = _dot(tri, lb_hi) + _dot(tri, lb_lo)
    b3 = b.reshape(n, C, W)
    b_mid = jnp.broadcast_to(b3[:, mid:mid + 1, :], (n, C, W)).reshape(R, W)
    b_last = jnp.broadcast_to(b3[:, C - 1:C, :], (n, C, W)).reshape(R, W)

    q = q_ref[...].astype(F32) * (GLA_DK ** -0.5)
    k = k_ref[...].astype(F32)
    q1 = (q * jnp.exp(b)).astype(BF16)
    q2 = (q * jnp.exp(b - b_mid)).astype(BF16)
    k2 = (k * jnp.exp(b_mid - b)).astype(BF16)
    k3 = k * jnp.exp(b_last - b)

    e_scr[...] = jnp.zeros((V7X_LANES, GLA_DK), F32)
    for h in range(GLA_HEADS):
        b_scr[h] = b[:, h * GLA_DK:(h + 1) * GLA_DK]
        e_scr[h * n:(h + 1) * n, :] = jnp.exp(b_scr[h, pl.ds(C - 1, n, stride=C), :])
    d_cols = e_scr[...].T

    causal = mask_ref[...] > 0.0
    if not sequential:
        grp = lax.broadcasted_iota(jnp.int32, (R, 1), 0) >> (C.bit_length() - 1)

    for h in range(GLA_HEADS):
        ks = slice(h * GLA_DK, (h + 1) * GLA_DK)
        vs = slice(h * GLA_DV, (h + 1) * GLA_DV)
        v = v_ref[:, vs]
        att = jnp.where(causal, _dot_nt(q2[:, ks], k2[:, ks]), 0.0).astype(BF16)
        o = _dot(att, v)
        q1h = q1[:, ks]
        k3h = k3[:, ks]
        if sequential:
            s = s_scr[h]
            o_parts = []
            for g in range(n):
                rows = slice(g * C, (g + 1) * C)
                o_parts.append(o[rows] + _dot(q1h[rows], s.astype(BF16)))
                s = d_cols[:, h * n + g:h * n + g + 1] * s + _dot_tn(k3h[rows].astype(BF16), v[rows])
            s_scr[h] = s
            s_out_ref[0, h] = s
            o = jnp.concatenate(o_parts, axis=0)
        else:
            for g in range(n):
                s = s0_ref[g, h]
                o = o + jnp.where(grp == g, _dot(q1h, s.astype(BF16)), 0.0)
                k3g = jnp.where(grp == g, k3h, 0.0).astype(BF16)
                s_out_ref[g, h] = d_cols[:, h * n + g:h * n + g + 1] * s + _dot_tn(k3g, v)
        o = _rms(o, gn_ref[:, vs]) * jax.nn.silu(r_ref[:, vs].astype(F32))
        o_ref[:, vs] = o.astype(BF16)


N_GLA_TILE_IN = 7


def _gla_kernel(*refs, C_seq, C_step, n_cast):
    n = N_GLA_TILE_IN
    seq_in, step_in = refs[:n], refs[n:2 * n]
    s0_ref, wa2_ref, ba_ref, gn_ref = refs[2 * n:2 * n + 4]
    cast_in = refs[2 * n + 4:2 * n + 4 + n_cast]
    outs = refs[2 * n + 4 + n_cast:]
    o_seq, s_seq, o_step, s_step = outs[:4]
    cast_out = outs[4:4 + n_cast]
    s_scr, b_seq, e_seq, b_step, e_step = outs[4 + n_cast:]
    shared = (wa2_ref, ba_ref, gn_ref)

    @pl.when(pl.program_id(1) == 0)
    def _():
        s_scr[...] = jnp.zeros((GLA_HEADS, GLA_DK, GLA_DV), F32)

    _cast_blocks(cast_in, cast_out)
    _gla_tile(*seq_in, *shared, o_seq, s_seq, b_seq, e_seq, C=C_seq, s_scr=s_scr)
    _gla_tile(*step_in, *shared, o_step, s_step, b_step, e_step, C=C_step, s0_ref=s0_ref)


def _gla(zp, alow_p, zs, alow_s, s0, wa2_pad, b_a, g_norm, *, n_seq_p, len_p, n_seq_s, len_s, cast=()):
    R_p = GLA_SEQ_ROWS
    nt = len_p // R_p
    n_steps = n_seq_p * nt
    R_s = zs.shape[0] // n_steps
    assert R_s * n_steps == zs.shape[0] and R_s % len_s == 0 and R_s % V7X_BF16_ROWS == 0
    assert not cast or n_steps >= N_CAST_BLOCKS
    step_of = lambda a, c: a * nt + c
    const = lambda a, c: (0, 0)

    def tile_specs(R):
        return [
            pl.BlockSpec((R, GLA_KEY_WIDTH), lambda a, c: (step_of(a, c), 0)),
            pl.BlockSpec((R, GLA_KEY_WIDTH), lambda a, c: (step_of(a, c), 1)),
            pl.BlockSpec((R, GLA_WIDTH), lambda a, c: (step_of(a, c), 1)),
            pl.BlockSpec((R, GLA_WIDTH), lambda a, c: (step_of(a, c), 2)),
            pl.BlockSpec((R, V7X_LANES), lambda a, c: (step_of(a, c), 0)),
            pl.BlockSpec((R, R), const),
            pl.BlockSpec((R, R), const),
        ]

    def tile_args(z, alow, R, C):
        mask = _group_mask(R, C)
        return [z, z, z, z, alow, jnp.asarray(mask, BF16), jnp.asarray(mask, F32)]

    s_block_p = (1, GLA_HEADS, GLA_DK, GLA_DV)
    s_block_s = (R_s // len_s, GLA_HEADS, GLA_DK, GLA_DV)
    state_p = pl.BlockSpec(s_block_p, lambda a, c: (a, 0, 0, 0))
    state_s = pl.BlockSpec(s_block_s, lambda a, c: (step_of(a, c), 0, 0, 0))
    cast_specs = _cast_specs(cast, N_CAST_BLOCKS, step_of)
    scratch = [pltpu.VMEM((GLA_HEADS, GLA_DK, GLA_DV), F32)]
    for R in (R_p, R_s):
        scratch += [pltpu.VMEM((GLA_HEADS, R, GLA_DK), F32), pltpu.VMEM((V7X_LANES, GLA_DK), F32)]
    return pl.pallas_call(
        functools.partial(_gla_kernel, C_seq=GLA_CHUNK, C_step=len_s, n_cast=len(cast)),
        grid=(n_seq_p, nt),
        in_specs=tile_specs(R_p) + tile_specs(R_s) + [
            state_s,
            pl.BlockSpec((V7X_LANES, GLA_KEY_WIDTH), const),
            pl.BlockSpec((1, GLA_KEY_WIDTH), const),
            pl.BlockSpec((1, GLA_WIDTH), const),
        ] + cast_specs,
        out_specs=[
            pl.BlockSpec((R_p, GLA_WIDTH), lambda a, c: (step_of(a, c), 0)),
            state_p,
            pl.BlockSpec((R_s, GLA_WIDTH), lambda a, c: (step_of(a, c), 0)),
            state_s,
        ] + cast_specs,
        out_shape=[
            jax.ShapeDtypeStruct((zp.shape[0], GLA_WIDTH), BF16),
            jax.ShapeDtypeStruct((n_seq_p, GLA_HEADS, GLA_DK, GLA_DV), F32),
            jax.ShapeDtypeStruct((zs.shape[0], GLA_WIDTH), BF16),
            jax.ShapeDtypeStruct((n_seq_s, GLA_HEADS, GLA_DK, GLA_DV), F32),
        ] + [jax.ShapeDtypeStruct(w.shape, BF16) for w in cast],
        scratch_shapes=scratch,
        compiler_params=_params("parallel", "arbitrary"),
        name="gla",
    )(*tile_args(zp, alow_p, R_p, GLA_CHUNK), *tile_args(zs, alow_s, R_s, len_s), s0, wa2_pad, b_a, g_norm, *cast)


def _mixout_tile(h_ref, o_ref, u_ref, vg_ref, gln_ref, bln_ref, ws_ref, bias_ref, gout_ref, wo_ref, wm_ref,
                 h1_ref, vn_ref, *, tm, blk):
    n_chunks = tm // GMLP_CHUNK
    col_w = D_MODEL // n_chunks
    shift = blk.bit_length() - 1
    row = lax.broadcasted_iota(jnp.int32, (GMLP_CHUNK, GMLP_CHUNK), 0)
    col = lax.broadcasted_iota(jnp.int32, (GMLP_CHUNK, GMLP_CHUNK), 1)
    causal = jnp.logical_and((row >> shift) == (col >> shift), col <= row)
    w = [jnp.where(causal, ws_ref[g], 0.0).astype(BF16) for g in range(GMLP_GROUPS)]
    m_parts = []
    for c in range(n_chunks):
        cols = slice(c * col_w, (c + 1) * col_w)
        h1_ref[:, cols] = h_ref[:, cols] + _dot(o_ref[...], wo_ref[:, cols])
        rows = pl.ds(c * GMLP_CHUNK, GMLP_CHUNK)
        vg = jax.nn.gelu(vg_ref[rows, :].astype(F32))
        mu = jnp.mean(vg, axis=-1, keepdims=True)
        vc = vg - mu
        vn = vc * lax.rsqrt(jnp.mean(vc * vc, axis=-1, keepdims=True) + EPS) * gln_ref[...] + bln_ref[...]
        if vn_ref is not None:
            vn_ref[rows, :] = vn
        vn16 = vn.astype(BF16)
        mixed = jnp.concatenate(
            [_dot(w[g], vn16[:, g * GMLP_GROUP_DIM:(g + 1) * GMLP_GROUP_DIM]) for g in range(GMLP_GROUPS)],
            axis=1) + bias_ref[...]
        mm = jax.nn.gelu(u_ref[rows, :].astype(F32)) * mixed
        m_parts.append(_rms(mm, gout_ref[...]).astype(BF16))
    h1_ref[...] += _dot(jnp.concatenate(m_parts, axis=0), wm_ref[...])


def _mixout_kernel(hp_ref, hs_ref, op_ref, os_ref, up_ref, vgp_ref, us_ref, vgs_ref, gln_ref, bln_ref,
                   wsp_ref, biasp_ref, wss_ref, biass_ref, gout_ref, wo_ref, wm_ref, h1_ref, vn_ref,
                   *, tm, n_p, blk_p, blk_s):
    shared = (gln_ref, bln_ref)
    tail = (gout_ref, wo_ref, wm_ref, h1_ref)
    i = pl.program_id(0)

    @pl.when(i < n_p)
    def _():
        _mixout_tile(hp_ref, op_ref, up_ref, vgp_ref, *shared, wsp_ref, biasp_ref, *tail, None, tm=tm, blk=blk_p)

    @pl.when(i >= n_p)
    def _():
        _mixout_tile(hs_ref, os_ref, us_ref, vgs_ref, *shared, wss_ref, biass_ref, *tail, vn_ref, tm=tm, blk=blk_s)


def _gmlp_operands(w_s, b_s, blk):
    reps = GMLP_CHUNK // blk
    sel = jnp.asarray(np.tile(np.eye(blk, dtype=np.float32), (reps, 1)))
    w_mix = jnp.einsum("ra,gab,cb->grc", sel, w_s[:, :blk, :blk], sel, precision=lax.Precision.HIGHEST)
    bias = jnp.tile(jnp.repeat(b_s[:, :blk].T, GMLP_GROUP_DIM, axis=1), (reps, 1))
    return w_mix, bias


def _mixout(hp, hs, op, os_, zp, zs, g_ln, b_ln, w_s, b_s, g_out, w_out, *, tm, len_p, len_s):
    n_p, n_s = hp.shape[0] // tm, hs.shape[0] // tm
    blk_p, blk_s = min(len_p, GMLP_CHUNK), min(len_s, GMLP_CHUNK)
    pidx = lambda i: jnp.minimum(i, n_p - 1)
    sidx = lambda i: jnp.maximum(i - n_p, 0)
    once = pl.Buffered(1)
    vec = pl.BlockSpec((1, GMLP_WIDTH), lambda i: (0, 0))
    mixw = pl.BlockSpec((GMLP_GROUPS, GMLP_CHUNK, GMLP_CHUNK), lambda i: (0, 0, 0))
    biasb = pl.BlockSpec((GMLP_CHUNK, GMLP_WIDTH), lambda i: (0, 0))
    return pl.pallas_call(
        functools.partial(_mixout_kernel, tm=tm, n_p=n_p, blk_p=blk_p, blk_s=blk_s),
        grid=(n_p + n_s,),
        in_specs=[
            pl.BlockSpec((tm, D_MODEL), lambda i: (pidx(i), 0)),
            pl.BlockSpec((tm, D_MODEL), lambda i: (sidx(i), 0)),
            pl.BlockSpec((tm, GLA_WIDTH), lambda i: (pidx(i), 0)),
            pl.BlockSpec((tm, GLA_WIDTH), lambda i: (sidx(i), 0)),
            pl.BlockSpec((tm, GMLP_WIDTH), lambda i: (pidx(i), Z_U_OFF // GMLP_WIDTH)),
            pl.BlockSpec((tm, GMLP_WIDTH), lambda i: (pidx(i), Z_VG_OFF // GMLP_WIDTH)),
            pl.BlockSpec((tm, GMLP_WIDTH), lambda i: (sidx(i), Z_U_OFF // GMLP_WIDTH)),
            pl.BlockSpec((tm, GMLP_WIDTH), lambda i: (sidx(i), Z_VG_OFF // GMLP_WIDTH)),
            vec, vec,
            mixw, biasb, mixw, biasb,
            vec,
            pl.BlockSpec((GLA_WIDTH, D_MODEL), lambda i: (0, 0), pipeline_mode=once),
            pl.BlockSpec((GMLP_WIDTH, D_MODEL), lambda i: (1, 0), pipeline_mode=once),
        ],
        out_specs=[
            pl.BlockSpec((tm, D_MODEL), lambda i: (i, 0)),
            pl.BlockSpec((tm, GMLP_WIDTH), lambda i: (sidx(i), 0)),
        ],
        out_shape=[
            jax.ShapeDtypeStruct((hp.shape[0] + hs.shape[0], D_MODEL), F32),
            jax.ShapeDtypeStruct((hs.shape[0], GMLP_WIDTH), F32),
        ],
        compiler_params=_params("arbitrary"),
        name="mixout",
    )(hp, hs, op, os_, zp, zp, zs, zs, g_ln, b_ln, *_gmlp_operands(w_s, b_s, blk_p), *_gmlp_operands(w_s, b_s, blk_s),
      g_out, w_out, w_out)


N_FFN_IN = 5


def _ffn_kernel(*refs, n_cast):
    h_ref, g_ref, wg_ref, wu_ref, wo_ref = refs[:N_FFN_IN]
    cast_in = refs[N_FFN_IN:N_FFN_IN + n_cast]
    o_ref = refs[N_FFN_IN + n_cast]
    cast_out = refs[N_FFN_IN + n_cast + 1:N_FFN_IN + 2 * n_cast + 1]
    hb_scr, r_scr = refs[N_FFN_IN + 2 * n_cast + 1:]
    j = pl.program_id(1)

    def tile(hb, r):
        gate = r * _dot(hb, wg_ref[...])
        up = r * _dot(hb, wu_ref[...])
        return _dot((jax.nn.silu(gate) * up).astype(BF16), wo_ref[...])

    @pl.when(j == 0)
    def _():
        _cast_blocks(cast_in, cast_out)
        h = h_ref[...]
        hb = (h * g_ref[...]).astype(BF16)
        hb_scr[...] = hb
        r = lax.rsqrt(jnp.mean(h * h, axis=-1, keepdims=True) + EPS)
        r_scr[...] = r
        o_ref[...] = h + tile(hb, r)

    @pl.when(j > 0)
    def _():
        _cast_blocks(cast_in, cast_out)
        o_ref[...] += tile(hb_scr[...], r_scr[...])


def _ffn(h1, g_ffn, w_in, w_out, tm, tf, cast=()):
    m = h1.shape[0]
    nf = D_FF // tf
    cast_specs = _cast_specs(cast, N_CAST_BLOCKS, lambda i, j: i * nf + j)
    assert not cast or (m // tm) * nf >= N_CAST_BLOCKS
    return pl.pallas_call(
        functools.partial(_ffn_kernel, n_cast=len(cast)),
        grid=(m // tm, nf),
        in_specs=[
            pl.BlockSpec((tm, D_MODEL), lambda i, j: (i, 0)),
            pl.BlockSpec((1, D_MODEL), lambda i, j: (0, 0)),
            pl.BlockSpec((D_MODEL, tf), lambda i, j: (0, j)),
            pl.BlockSpec((D_MODEL, tf), lambda i, j: (0, nf + j)),
            pl.BlockSpec((tf, D_MODEL), lambda i, j: (j, 0)),
        ] + cast_specs,
        out_specs=[pl.BlockSpec((tm, D_MODEL), lambda i, j: (i, 0))] + cast_specs,
        out_shape=[jax.ShapeDtypeStruct((m, D_MODEL), F32)] + [jax.ShapeDtypeStruct(w.shape, BF16) for w in cast],
        scratch_shapes=[pltpu.VMEM((tm, D_MODEL), BF16), pltpu.VMEM((tm, 1), F32)],
        compiler_params=_params("arbitrary" if cast else "parallel", "arbitrary"),
        name="ffn",
    )(h1, g_ffn, w_in, w_in, w_out, *cast)


def _ple_tile(h_ref, p_ref, wp_ref, gp_ref, gg_ref, wg_ref, gf_ref, y_ref):
    h = h_ref[...]
    r = lax.rsqrt(jnp.mean(h * h, axis=-1, keepdims=True) + EPS)
    hb = (h * gg_ref[...]).astype(BF16)
    pe = _rms(_dot(p_ref[...].astype(BF16), wp_ref[...]), gp_ref[...])
    ssq = jnp.zeros((h.shape[0], 1), F32)
    for c in range(D_MODEL // PLE_COL_CHUNK):
        cols = slice(c * PLE_COL_CHUNK, (c + 1) * PLE_COL_CHUNK)
        t = h[:, cols] + pe[:, cols] * jax.nn.sigmoid(r * _dot(hb, wg_ref[:, cols]))
        ssq = ssq + jnp.sum(t * t, axis=-1, keepdims=True)
        y_ref[:, cols] = t
    y_ref[...] = y_ref[...] * lax.rsqrt(ssq / D_MODEL + EPS) * gf_ref[...]


def _ple_kernel(h_ref, pp_ref, ps_ref, wp_ref, gp_ref, gg_ref, wg_ref, gf_ref, yp_ref, ys_ref, *, n_p):
    params = (wp_ref, gp_ref, gg_ref, wg_ref, gf_ref)
    i = pl.program_id(0)

    @pl.when(i < n_p)
    def _():
        _ple_tile(h_ref, pp_ref, *params, yp_ref)

    @pl.when(i >= n_p)
    def _():
        _ple_tile(h_ref, ps_ref, *params, ys_ref)


def _ple(h, pp, ps, w_ple, g_ple, g_gate, w_gate, g_final, tm):
    n_p, n_s = pp.shape[0] // tm, ps.shape[0] // tm
    pidx = lambda i: (jnp.minimum(i, n_p - 1), 0)
    sidx = lambda i: (jnp.maximum(i - n_p, 0), 0)
    vec = pl.BlockSpec((1, D_MODEL), lambda i: (0, 0))
    return pl.pallas_call(
        functools.partial(_ple_kernel, n_p=n_p),
        grid=(n_p + n_s,),
        in_specs=[
            pl.BlockSpec((tm, D_MODEL), lambda i: (i, 0)),
            pl.BlockSpec((tm, PLE_DIM), pidx),
            pl.BlockSpec((tm, PLE_DIM), sidx),
            pl.BlockSpec((PLE_DIM, D_MODEL), lambda i: (0, 0)),
            vec, vec,
            pl.BlockSpec((D_MODEL, D_MODEL), lambda i: (0, 0), pipeline_mode=pl.Buffered(1)),
            vec,
        ],
        out_specs=[pl.BlockSpec((tm, D_MODEL), pidx), pl.BlockSpec((tm, D_MODEL), sidx)],
        out_shape=[jax.ShapeDtypeStruct((pp.shape[0], D_MODEL), F32),
                   jax.ShapeDtypeStruct((ps.shape[0], D_MODEL), F32)],
        compiler_params=_params("arbitrary"),
        name="ple",
    )(h, pp, ps, w_ple, g_ple, g_gate, w_gate, g_final)


def kernel(x_prompt, x_sample, state_gla, p_prompt, p_sample, g_mix, w_in, w_a2, b_a, g_gla_norm, g_gmlp_ln,
           b_gmlp_ln, w_s, b_s, g_gmlp_out, w_out, g_ffn, w_ffn_in, w_ffn_out, w_ple, g_ple, g_ple_gate,
           w_ple_gate, g_final):
    batch, seq, _ = x_prompt.shape
    dec_batch, dec_seq, _ = x_sample.shape
    depth = w_in.shape[0]
    assert depth == 1, "one layer: the prompt / sample passes below are not chained over depth"
    row = lambda a: a.reshape(1, -1)
    wts = dict(
        g_mix=row(g_mix[0]),
        wa2_pad=jnp.pad(w_a2[0], ((0, V7X_LANES - GLA_GATE_RANK), (0, 0))),
        b_a=row(b_a[0]),
        g_gla_norm=row(g_gla_norm[0]),
        g_gmlp_ln=row(g_gmlp_ln[0]),
        b_gmlp_ln=row(b_gmlp_ln[0]),
        w_s=w_s[0],
        b_s=b_s[0],
        g_gmlp_out=row(g_gmlp_out[0]),
        g_ffn=row(g_ffn[0]),
        w_ple=w_ple[0].astype(BF16),
        g_ple=row(g_ple[0]),
        g_ple_gate=row(g_ple_gate[0]),
        g_final=row(g_final),
    )
    xp = x_prompt.reshape(batch * seq, D_MODEL)
    xs = x_sample.reshape(dec_batch * dec_seq, D_MODEL)
    wt_in = jnp.swapaxes(w_in, 1, 2)[0]
    zs, alow_s, wt16, wl16 = _inproj(xs, wts["g_mix"], wt_in, wt_in, tm=xs.shape[0], tn=INPROJ_COLS, emit=True)
    zp, alow_p, wts["w_out"], wts["w_ffn_out"] = _inproj(
        xp, wts["g_mix"], wt16, wl16, tm=INPROJ_ROWS, tn=INPROJ_COLS, emit=False, cast=(w_out[0], w_ffn_out[0]))
    op, sp, os_, ss, wts["w_ffn_in"] = _gla(
        zp, alow_p, zs, alow_s, state_gla[0], wts["wa2_pad"], wts["b_a"], wts["g_gla_norm"],
        n_seq_p=batch, len_p=seq, n_seq_s=dec_batch, len_s=dec_seq, cast=(w_ffn_in[0],))
    h1, vs = _mixout(xp, xs, op, os_, zp, zs, wts["g_gmlp_ln"], wts["b_gmlp_ln"], wts["w_s"], wts["b_s"],
                     wts["g_gmlp_out"], wts["w_out"], tm=MIX_ROWS, len_p=seq, len_s=dec_seq)
    h2, wts["w_ple_gate"] = _ffn(h1, wts["g_ffn"], wts["w_ffn_in"], wts["w_ffn_out"], tm=FFN_ROWS, tf=FFN_COLS,
                                 cast=(w_ple_gate[0],))
    yp, ys = _ple(h2, p_prompt[0].reshape(batch * seq, PLE_DIM), p_sample[0].reshape(dec_batch * dec_seq, PLE_DIM),
                  wts["w_ple"], wts["g_ple"], wts["g_ple_gate"], wts["w_ple_gate"], wts["g_final"], tm=PLE_ROWS)
    return (yp.reshape(batch, seq, D_MODEL), ys.reshape(dec_batch, dec_seq, D_MODEL), sp[None], ss[None],
            vs.reshape(1, dec_batch, dec_seq, GMLP_WIDTH))
```

```python
import functools

import numpy as np

import jax
import jax.numpy as jnp
from jax import lax
from jax.experimental import pallas as pl
from jax.experimental.pallas import tpu as pltpu

F32 = jnp.float32
BF16 = jnp.bfloat16

D_MODEL = 2048
GLA_WIDTH = 1024
GMLP_WIDTH = 1024
GLA_HEADS = 4
GLA_DV = 256
GLA_DK = 128
GLA_KEY_WIDTH = 512
GLA_GATE_RANK = 16
GLA_TAU = 16.0
GLA_CHUNK = 64
GMLP_GROUPS = 8
GMLP_GROUP_DIM = 128
GMLP_CHUNK = 128
D_FF = 5632
PLE_DIM = 256
EPS = 1e-6
IN_OFFSETS = (0, 512, 1024, 2048, 3072, 3088, 4112, 5136)
Z_WIDTH = 5120
Z_GLA_WIDTH = 3072
Z_U_OFF = 3072
Z_VG_OFF = 4096

V7X_LANES = 128
V7X_BF16_ROWS = 16
V7X_VMEM_BYTES = 64 * 1024 * 1024
VMEM_LIMIT_BYTES = V7X_VMEM_BYTES - 4 * 1024 * 1024
INPROJ_ROWS, INPROJ_COLS = 1024, 1024
MIX_ROWS = 512
FFN_ROWS, FFN_COLS = 1024, 512
PLE_ROWS, PLE_COL_CHUNK = 512, 1024
GLA_SEQ_ROWS = 4 * GLA_CHUNK
N_CAST_BLOCKS = 32
W_RING_SLOTS = 3


def _params(*sem):
    return pltpu.CompilerParams(dimension_semantics=sem, vmem_limit_bytes=VMEM_LIMIT_BYTES)


def _rms(x, g):
    return x * lax.rsqrt(jnp.mean(x * x, axis=-1, keepdims=True) + EPS) * g


def _dot(a, b):
    return jnp.dot(a, b, preferred_element_type=F32)


def _dot_tn(a, b):
    return lax.dot_general(a, b, (((0,), (0,)), ((), ())), preferred_element_type=F32)


def _dot_nt(a, b):
    return lax.dot_general(a, b, (((1,), (1,)), ((), ())), preferred_element_type=F32)


def _cast_specs(cast, n_blocks, step_of):
    in_specs, out_specs, out_shape = [], [], []
    for item in cast:
        w, k, n = item if isinstance(item, tuple) else (item, 0, 1)
        rows, cols = w.shape[0] // n_blocks, w.shape[1] // n
        assert rows * n_blocks == w.shape[0] and rows % V7X_BF16_ROWS == 0 and cols % V7X_LANES == 0
        step = lambda *ids: jnp.minimum(step_of(*ids), n_blocks - 1)
        in_specs.append(pl.BlockSpec((rows, cols), lambda *ids, step=step, k=k: (step(*ids), k)))
        out_specs.append(pl.BlockSpec((rows, cols), lambda *ids, step=step: (step(*ids), 0)))
        out_shape.append(jax.ShapeDtypeStruct((w.shape[0], cols), BF16))
    return in_specs, out_specs, out_shape


def _cast_args(cast):
    return [item[0] if isinstance(item, tuple) else item for item in cast]


def _cast_blocks(cast_in, cast_out):
    for src_ref, dst_ref in zip(cast_in, cast_out):
        dst_ref[...] = src_ref[...].astype(BF16)


def _inproj_kernel(*refs, emit, n_cast):
    x_ref, g_ref, w_ref, wl_ref = refs[:4]
    cast_in = refs[4:4 + n_cast]
    n_out = 4 if emit else 2
    z_ref, alow_ref = refs[4 + n_cast:6 + n_cast]
    emit_refs = refs[6 + n_cast:4 + n_cast + n_out]
    cast_out = refs[4 + n_cast + n_out:4 + 2 * n_cast + n_out]
    xb_scr, r_scr = refs[4 + 2 * n_cast + n_out:6 + 2 * n_cast + n_out]
    j = pl.program_id(1)

    if not emit:
        w_ring, w_sem = refs[6 + 2 * n_cast + n_out:]
        tn = w_ring.shape[1]
        n_j = pl.num_programs(1)
        n_steps = pl.num_programs(0) * n_j
        step = pl.program_id(0) * n_j + j

        def tile_copy(s):
            slot = lax.rem(s, W_RING_SLOTS)
            row = pl.multiple_of(lax.rem(s, n_j) * tn, tn)
            return pltpu.make_async_copy(w_ref.at[pl.ds(row, tn), :], w_ring.at[slot], w_sem.at[slot])

        @pl.when(step == 0)
        def _():
            for s in range(W_RING_SLOTS - 1):
                tile_copy(s).start()

        @pl.when(step + (W_RING_SLOTS - 1) < n_steps)
        def _():
            tile_copy(step + (W_RING_SLOTS - 1)).start()

        tile_copy(step).wait()
        w_slot = lax.rem(step, W_RING_SLOTS)

    def weight(ref, emit_ref):
        if not emit:
            return ref[...]
        w16 = ref[...].astype(BF16)
        emit_ref[...] = w16
        return w16

    def z_tile(xb, r):
        w16 = weight(w_ref, emit_refs[0]) if emit else w_ring[w_slot]
        return r * _dot_nt(xb, w16)

    @pl.when(j == 0)
    def _():
        _cast_blocks(cast_in, cast_out)
        x = x_ref[...]
        xb = (x * g_ref[...]).astype(BF16)
        xb_scr[...] = xb
        r = lax.rsqrt(jnp.mean(x * x, axis=-1, keepdims=True) + EPS)
        r_scr[...] = r
        alow_ref[...] = r * _dot_nt(xb, weight(wl_ref, emit_refs[1] if emit else None))
        z_ref[...] = z_tile(xb, r).astype(BF16)

    @pl.when(j > 0)
    def _():
        _cast_blocks(cast_in, cast_out)
        z_ref[...] = z_tile(xb_scr[...], r_scr[...]).astype(BF16)


def _inproj(x, g_mix, w, wl, *, tm, tn, emit, cast=()):
    m = x.shape[0]
    n_a = Z_GLA_WIDTH // tn
    n_j = Z_WIDTH // tn
    if emit:
        def w_row(j):
            return pl.multiple_of(jnp.where(j < n_a, j * tn, IN_OFFSETS[5] + (j - n_a) * tn), V7X_BF16_ROWS)
        w_spec = pl.BlockSpec((pl.Element(tn), pl.Element(D_MODEL)), lambda i, j: (w_row(j), 0))
        wl_spec = pl.BlockSpec((V7X_LANES, D_MODEL), lambda i, j: (IN_OFFSETS[4] // V7X_LANES, 0))
        assert m == tm, "every weight tile must be visited exactly once"
    else:
        w_spec = pl.BlockSpec(memory_space=pl.ANY)
        wl_spec = pl.BlockSpec((V7X_LANES, D_MODEL), lambda i, j: (0, 0))
        assert (m // tm) * n_j >= W_RING_SLOTS - 1
    cast_in, cast_out, cast_shape = _cast_specs(cast, N_CAST_BLOCKS, lambda i, j: i * n_j + j)
    assert not cast or (m // tm) * n_j >= N_CAST_BLOCKS
    out_specs = [
        pl.BlockSpec((tm, tn), lambda i, j: (i, j)),
        pl.BlockSpec((tm, V7X_LANES), lambda i, j: (i, 0)),
    ]
    out_shape = [
        jax.ShapeDtypeStruct((m, Z_WIDTH), BF16),
        jax.ShapeDtypeStruct((m, V7X_LANES), F32),
    ]
    if emit:
        out_specs += [pl.BlockSpec((tn, D_MODEL), lambda i, j: (j, 0)),
                      pl.BlockSpec((V7X_LANES, D_MODEL), lambda i, j: (0, 0))]
        out_shape += [jax.ShapeDtypeStruct((Z_WIDTH, D_MODEL), BF16),
                      jax.ShapeDtypeStruct((V7X_LANES, D_MODEL), BF16)]
    ring = [] if emit else [pltpu.VMEM((W_RING_SLOTS, tn, D_MODEL), BF16), pltpu.SemaphoreType.DMA((W_RING_SLOTS,))]
    return pl.pallas_call(
        functools.partial(_inproj_kernel, emit=emit, n_cast=len(cast)),
        grid=(m // tm, n_j),
        in_specs=[
            pl.BlockSpec((tm, D_MODEL), lambda i, j: (i, 0)),
            pl.BlockSpec((1, D_MODEL), lambda i, j: (0, 0)),
            w_spec,
            wl_spec,
        ] + cast_in,
        out_specs=out_specs + cast_out,
        out_shape=out_shape + cast_shape,
        scratch_shapes=[pltpu.VMEM((tm, D_MODEL), BF16), pltpu.VMEM((tm, 1), F32)] + ring,
        compiler_params=_params("parallel" if emit else "arbitrary", "arbitrary"),
        name="inproj",
    )(x, g_mix, w, wl, *_cast_args(cast))


def _group_mask(R, C):
    t = np.arange(R)[:, None]
    s = np.arange(R)[None, :]
    return ((t // C == s // C) & (s <= t)).astype(np.float32)


def _gla_tile(q_ref, k_ref, v_ref, r_ref, alow_ref, tri_ref, mask_ref, wa2_ref, ba_ref, gn_ref,
              o_ref, s_out_ref, b_scr, e_scr, *, C, s_scr=None, s0_ref=None):
    sequential = s_scr is not None
    R = q_ref.shape[0]
    n = R // C
    mid = C // 2
    W = GLA_KEY_WIDTH

    x = _dot(alow_ref[...].astype(BF16), wa2_ref[...].astype(BF16)) + ba_ref[...]
    lb = (jnp.minimum(x, 0.0) - jnp.log(1.0 + jnp.exp(-jnp.abs(x)))) * (1.0 / GLA_TAU)

    tri = tri_ref[...]
    lb_hi = lb.astype(BF16)
    lb_lo = (lb - lb_hi.astype(F32)).astype(BF16)
    b = _dot(tri, lb_hi) + _dot(tri, lb_lo)
    b3 = b.reshape(n, C, W)
    b_mid = jnp.broadcast_to(b3[:, mid:mid + 1, :], (n, C, W)).reshape(R, W)
    b_last = jnp.broadcast_to(b3[:, C - 1:C, :], (n, C, W)).reshape(R, W)

    q = q_ref[...].astype(F32) * (GLA_DK ** -0.5)
    k = k_ref[...].astype(F32)
    q1 = (q * jnp.exp(b)).astype(BF16)
    q2 = (q * jnp.exp(b - b_mid)).astype(BF16)
    k2 = (k * jnp.exp(b_mid - b)).astype(BF16)
    k3 = k * jnp.exp(b_last - b)

    e_scr[...] = jnp.zeros((V7X_LANES, GLA_DK), F32)
    for h in range(GLA_HEADS):
        b_scr[h] = b[:, h * GLA_DK:(h + 1) * GLA_DK]
        e_scr[h * n:(h + 1) * n, :] = jnp.exp(b_scr[h, pl.ds(C - 1, n, stride=C), :])
    d_cols = e_scr[...].T

    causal = mask_ref[...] > 0.0
    if not sequential:
        grp = lax.broadcasted_iota(jnp.int32, (R, 1), 0) >> (C.bit_length() - 1)

    for h in range(GLA_HEADS):
        ks = slice(h * GLA_DK, (h + 1) * GLA_DK)
        vs = slice(h * GLA_DV, (h + 1) * GLA_DV)
        v = v_ref[:, vs]
        att = jnp.where(causal, _dot_nt(q2[:, ks], k2[:, ks]), 0.0).astype(BF16)
        o = _dot(att, v)
        q1h = q1[:, ks]
        k3h = k3[:, ks]
        if sequential:
            s = s_scr[h]
            o_parts = []
            for g in range(n):
                rows = slice(g * C, (g + 1) * C)
                o_parts.append(o[rows] + _dot(q1h[rows], s.astype(BF16)))
                s = d_cols[:, h * n + g:h * n + g + 1] * s + _dot_tn(k3h[rows].astype(BF16), v[rows])
            s_scr[h] = s
            s_out_ref[0, h] = s
            o = jnp.concatenate(o_parts, axis=0)
        else:
            for g in range(n):
                s = s0_ref[g, h]
                o = o + jnp.where(grp == g, _dot(q1h, s.astype(BF16)), 0.0)
                k3g = jnp.where(grp == g, k3h, 0.0).astype(BF16)
                s_out_ref[g, h] = d_cols[:, h * n + g:h * n + g + 1] * s + _dot_tn(k3g, v)
        o = _rms(o, gn_ref[:, vs]) * jax.nn.silu(r_ref[:, vs].astype(F32))
        o_ref[:, vs] = o.astype(BF16)


N_GLA_TILE_IN = 7


def _gla_kernel(*refs, C_seq, C_step, n_cast):
    n = N_GLA_TILE_IN
    seq_in, step_in = refs[:n], refs[n:2 * n]
    s0_ref, wa2_ref, ba_ref, gn_ref = refs[2 * n:2 * n + 4]
    cast_in = refs[2 * n + 4:2 * n + 4 + n_cast]
    outs = refs[2 * n + 4 + n_cast:]
    o_seq, s_seq, o_step, s_step = outs[:4]
    cast_out = outs[4:4 + n_cast]
    s_scr, b_seq, e_seq, b_step, e_step = outs[4 + n_cast:]
    shared = (wa2_ref, ba_ref, gn_ref)

    @pl.when(pl.program_id(1) == 0)
    def _():
        s_scr[...] = jnp.zeros((GLA_HEADS, GLA_DK, GLA_DV), F32)

    _cast_blocks(cast_in, cast_out)
    _gla_tile(*seq_in, *shared, o_seq, s_seq, b_seq, e_seq, C=C_seq, s_scr=s_scr)
    _gla_tile(*step_in, *shared, o_step, s_step, b_step, e_step, C=C_step, s0_ref=s0_ref)


def _gla(zp, alow_p, zs, alow_s, s0, wa2_pad, b_a, g_norm, *, n_seq_p, len_p, n_seq_s, len_s, cast=()):
    R_p = GLA_SEQ_ROWS
    nt = len_p // R_p
    n_steps = n_seq_p * nt
    R_s = zs.shape[0] // n_steps
    assert R_s * n_steps == zs.shape[0] and R_s % len_s == 0 and R_s % V7X_BF16_ROWS == 0
    assert not cast or n_steps >= N_CAST_BLOCKS
    step_of = lambda a, c: a * nt + c
    const = lambda a, c: (0, 0)

    def tile_specs(R):
        return [
            pl.BlockSpec((R, GLA_KEY_WIDTH), lambda a, c: (step_of(a, c), 0)),
            pl.BlockSpec((R, GLA_KEY_WIDTH), lambda a, c: (step_of(a, c), 1)),
            pl.BlockSpec((R, GLA_WIDTH), lambda a, c: (step_of(a, c), 1)),
            pl.BlockSpec((R, GLA_WIDTH), lambda a, c: (step_of(a, c), 2)),
            pl.BlockSpec((R, V7X_LANES), lambda a, c: (step_of(a, c), 0)),
            pl.BlockSpec((R, R), const),
            pl.BlockSpec((R, R), const),
        ]

    def tile_args(z, alow, R, C):
        mask = _group_mask(R, C)
        return [z, z, z, z, alow, jnp.asarray(mask, BF16), jnp.asarray(mask, F32)]

    s_block_p = (1, GLA_HEADS, GLA_DK, GLA_DV)
    s_block_s = (R_s // len_s, GLA_HEADS, GLA_DK, GLA_DV)
    state_p = pl.BlockSpec(s_block_p, lambda a, c: (a, 0, 0, 0))
    state_s = pl.BlockSpec(s_block_s, lambda a, c: (step_of(a, c), 0, 0, 0))
    cast_in, cast_out, cast_shape = _cast_specs(cast, N_CAST_BLOCKS, step_of)
    scratch = [pltpu.VMEM((GLA_HEADS, GLA_DK, GLA_DV), F32)]
    for R in (R_p, R_s):
        scratch += [pltpu.VMEM((GLA_HEADS, R, GLA_DK), F32), pltpu.VMEM((V7X_LANES, GLA_DK), F32)]
    return pl.pallas_call(
        functools.partial(_gla_kernel, C_seq=GLA_CHUNK, C_step=len_s, n_cast=len(cast)),
        grid=(n_seq_p, nt),
        in_specs=tile_specs(R_p) + tile_specs(R_s) + [
            state_s,
            pl.BlockSpec((V7X_LANES, GLA_KEY_WIDTH), const),
            pl.BlockSpec((1, GLA_KEY_WIDTH), const),
            pl.BlockSpec((1, GLA_WIDTH), const),
        ] + cast_in,
        out_specs=[
            pl.BlockSpec((R_p, GLA_WIDTH), lambda a, c: (step_of(a, c), 0)),
            state_p,
            pl.BlockSpec((R_s, GLA_WIDTH), lambda a, c: (step_of(a, c), 0)),
            state_s,
        ] + cast_out,
        out_shape=[
            jax.ShapeDtypeStruct((zp.shape[0], GLA_WIDTH), BF16),
            jax.ShapeDtypeStruct((n_seq_p, GLA_HEADS, GLA_DK, GLA_DV), F32),
            jax.ShapeDtypeStruct((zs.shape[0], GLA_WIDTH), BF16),
            jax.ShapeDtypeStruct((n_seq_s, GLA_HEADS, GLA_DK, GLA_DV), F32),
        ] + cast_shape,
        scratch_shapes=scratch,
        compiler_params=_params("parallel", "arbitrary"),
        name="gla",
    )(*tile_args(zp, alow_p, R_p, GLA_CHUNK), *tile_args(zs, alow_s, R_s, len_s), s0, wa2_pad, b_a, g_norm,
      *_cast_args(cast))


def _mixout_tile(h_ref, o_ref, u_ref, vg_ref, gln_ref, bln_ref, ws_ref, bias_ref, gout_ref, wo_ref, wm_ref,
                 h1_ref, vn_ref, *, tm, blk):
    n_chunks = tm // GMLP_CHUNK
    col_w = D_MODEL // n_chunks
    shift = blk.bit_length() - 1
    row = lax.broadcasted_iota(jnp.int32, (GMLP_CHUNK, GMLP_CHUNK), 0)
    col = lax.broadcasted_iota(jnp.int32, (GMLP_CHUNK, GMLP_CHUNK), 1)
    causal = jnp.logical_and((row >> shift) == (col >> shift), col <= row)
    w = [jnp.where(causal, ws_ref[g], 0.0).astype(BF16) for g in range(GMLP_GROUPS)]
    m_parts = []
    for c in range(n_chunks):
        cols = slice(c * col_w, (c + 1) * col_w)
        h1_ref[:, cols] = h_ref[:, cols] + _dot(o_ref[...], wo_ref[:, cols])
        rows = pl.ds(c * GMLP_CHUNK, GMLP_CHUNK)
        vg = jax.nn.gelu(vg_ref[rows, :].astype(F32))
        mu = jnp.mean(vg, axis=-1, keepdims=True)
        vc = vg - mu
        vn = vc * lax.rsqrt(jnp.mean(vc * vc, axis=-1, keepdims=True) + EPS) * gln_ref[...] + bln_ref[...]
        if vn_ref is not None:
            vn_ref[rows, :] = vn
        vn16 = vn.astype(BF16)
        mixed = jnp.concatenate(
            [_dot(w[g], vn16[:, g * GMLP_GROUP_DIM:(g + 1) * GMLP_GROUP_DIM]) for g in range(GMLP_GROUPS)],
            axis=1) + bias_ref[...]
        mm = jax.nn.gelu(u_ref[rows, :].astype(F32)) * mixed
        m_parts.append(_rms(mm, gout_ref[...]).astype(BF16))
    h1_ref[...] += _dot(jnp.concatenate(m_parts, axis=0), wm_ref[...])


def _mixout_kernel(hp_ref, hs_ref, op_ref, os_ref, up_ref, vgp_ref, us_ref, vgs_ref, gln_ref, bln_ref,
                   wsp_ref, biasp_ref, wss_ref, biass_ref, gout_ref, wo_ref, wm_ref, h1_ref, vn_ref,
                   *, tm, n_p, blk_p, blk_s):
    shared = (gln_ref, bln_ref)
    tail = (gout_ref, wo_ref, wm_ref, h1_ref)
    i = pl.program_id(0)

    @pl.when(i < n_p)
    def _():
        _mixout_tile(hp_ref, op_ref, up_ref, vgp_ref, *shared, wsp_ref, biasp_ref, *tail, None, tm=tm, blk=blk_p)

    @pl.when(i >= n_p)
    def _():
        _mixout_tile(hs_ref, os_ref, us_ref, vgs_ref, *shared, wss_ref, biass_ref, *tail, vn_ref, tm=tm, blk=blk_s)


def _gmlp_operands(w_s, b_s, blk):
    reps = GMLP_CHUNK // blk
    sel = jnp.asarray(np.tile(np.eye(blk, dtype=np.float32), (reps, 1)))
    w_mix = jnp.einsum("ra,gab,cb->grc", sel, w_s[:, :blk, :blk], sel, precision=lax.Precision.HIGHEST)
    bias = jnp.tile(jnp.repeat(b_s[:, :blk].T, GMLP_GROUP_DIM, axis=1), (reps, 1))
    return w_mix, bias


def _mixout(hp, hs, op, os_, zp, zs, g_ln, b_ln, w_s, b_s, g_out, w_out, *, tm, len_p, len_s):
    n_p, n_s = hp.shape[0] // tm, hs.shape[0] // tm
    blk_p, blk_s = min(len_p, GMLP_CHUNK), min(len_s, GMLP_CHUNK)
    pidx = lambda i: jnp.minimum(i, n_p - 1)
    sidx = lambda i: jnp.maximum(i - n_p, 0)
    once = pl.Buffered(1)
    vec = pl.BlockSpec((1, GMLP_WIDTH), lambda i: (0, 0))
    mixw = pl.BlockSpec((GMLP_GROUPS, GMLP_CHUNK, GMLP_CHUNK), lambda i: (0, 0, 0))
    biasb = pl.BlockSpec((GMLP_CHUNK, GMLP_WIDTH), lambda i: (0, 0))
    return pl.pallas_call(
        functools.partial(_mixout_kernel, tm=tm, n_p=n_p, blk_p=blk_p, blk_s=blk_s),
        grid=(n_p + n_s,),
        in_specs=[
            pl.BlockSpec((tm, D_MODEL), lambda i: (pidx(i), 0)),
            pl.BlockSpec((tm, D_MODEL), lambda i: (sidx(i), 0)),
            pl.BlockSpec((tm, GLA_WIDTH), lambda i: (pidx(i), 0)),
            pl.BlockSpec((tm, GLA_WIDTH), lambda i: (sidx(i), 0)),
            pl.BlockSpec((tm, GMLP_WIDTH), lambda i: (pidx(i), Z_U_OFF // GMLP_WIDTH)),
            pl.BlockSpec((tm, GMLP_WIDTH), lambda i: (pidx(i), Z_VG_OFF // GMLP_WIDTH)),
            pl.BlockSpec((tm, GMLP_WIDTH), lambda i: (sidx(i), Z_U_OFF // GMLP_WIDTH)),
            pl.BlockSpec((tm, GMLP_WIDTH), lambda i: (sidx(i), Z_VG_OFF // GMLP_WIDTH)),
            vec, vec,
            mixw, biasb, mixw, biasb,
            vec,
            pl.BlockSpec((GLA_WIDTH, D_MODEL), lambda i: (0, 0), pipeline_mode=once),
            pl.BlockSpec((GMLP_WIDTH, D_MODEL), lambda i: (1, 0), pipeline_mode=once),
        ],
        out_specs=[
            pl.BlockSpec((tm, D_MODEL), lambda i: (i, 0)),
            pl.BlockSpec((tm, GMLP_WIDTH), lambda i: (sidx(i), 0)),
        ],
        out_shape=[
            jax.ShapeDtypeStruct((hp.shape[0] + hs.shape[0], D_MODEL), F32),
            jax.ShapeDtypeStruct((hs.shape[0], GMLP_WIDTH), F32),
        ],
        compiler_params=_params("arbitrary"),
        name="mixout",
    )(hp, hs, op, os_, zp, zp, zs, zs, g_ln, b_ln, *_gmlp_operands(w_s, b_s, blk_p), *_gmlp_operands(w_s, b_s, blk_s),
      g_out, w_out, w_out)


def _ffn_kernel(h_ref, g_ref, wg_ref, wu_ref, wo_ref, o_ref, hb_scr, r_scr):
    j = pl.program_id(1)

    def tile(hb, r):
        gate = r * _dot(hb, wg_ref[...])
        up = r * _dot(hb, wu_ref[...])
        return _dot((jax.nn.silu(gate) * up).astype(BF16), wo_ref[...])

    @pl.when(j == 0)
    def _():
        h = h_ref[...]
        hb = (h * g_ref[...]).astype(BF16)
        hb_scr[...] = hb
        r = lax.rsqrt(jnp.mean(h * h, axis=-1, keepdims=True) + EPS)
        r_scr[...] = r
        o_ref[...] = h + tile(hb, r)

    @pl.when(j > 0)
    def _():
        o_ref[...] += tile(hb_scr[...], r_scr[...])


def _ffn(h1, g_ffn, w_gate, w_up, w_out, tm, tf):
    m = h1.shape[0]
    return pl.pallas_call(
        _ffn_kernel,
        grid=(m // tm, D_FF // tf),
        in_specs=[
            pl.BlockSpec((tm, D_MODEL), lambda i, j: (i, 0)),
            pl.BlockSpec((1, D_MODEL), lambda i, j: (0, 0)),
            pl.BlockSpec((D_MODEL, tf), lambda i, j: (0, j)),
            pl.BlockSpec((D_MODEL, tf), lambda i, j: (0, j)),
            pl.BlockSpec((tf, D_MODEL), lambda i, j: (j, 0)),
        ],
        out_specs=pl.BlockSpec((tm, D_MODEL), lambda i, j: (i, 0)),
        out_shape=jax.ShapeDtypeStruct((m, D_MODEL), F32),
        scratch_shapes=[pltpu.VMEM((tm, D_MODEL), BF16), pltpu.VMEM((tm, 1), F32)],
        compiler_params=_params("parallel", "arbitrary"),
        name="ffn",
    )(h1, g_ffn, w_gate, w_up, w_out)


def _ple_tile(h_ref, p_ref, wp_ref, gp_ref, gg_ref, wg_ref, gf_ref, y_ref):
    h = h_ref[...]
    r = lax.rsqrt(jnp.mean(h * h, axis=-1, keepdims=True) + EPS)
    hb = (h * gg_ref[...]).astype(BF16)
    pe = _rms(_dot(p_ref[...].astype(BF16), wp_ref[...]), gp_ref[...])
    ssq = jnp.zeros((h.shape[0], 1), F32)
    for c in range(D_MODEL // PLE_COL_CHUNK):
        cols = slice(c * PLE_COL_CHUNK, (c + 1) * PLE_COL_CHUNK)
        t = h[:, cols] + pe[:, cols] * jax.nn.sigmoid(r * _dot(hb, wg_ref[:, cols]))
        ssq = ssq + jnp.sum(t * t, axis=-1, keepdims=True)
        y_ref[:, cols] = t
    y_ref[...] = y_ref[...] * lax.rsqrt(ssq / D_MODEL + EPS) * gf_ref[...]


def _ple_kernel(h_ref, pp_ref, ps_ref, wp_ref, gp_ref, gg_ref, wg_ref, gf_ref, yp_ref, ys_ref, *, n_p):
    params = (wp_ref, gp_ref, gg_ref, wg_ref, gf_ref)
    i = pl.program_id(0)

    @pl.when(i < n_p)
    def _():
        _ple_tile(h_ref, pp_ref, *params, yp_ref)

    @pl.when(i >= n_p)
    def _():
        _ple_tile(h_ref, ps_ref, *params, ys_ref)


def _ple(h, pp, ps, w_ple, g_ple, g_gate, w_gate, g_final, tm):
    n_p, n_s = pp.shape[0] // tm, ps.shape[0] // tm
    pidx = lambda i: (jnp.minimum(i, n_p - 1), 0)
    sidx = lambda i: (jnp.maximum(i - n_p, 0), 0)
    vec = pl.BlockSpec((1, D_MODEL), lambda i: (0, 0))
    return pl.pallas_call(
        functools.partial(_ple_kernel, n_p=n_p),
        grid=(n_p + n_s,),
        in_specs=[
            pl.BlockSpec((tm, D_MODEL), lambda i: (i, 0)),
            pl.BlockSpec((tm, PLE_DIM), pidx),
            pl.BlockSpec((tm, PLE_DIM), sidx),
            pl.BlockSpec((PLE_DIM, D_MODEL), lambda i: (0, 0)),
            vec, vec,
            pl.BlockSpec((D_MODEL, D_MODEL), lambda i: (0, 0), pipeline_mode=pl.Buffered(1)),
            vec,
        ],
        out_specs=[pl.BlockSpec((tm, D_MODEL), pidx), pl.BlockSpec((tm, D_MODEL), sidx)],
        out_shape=[jax.ShapeDtypeStruct((pp.shape[0], D_MODEL), F32),
                   jax.ShapeDtypeStruct((ps.shape[0], D_MODEL), F32)],
        compiler_params=_params("arbitrary"),
        name="ple",
    )(h, pp, ps, w_ple, g_ple, g_gate, w_gate, g_final)


def kernel(x_prompt, x_sample, state_gla, p_prompt, p_sample, g_mix, w_in, w_a2, b_a, g_gla_norm, g_gmlp_ln,
           b_gmlp_ln, w_s, b_s, g_gmlp_out, w_out, g_ffn, w_ffn_in, w_ffn_out, w_ple, g_ple, g_ple_gate,
           w_ple_gate, g_final):
    batch, seq, _ = x_prompt.shape
    dec_batch, dec_seq, _ = x_sample.shape
    depth = w_in.shape[0]
    assert depth == 1, "one layer: the prompt / sample passes below are not chained over depth"
    row = lambda a: a.reshape(1, -1)
    wts = dict(
        g_mix=row(g_mix[0]),
        wa2_pad=jnp.pad(w_a2[0], ((0, V7X_LANES - GLA_GATE_RANK), (0, 0))),
        b_a=row(b_a[0]),
        g_gla_norm=row(g_gla_norm[0]),
        g_gmlp_ln=row(g_gmlp_ln[0]),
        b_gmlp_ln=row(b_gmlp_ln[0]),
        w_s=w_s[0],
        b_s=b_s[0],
        g_gmlp_out=row(g_gmlp_out[0]),
        g_ffn=row(g_ffn[0]),
        w_ple=w_ple[0].astype(BF16),
        g_ple=row(g_ple[0]),
        g_ple_gate=row(g_ple_gate[0]),
        g_final=row(g_final),
    )
    xp = x_prompt.reshape(batch * seq, D_MODEL)
    xs = x_sample.reshape(dec_batch * dec_seq, D_MODEL)
    wt_in = jnp.swapaxes(w_in, 1, 2)[0]
    zs, alow_s, wt16, wl16 = _inproj(xs, wts["g_mix"], wt_in, wt_in, tm=xs.shape[0], tn=INPROJ_COLS, emit=True)
    zp, alow_p, wts["w_out"], wts["w_ffn_out"], w_gate16 = _inproj(
        xp, wts["g_mix"], wt16, wl16, tm=INPROJ_ROWS, tn=INPROJ_COLS, emit=False,
        cast=(w_out[0], w_ffn_out[0], (w_ffn_in[0], 0, 2)))
    op, sp, os_, ss, w_up16, wts["w_ple_gate"] = _gla(
        zp, alow_p, zs, alow_s, state_gla[0], wts["wa2_pad"], wts["b_a"], wts["g_gla_norm"],
        n_seq_p=batch, len_p=seq, n_seq_s=dec_batch, len_s=dec_seq, cast=((w_ffn_in[0], 1, 2), w_ple_gate[0]))
    h1, vs = _mixout(xp, xs, op, os_, zp, zs, wts["g_gmlp_ln"], wts["b_gmlp_ln"], wts["w_s"], wts["b_s"],
                     wts["g_gmlp_out"], wts["w_out"], tm=MIX_ROWS, len_p=seq, len_s=dec_seq)
    h2 = _ffn(h1, wts["g_ffn"], w_gate16, w_up16, wts["w_ffn_out"], tm=FFN_ROWS, tf=FFN_COLS)
    yp, ys = _ple(h2, p_prompt[0].reshape(batch * seq, PLE_DIM), p_sample[0].reshape(dec_batch * dec_seq, PLE_DIM),
                  wts["w_ple"], wts["g_ple"], wts["g_ple_gate"], wts["w_ple_gate"], wts["g_final"], tm=PLE_ROWS)
    return (yp.reshape(batch, seq, D_MODEL), ys.reshape(dec_batch, dec_seq, D_MODEL), sp[None], ss[None],
            vs.reshape(1, dec_batch, dec_seq, GMLP_WIDTH))
```

```python
import functools

import numpy as np

import jax
import jax.numpy as jnp
from jax import lax
from jax.experimental import pallas as pl
from jax.experimental.pallas import tpu as pltpu

F32 = jnp.float32
BF16 = jnp.bfloat16

D_MODEL = 2048
GLA_WIDTH = 1024
GMLP_WIDTH = 1024
GLA_HEADS = 4
GLA_DV = 256
GLA_DK = 128
GLA_KEY_WIDTH = 512
GLA_GATE_RANK = 16
GLA_TAU = 16.0
GLA_CHUNK = 64
GMLP_GROUPS = 8
GMLP_GROUP_DIM = 128
GMLP_CHUNK = 128
D_FF = 5632
PLE_DIM = 256
EPS = 1e-6
IN_OFFSETS = (0, 512, 1024, 2048, 3072, 3088, 4112, 5136)
Z_WIDTH = 5120
Z_GLA_WIDTH = 3072
Z_U_OFF = 3072
Z_VG_OFF = 4096

V7X_LANES = 128
V7X_BF16_ROWS = 16
V7X_VMEM_BYTES = 64 * 1024 * 1024
VMEM_LIMIT_BYTES = V7X_VMEM_BYTES - 4 * 1024 * 1024
INPROJ_ROWS, INPROJ_COLS = 1024, 1024
MIX_ROWS = 512
FFN_ROWS, FFN_COLS = 1024, 512
PLE_ROWS, PLE_COL_CHUNK = 512, 1024
GLA_SEQ_ROWS = 4 * GLA_CHUNK
N_CAST_BLOCKS = 32
W_RING_SLOTS = 3


def _params(*sem):
    return pltpu.CompilerParams(dimension_semantics=sem, vmem_limit_bytes=VMEM_LIMIT_BYTES)


def _rms(x, g):
    return x * lax.rsqrt(jnp.mean(x * x, axis=-1, keepdims=True) + EPS) * g


def _dot(a, b):
    return jnp.dot(a, b, preferred_element_type=F32)


def _dot_tn(a, b):
    return lax.dot_general(a, b, (((0,), (0,)), ((), ())), preferred_element_type=F32)


def _dot_nt(a, b):
    return lax.dot_general(a, b, (((1,), (1,)), ((), ())), preferred_element_type=F32)


def _cast_specs(cast, n_blocks, step_of):
    in_specs, out_specs, out_shape = [], [], []
    for item in cast:
        w, k, n = item if isinstance(item, tuple) else (item, 0, 1)
        rows, cols = w.shape[0] // n_blocks, w.shape[1] // n
        assert rows * n_blocks == w.shape[0] and rows % V7X_BF16_ROWS == 0 and cols % V7X_LANES == 0
        step = lambda *ids: jnp.minimum(step_of(*ids), n_blocks - 1)
        in_specs.append(pl.BlockSpec((rows, cols), lambda *ids, step=step, k=k: (step(*ids), k)))
        out_specs.append(pl.BlockSpec((rows, cols), lambda *ids, step=step: (step(*ids), 0)))
        out_shape.append(jax.ShapeDtypeStruct((w.shape[0], cols), BF16))
    return in_specs, out_specs, out_shape


def _cast_args(cast):
    return [item[0] if isinstance(item, tuple) else item for item in cast]


def _cast_blocks(cast_in, cast_out):
    for src_ref, dst_ref in zip(cast_in, cast_out):
        dst_ref[...] = src_ref[...].astype(BF16)


def _inproj_kernel(*refs, emit, n_cast):
    x_ref, g_ref, w_ref, wl_ref = refs[:4]
    cast_in = refs[4:4 + n_cast]
    n_out = 4 if emit else 2
    z_ref, alow_ref = refs[4 + n_cast:6 + n_cast]
    emit_refs = refs[6 + n_cast:4 + n_cast + n_out]
    cast_out = refs[4 + n_cast + n_out:4 + 2 * n_cast + n_out]
    xb_scr, r_scr = refs[4 + 2 * n_cast + n_out:6 + 2 * n_cast + n_out]
    j = pl.program_id(1)

    if not emit:
        w_ring, w_sem, x_buf, x_sem = refs[6 + 2 * n_cast + n_out:]
        tn = w_ring.shape[1]
        n_j = pl.num_programs(1)
        n_steps = pl.num_programs(0) * n_j
        step = pl.program_id(0) * n_j + j

        def tile_copy(s):
            slot = lax.rem(s, W_RING_SLOTS)
            row = pl.multiple_of(lax.rem(s, n_j) * tn, tn)
            return pltpu.make_async_copy(w_ref.at[pl.ds(row, tn), :], w_ring.at[slot], w_sem.at[slot])

        @pl.when(step == 0)
        def _():
            for s in range(W_RING_SLOTS - 1):
                tile_copy(s).start()

        @pl.when(step + (W_RING_SLOTS - 1) < n_steps)
        def _():
            tile_copy(step + (W_RING_SLOTS - 1)).start()

        tile_copy(step).wait()
        w_slot = lax.rem(step, W_RING_SLOTS)

        tm = x_buf.shape[1]
        i = pl.program_id(0)

        def rows_copy(t):
            slot = lax.rem(t, 2)
            row = pl.multiple_of(t * tm, tm)
            return pltpu.make_async_copy(x_ref.at[pl.ds(row, tm), :], x_buf.at[slot], x_sem.at[slot])

        @pl.when(step == 0)
        def _():
            rows_copy(0).start()

        @pl.when(jnp.logical_and(j == 1, i + 1 < pl.num_programs(0)))
        def _():
            rows_copy(i + 1).start()

        @pl.when(j == 0)
        def _():
            rows_copy(i).wait()

        x_slot = lax.rem(i, 2)

    def weight(ref, emit_ref):
        if not emit:
            return ref[...]
        w16 = ref[...].astype(BF16)
        emit_ref[...] = w16
        return w16

    def z_tile(xb, r):
        w16 = weight(w_ref, emit_refs[0]) if emit else w_ring[w_slot]
        return r * _dot_nt(xb, w16)

    @pl.when(j == 0)
    def _():
        _cast_blocks(cast_in, cast_out)
        x = x_ref[...] if emit else x_buf[x_slot]
        xb = (x * g_ref[...]).astype(BF16)
        xb_scr[...] = xb
        r = lax.rsqrt(jnp.mean(x * x, axis=-1, keepdims=True) + EPS)
        r_scr[...] = r
        alow_ref[...] = r * _dot_nt(xb, weight(wl_ref, emit_refs[1] if emit else None))
        z_ref[...] = z_tile(xb, r).astype(BF16)

    @pl.when(j > 0)
    def _():
        _cast_blocks(cast_in, cast_out)
        z_ref[...] = z_tile(xb_scr[...], r_scr[...]).astype(BF16)


def _inproj(x, g_mix, w, wl, *, tm, tn, emit, cast=()):
    m = x.shape[0]
    n_a = Z_GLA_WIDTH // tn
    n_j = Z_WIDTH // tn
    if emit:
        def w_row(j):
            return pl.multiple_of(jnp.where(j < n_a, j * tn, IN_OFFSETS[5] + (j - n_a) * tn), V7X_BF16_ROWS)
        w_spec = pl.BlockSpec((pl.Element(tn), pl.Element(D_MODEL)), lambda i, j: (w_row(j), 0))
        wl_spec = pl.BlockSpec((V7X_LANES, D_MODEL), lambda i, j: (IN_OFFSETS[4] // V7X_LANES, 0))
        assert m == tm, "every weight tile must be visited exactly once"
    else:
        w_spec = pl.BlockSpec(memory_space=pl.ANY)
        wl_spec = pl.BlockSpec((V7X_LANES, D_MODEL), lambda i, j: (0, 0))
        assert (m // tm) * n_j >= W_RING_SLOTS - 1
    cast_in, cast_out, cast_shape = _cast_specs(cast, N_CAST_BLOCKS, lambda i, j: i * n_j + j)
    assert not cast or (m // tm) * n_j >= N_CAST_BLOCKS
    out_specs = [
        pl.BlockSpec((tm, tn), lambda i, j: (i, j)),
        pl.BlockSpec((tm, V7X_LANES), lambda i, j: (i, 0)),
    ]
    out_shape = [
        jax.ShapeDtypeStruct((m, Z_WIDTH), BF16),
        jax.ShapeDtypeStruct((m, V7X_LANES), F32),
    ]
    if emit:
        out_specs += [pl.BlockSpec((tn, D_MODEL), lambda i, j: (j, 0)),
                      pl.BlockSpec((V7X_LANES, D_MODEL), lambda i, j: (0, 0))]
        out_shape += [jax.ShapeDtypeStruct((Z_WIDTH, D_MODEL), BF16),
                      jax.ShapeDtypeStruct((V7X_LANES, D_MODEL), BF16)]
    x_spec = pl.BlockSpec((tm, D_MODEL), lambda i, j: (i, 0)) if emit else pl.BlockSpec(memory_space=pl.ANY)
    ring = [] if emit else [pltpu.VMEM((W_RING_SLOTS, tn, D_MODEL), BF16), pltpu.SemaphoreType.DMA((W_RING_SLOTS,)),
                            pltpu.VMEM((2, tm, D_MODEL), F32), pltpu.SemaphoreType.DMA((2,))]
    return pl.pallas_call(
        functools.partial(_inproj_kernel, emit=emit, n_cast=len(cast)),
        grid=(m // tm, n_j),
        in_specs=[
            x_spec,
            pl.BlockSpec((1, D_MODEL), lambda i, j: (0, 0)),
            w_spec,
            wl_spec,
        ] + cast_in,
        out_specs=out_specs + cast_out,
        out_shape=out_shape + cast_shape,
        scratch_shapes=[pltpu.VMEM((tm, D_MODEL), BF16), pltpu.VMEM((tm, 1), F32)] + ring,
        compiler_params=_params("parallel" if emit else "arbitrary", "arbitrary"),
        name="inproj",
    )(x, g_mix, w, wl, *_cast_args(cast))


def _group_mask(R, C):
    t = np.arange(R)[:, None]
    s = np.arange(R)[None, :]
    return ((t // C == s // C) & (s <= t)).astype(np.float32)


def _gla_tile(q_ref, k_ref, v_ref, r_ref, alow_ref, tri_ref, mask_ref, wa2_ref, ba_ref, gn_ref,
              o_ref, s_out_ref, b_scr, e_scr, *, C, s_scr=None, s0_ref=None):
    sequential = s_scr is not None
    R = q_ref.shape[0]
    n = R // C
    mid = C // 2
    W = GLA_KEY_WIDTH

    x = _dot(alow_ref[...].astype(BF16), wa2_ref[...].astype(BF16)) + ba_ref[...]
    lb = (jnp.minimum(x, 0.0) - jnp.log(1.0 + jnp.exp(-jnp.abs(x)))) * (1.0 / GLA_TAU)

    tri = tri_ref[...]
    lb_hi = lb.astype(BF16)
    lb_lo = (lb - lb_hi.astype(F32)).astype(BF16)
    b = _dot(tri, lb_hi) + _dot(tri, lb_lo)
    b3 = b.reshape(n, C, W)
    b_mid = jnp.broadcast_to(b3[:, mid:mid + 1, :], (n, C, W)).reshape(R, W)
    b_last = jnp.broadcast_to(b3[:, C - 1:C, :], (n, C, W)).reshape(R, W)

    q = q_ref[...].astype(F32) * (GLA_DK ** -0.5)
    k = k_ref[...].astype(F32)
    q1 = (q * jnp.exp(b)).astype(BF16)
    q2 = (q * jnp.exp(b - b_mid)).astype(BF16)
    k2 = (k * jnp.exp(b_mid - b)).astype(BF16)
    k3 = k * jnp.exp(b_last - b)

    e_scr[...] = jnp.zeros((V7X_LANES, GLA_DK), F32)
    for h in range(GLA_HEADS):
        b_scr[h] = b[:, h * GLA_DK:(h + 1) * GLA_DK]
        e_scr[h * n:(h + 1) * n, :] = jnp.exp(b_scr[h, pl.ds(C - 1, n, stride=C), :])
    d_cols = e_scr[...].T

    causal = mask_ref[...] > 0.0
    if not sequential:
        grp = lax.broadcasted_iota(jnp.int32, (R, 1), 0) >> (C.bit_length() - 1)

    for h in range(GLA_HEADS):
        ks = slice(h * GLA_DK, (h + 1) * GLA_DK)
        vs = slice(h * GLA_DV, (h + 1) * GLA_DV)
        v = v_ref[:, vs]
        att = jnp.where(causal, _dot_nt(q2[:, ks], k2[:, ks]), 0.0).astype(BF16)
        o = _dot(att, v)
        q1h = q1[:, ks]
        k3h = k3[:, ks]
        if sequential:
            s = s_scr[h]
            o_parts = []
            for g in range(n):
                rows = slice(g * C, (g + 1) * C)
                o_parts.append(o[rows] + _dot(q1h[rows], s.astype(BF16)))
                s = d_cols[:, h * n + g:h * n + g + 1] * s + _dot_tn(k3h[rows].astype(BF16), v[rows])
            s_scr[h] = s
            s_out_ref[0, h] = s
            o = jnp.concatenate(o_parts, axis=0)
        else:
            for g in range(n):
                s = s0_ref[g, h]
                o = o + jnp.where(grp == g, _dot(q1h, s.astype(BF16)), 0.0)
                k3g = jnp.where(grp == g, k3h, 0.0).astype(BF16)
                s_out_ref[g, h] = d_cols[:, h * n + g:h * n + g + 1] * s + _dot_tn(k3g, v)
        o = _rms(o, gn_ref[:, vs]) * jax.nn.silu(r_ref[:, vs].astype(F32))
        o_ref[:, vs] = o.astype(BF16)


N_GLA_TILE_IN = 7


def _gla_kernel(*refs, C_seq, C_step, n_cast):
    n = N_GLA_TILE_IN
    seq_in, step_in = refs[:n], refs[n:2 * n]
    s0_ref, wa2_ref, ba_ref, gn_ref = refs[2 * n:2 * n + 4]
    cast_in = refs[2 * n + 4:2 * n + 4 + n_cast]
    outs = refs[2 * n + 4 + n_cast:]
    o_seq, s_seq, o_step, s_step = outs[:4]
    cast_out = outs[4:4 + n_cast]
    s_scr, b_seq, e_seq, b_step, e_step = outs[4 + n_cast:]
    shared = (wa2_ref, ba_ref, gn_ref)

    @pl.when(pl.program_id(1) == 0)
    def _():
        s_scr[...] = jnp.zeros((GLA_HEADS, GLA_DK, GLA_DV), F32)

    _cast_blocks(cast_in, cast_out)
    _gla_tile(*seq_in, *shared, o_seq, s_seq, b_seq, e_seq, C=C_seq, s_scr=s_scr)
    _gla_tile(*step_in, *shared, o_step, s_step, b_step, e_step, C=C_step, s0_ref=s0_ref)


def _gla(zp, alow_p, zs, alow_s, s0, wa2_pad, b_a, g_norm, *, n_seq_p, len_p, n_seq_s, len_s, cast=()):
    R_p = GLA_SEQ_ROWS
    nt = len_p // R_p
    n_steps = n_seq_p * nt
    R_s = zs.shape[0] // n_steps
    assert R_s * n_steps == zs.shape[0] and R_s % len_s == 0 and R_s % V7X_BF16_ROWS == 0
    assert not cast or n_steps >= N_CAST_BLOCKS
    step_of = lambda a, c: a * nt + c
    const = lambda a, c: (0, 0)

    def tile_specs(R):
        return [
            pl.BlockSpec((R, GLA_KEY_WIDTH), lambda a, c: (step_of(a, c), 0)),
            pl.BlockSpec((R, GLA_KEY_WIDTH), lambda a, c: (step_of(a, c), 1)),
            pl.BlockSpec((R, GLA_WIDTH), lambda a, c: (step_of(a, c), 1)),
            pl.BlockSpec((R, GLA_WIDTH), lambda a, c: (step_of(a, c), 2)),
            pl.BlockSpec((R, V7X_LANES), lambda a, c: (step_of(a, c), 0)),
            pl.BlockSpec((R, R), const),
            pl.BlockSpec((R, R), const),
        ]

    def tile_args(z, alow, R, C):
        mask = _group_mask(R, C)
        return [z, z, z, z, alow, jnp.asarray(mask, BF16), jnp.asarray(mask, F32)]

    s_block_p = (1, GLA_HEADS, GLA_DK, GLA_DV)
    s_block_s = (R_s // len_s, GLA_HEADS, GLA_DK, GLA_DV)
    state_p = pl.BlockSpec(s_block_p, lambda a, c: (a, 0, 0, 0))
    state_s = pl.BlockSpec(s_block_s, lambda a, c: (step_of(a, c), 0, 0, 0))
    cast_in, cast_out, cast_shape = _cast_specs(cast, N_CAST_BLOCKS, step_of)
    scratch = [pltpu.VMEM((GLA_HEADS, GLA_DK, GLA_DV), F32)]
    for R in (R_p, R_s):
        scratch += [pltpu.VMEM((GLA_HEADS, R, GLA_DK), F32), pltpu.VMEM((V7X_LANES, GLA_DK), F32)]
    return pl.pallas_call(
        functools.partial(_gla_kernel, C_seq=GLA_CHUNK, C_step=len_s, n_cast=len(cast)),
        grid=(n_seq_p, nt),
        in_specs=tile_specs(R_p) + tile_specs(R_s) + [
            state_s,
            pl.BlockSpec((V7X_LANES, GLA_KEY_WIDTH), const),
            pl.BlockSpec((1, GLA_KEY_WIDTH), const),
            pl.BlockSpec((1, GLA_WIDTH), const),
        ] + cast_in,
        out_specs=[
            pl.BlockSpec((R_p, GLA_WIDTH), lambda a, c: (step_of(a, c), 0)),
            state_p,
            pl.BlockSpec((R_s, GLA_WIDTH), lambda a, c: (step_of(a, c), 0)),
            state_s,
        ] + cast_out,
        out_shape=[
            jax.ShapeDtypeStruct((zp.shape[0], GLA_WIDTH), BF16),
            jax.ShapeDtypeStruct((n_seq_p, GLA_HEADS, GLA_DK, GLA_DV), F32),
            jax.ShapeDtypeStruct((zs.shape[0], GLA_WIDTH), BF16),
            jax.ShapeDtypeStruct((n_seq_s, GLA_HEADS, GLA_DK, GLA_DV), F32),
        ] + cast_shape,
        scratch_shapes=scratch,
        compiler_params=_params("parallel", "arbitrary"),
        name="gla",
    )(*tile_args(zp, alow_p, R_p, GLA_CHUNK), *tile_args(zs, alow_s, R_s, len_s), s0, wa2_pad, b_a, g_norm,
      *_cast_args(cast))


def _mixout_tile(h_ref, o_ref, u_ref, vg_ref, gln_ref, bln_ref, ws_ref, bias_ref, gout_ref, wo_ref, wm_ref,
                 h1_ref, vn_ref, *, tm, blk):
    n_chunks = tm // GMLP_CHUNK
    col_w = D_MODEL // n_chunks
    shift = blk.bit_length() - 1
    row = lax.broadcasted_iota(jnp.int32, (GMLP_CHUNK, GMLP_CHUNK), 0)
    col = lax.broadcasted_iota(jnp.int32, (GMLP_CHUNK, GMLP_CHUNK), 1)
    causal = jnp.logical_and((row >> shift) == (col >> shift), col <= row)
    w = [jnp.where(causal, ws_ref[g], 0.0).astype(BF16) for g in range(GMLP_GROUPS)]
    m_parts = []
    for c in range(n_chunks):
        cols = slice(c * col_w, (c + 1) * col_w)
        h1_ref[:, cols] = h_ref[:, cols] + _dot(o_ref[...], wo_ref[:, cols])
        rows = pl.ds(c * GMLP_CHUNK, GMLP_CHUNK)
        vg = jax.nn.gelu(vg_ref[rows, :].astype(F32))
        mu = jnp.mean(vg, axis=-1, keepdims=True)
        vc = vg - mu
        vn = vc * lax.rsqrt(jnp.mean(vc * vc, axis=-1, keepdims=True) + EPS) * gln_ref[...] + bln_ref[...]
        if vn_ref is not None:
            vn_ref[rows, :] = vn
        vn16 = vn.astype(BF16)
        mixed = jnp.concatenate(
            [_dot(w[g], vn16[:, g * GMLP_GROUP_DIM:(g + 1) * GMLP_GROUP_DIM]) for g in range(GMLP_GROUPS)],
            axis=1) + bias_ref[...]
        mm = jax.nn.gelu(u_ref[rows, :].astype(F32)) * mixed
        m_parts.append(_rms(mm, gout_ref[...]).astype(BF16))
    h1_ref[...] += _dot(jnp.concatenate(m_parts, axis=0), wm_ref[...])


def _mixout_kernel(hp_ref, hs_ref, op_ref, os_ref, up_ref, vgp_ref, us_ref, vgs_ref, gln_ref, bln_ref,
                   wsp_ref, biasp_ref, wss_ref, biass_ref, gout_ref, wo_ref, wm_ref, h1_ref, vn_ref,
                   *, tm, n_p, blk_p, blk_s):
    shared = (gln_ref, bln_ref)
    tail = (gout_ref, wo_ref, wm_ref, h1_ref)
    i = pl.program_id(0)

    @pl.when(i < n_p)
    def _():
        _mixout_tile(hp_ref, op_ref, up_ref, vgp_ref, *shared, wsp_ref, biasp_ref, *tail, None, tm=tm, blk=blk_p)

    @pl.when(i >= n_p)
    def _():
        _mixout_tile(hs_ref, os_ref, us_ref, vgs_ref, *shared, wss_ref, biass_ref, *tail, vn_ref, tm=tm, blk=blk_s)


def _gmlp_operands(w_s, b_s, blk):
    reps = GMLP_CHUNK // blk
    sel = jnp.asarray(np.tile(np.eye(blk, dtype=np.float32), (reps, 1)))
    w_mix = jnp.einsum("ra,gab,cb->grc", sel, w_s[:, :blk, :blk], sel, precision=lax.Precision.HIGHEST)
    bias = jnp.tile(jnp.repeat(b_s[:, :blk].T, GMLP_GROUP_DIM, axis=1), (reps, 1))
    return w_mix, bias


def _mixout(hp, hs, op, os_, zp, zs, g_ln, b_ln, w_s, b_s, g_out, w_out, *, tm, len_p, len_s):
    n_p, n_s = hp.shape[0] // tm, hs.shape[0] // tm
    blk_p, blk_s = min(len_p, GMLP_CHUNK), min(len_s, GMLP_CHUNK)
    pidx = lambda i: jnp.minimum(i, n_p - 1)
    sidx = lambda i: jnp.maximum(i - n_p, 0)
    once = pl.Buffered(1)
    vec = pl.BlockSpec((1, GMLP_WIDTH), lambda i: (0, 0))
    mixw = pl.BlockSpec((GMLP_GROUPS, GMLP_CHUNK, GMLP_CHUNK), lambda i: (0, 0, 0))
    biasb = pl.BlockSpec((GMLP_CHUNK, GMLP_WIDTH), lambda i: (0, 0))
    return pl.pallas_call(
        functools.partial(_mixout_kernel, tm=tm, n_p=n_p, blk_p=blk_p, blk_s=blk_s),
        grid=(n_p + n_s,),
        in_specs=[
            pl.BlockSpec((tm, D_MODEL), lambda i: (pidx(i), 0)),
            pl.BlockSpec((tm, D_MODEL), lambda i: (sidx(i), 0)),
            pl.BlockSpec((tm, GLA_WIDTH), lambda i: (pidx(i), 0)),
            pl.BlockSpec((tm, GLA_WIDTH), lambda i: (sidx(i), 0)),
            pl.BlockSpec((tm, GMLP_WIDTH), lambda i: (pidx(i), Z_U_OFF // GMLP_WIDTH)),
            pl.BlockSpec((tm, GMLP_WIDTH), lambda i: (pidx(i), Z_VG_OFF // GMLP_WIDTH)),
            pl.BlockSpec((tm, GMLP_WIDTH), lambda i: (sidx(i), Z_U_OFF // GMLP_WIDTH)),
            pl.BlockSpec((tm, GMLP_WIDTH), lambda i: (sidx(i), Z_VG_OFF // GMLP_WIDTH)),
            vec, vec,
            mixw, biasb, mixw, biasb,
            vec,
            pl.BlockSpec((GLA_WIDTH, D_MODEL), lambda i: (0, 0), pipeline_mode=once),
            pl.BlockSpec((GMLP_WIDTH, D_MODEL), lambda i: (1, 0), pipeline_mode=once),
        ],
        out_specs=[
            pl.BlockSpec((tm, D_MODEL), lambda i: (i, 0)),
            pl.BlockSpec((tm, GMLP_WIDTH), lambda i: (sidx(i), 0)),
        ],
        out_shape=[
            jax.ShapeDtypeStruct((hp.shape[0] + hs.shape[0], D_MODEL), F32),
            jax.ShapeDtypeStruct((hs.shape[0], GMLP_WIDTH), F32),
        ],
        compiler_params=_params("arbitrary"),
        name="mixout",
    )(hp, hs, op, os_, zp, zp, zs, zs, g_ln, b_ln, *_gmlp_operands(w_s, b_s, blk_p), *_gmlp_operands(w_s, b_s, blk_s),
      g_out, w_out, w_out)


def _ffn_kernel(h_ref, g_ref, wg_ref, wu_ref, wo_ref, o_ref, hb_scr, r_scr):
    j = pl.program_id(1)

    def tile(hb, r):
        gate = r * _dot(hb, wg_ref[...])
        up = r * _dot(hb, wu_ref[...])
        return _dot((jax.nn.silu(gate) * up).astype(BF16), wo_ref[...])

    @pl.when(j == 0)
    def _():
        h = h_ref[...]
        hb = (h * g_ref[...]).astype(BF16)
        hb_scr[...] = hb
        r = lax.rsqrt(jnp.mean(h * h, axis=-1, keepdims=True) + EPS)
        r_scr[...] = r
        o_ref[...] = h + tile(hb, r)

    @pl.when(j > 0)
    def _():
        o_ref[...] += tile(hb_scr[...], r_scr[...])


def _ffn(h1, g_ffn, w_gate, w_up, w_out, tm, tf):
    m = h1.shape[0]
    return pl.pallas_call(
        _ffn_kernel,
        grid=(m // tm, D_FF // tf),
        in_specs=[
            pl.BlockSpec((tm, D_MODEL), lambda i, j: (i, 0)),
            pl.BlockSpec((1, D_MODEL), lambda i, j: (0, 0)),
            pl.BlockSpec((D_MODEL, tf), lambda i, j: (0, j)),
            pl.BlockSpec((D_MODEL, tf), lambda i, j: (0, j)),
            pl.BlockSpec((tf, D_MODEL), lambda i, j: (j, 0)),
        ],
        out_specs=pl.BlockSpec((tm, D_MODEL), lambda i, j: (i, 0)),
        out_shape=jax.ShapeDtypeStruct((m, D_MODEL), F32),
        scratch_shapes=[pltpu.VMEM((tm, D_MODEL), BF16), pltpu.VMEM((tm, 1), F32)],
        compiler_params=_params("parallel", "arbitrary"),
        name="ffn",
    )(h1, g_ffn, w_gate, w_up, w_out)


def _ple_tile(h_ref, p_ref, wp_ref, gp_ref, gg_ref, wg_ref, gf_ref, y_ref):
    h = h_ref[...]
    r = lax.rsqrt(jnp.mean(h * h, axis=-1, keepdims=True) + EPS)
    hb = (h * gg_ref[...]).astype(BF16)
    pe = _rms(_dot(p_ref[...].astype(BF16), wp_ref[...]), gp_ref[...])
    ssq = jnp.zeros((h.shape[0], 1), F32)
    for c in range(D_MODEL // PLE_COL_CHUNK):
        cols = slice(c * PLE_COL_CHUNK, (c + 1) * PLE_COL_CHUNK)
        t = h[:, cols] + pe[:, cols] * jax.nn.sigmoid(r * _dot(hb, wg_ref[:, cols]))
        ssq = ssq + jnp.sum(t * t, axis=-1, keepdims=True)
        y_ref[:, cols] = t
    y_ref[...] = y_ref[...] * lax.rsqrt(ssq / D_MODEL + EPS) * gf_ref[...]


def _ple_kernel(h_ref, pp_ref, ps_ref, wp_ref, gp_ref, gg_ref, wg_ref, gf_ref, yp_ref, ys_ref, *, n_p):
    params = (wp_ref, gp_ref, gg_ref, wg_ref, gf_ref)
    i = pl.program_id(0)

    @pl.when(i < n_p)
    def _():
        _ple_tile(h_ref, pp_ref, *params, yp_ref)

    @pl.when(i >= n_p)
    def _():
        _ple_tile(h_ref, ps_ref, *params, ys_ref)


def _ple(h, pp, ps, w_ple, g_ple, g_gate, w_gate, g_final, tm):
    n_p, n_s = pp.shape[0] // tm, ps.shape[0] // tm
    pidx = lambda i: (jnp.minimum(i, n_p - 1), 0)
    sidx = lambda i: (jnp.maximum(i - n_p, 0), 0)
    vec = pl.BlockSpec((1, D_MODEL), lambda i: (0, 0))
    return pl.pallas_call(
        functools.partial(_ple_kernel, n_p=n_p),
        grid=(n_p + n_s,),
        in_specs=[
            pl.BlockSpec((tm, D_MODEL), lambda i: (i, 0)),
            pl.BlockSpec((tm, PLE_DIM), pidx),
            pl.BlockSpec((tm, PLE_DIM), sidx),
            pl.BlockSpec((PLE_DIM, D_MODEL), lambda i: (0, 0)),
            vec, vec,
            pl.BlockSpec((D_MODEL, D_MODEL), lambda i: (0, 0), pipeline_mode=pl.Buffered(1)),
            vec,
        ],
        out_specs=[pl.BlockSpec((tm, D_MODEL), pidx), pl.BlockSpec((tm, D_MODEL), sidx)],
        out_shape=[jax.ShapeDtypeStruct((pp.shape[0], D_MODEL), F32),
                   jax.ShapeDtypeStruct((ps.shape[0], D_MODEL), F32)],
        compiler_params=_params("arbitrary"),
        name="ple",
    )(h, pp, ps, w_ple, g_ple, g_gate, w_gate, g_final)


def kernel(x_prompt, x_sample, state_gla, p_prompt, p_sample, g_mix, w_in, w_a2, b_a, g_gla_norm, g_gmlp_ln,
           b_gmlp_ln, w_s, b_s, g_gmlp_out, w_out, g_ffn, w_ffn_in, w_ffn_out, w_ple, g_ple, g_ple_gate,
           w_ple_gate, g_final):
    batch, seq, _ = x_prompt.shape
    dec_batch, dec_seq, _ = x_sample.shape
    depth = w_in.shape[0]
    assert depth == 1, "one layer: the prompt / sample passes below are not chained over depth"
    row = lambda a: a.reshape(1, -1)
    wts = dict(
        g_mix=row(g_mix[0]),
        wa2_pad=jnp.pad(w_a2[0], ((0, V7X_LANES - GLA_GATE_RANK), (0, 0))),
        b_a=row(b_a[0]),
        g_gla_norm=row(g_gla_norm[0]),
        g_gmlp_ln=row(g_gmlp_ln[0]),
        b_gmlp_ln=row(b_gmlp_ln[0]),
        w_s=w_s[0],
        b_s=b_s[0],
        g_gmlp_out=row(g_gmlp_out[0]),
        g_ffn=row(g_ffn[0]),
        w_ple=w_ple[0].astype(BF16),
        g_ple=row(g_ple[0]),
        g_ple_gate=row(g_ple_gate[0]),
        g_final=row(g_final),
    )
    xp = x_prompt.reshape(batch * seq, D_MODEL)
    xs = x_sample.reshape(dec_batch * dec_seq, D_MODEL)
    wt_in = jnp.swapaxes(w_in, 1, 2)[0]
    zs, alow_s, wt16, wl16 = _inproj(xs, wts["g_mix"], wt_in, wt_in, tm=xs.shape[0], tn=INPROJ_COLS, emit=True)
    zp, alow_p, wts["w_out"], wts["w_ffn_out"], w_gate16 = _inproj(
        xp, wts["g_mix"], wt16, wl16, tm=INPROJ_ROWS, tn=INPROJ_COLS, emit=False,
        cast=(w_out[0], w_ffn_out[0], (w_ffn_in[0], 0, 2)))
    op, sp, os_, ss, w_up16, wts["w_ple_gate"] = _gla(
        zp, alow_p, zs, alow_s, state_gla[0], wts["wa2_pad"], wts["b_a"], wts["g_gla_norm"],
        n_seq_p=batch, len_p=seq, n_seq_s=dec_batch, len_s=dec_seq, cast=((w_ffn_in[0], 1, 2), w_ple_gate[0]))
    h1, vs = _mixout(xp, xs, op, os_, zp, zs, wts["g_gmlp_ln"], wts["b_gmlp_ln"], wts["w_s"], wts["b_s"],
                     wts["g_gmlp_out"], wts["w_out"], tm=MIX_ROWS, len_p=seq, len_s=dec_seq)
    h2 = _ffn(h1, wts["g_ffn"], w_gate16, w_up16, wts["w_ffn_out"], tm=FFN_ROWS, tf=FFN_COLS)
    yp, ys = _ple(h2, p_prompt[0].reshape(batch * seq, PLE_DIM), p_sample[0].reshape(dec_batch * dec_seq, PLE_DIM),
                  wts["w_ple"], wts["g_ple"], wts["g_ple_gate"], wts["w_ple_gate"], wts["g_final"], tm=PLE_ROWS)
    return (yp.reshape(batch, seq, D_MODEL), ys.reshape(dec_batch, dec_seq, D_MODEL), sp[None], ss[None],
            vs.reshape(1, dec_batch, dec_seq, GMLP_WIDTH))
```

```python
import functools

import numpy as np

import jax
import jax.numpy as jnp
from jax import lax
from jax.experimental import pallas as pl
from jax.experimental.pallas import tpu as pltpu

F32 = jnp.float32
BF16 = jnp.bfloat16

D_MODEL = 2048
GLA_WIDTH = 1024
GMLP_WIDTH = 1024
GLA_HEADS = 4
GLA_DV = 256
GLA_DK = 128
GLA_KEY_WIDTH = 512
GLA_GATE_RANK = 16
GLA_TAU = 16.0
GLA_CHUNK = 64
GMLP_GROUPS = 8
GMLP_GROUP_DIM = 128
GMLP_CHUNK = 128
D_FF = 5632
PLE_DIM = 256
EPS = 1e-6
IN_OFFSETS = (0, 512, 1024, 2048, 3072, 3088, 4112, 5136)
Z_WIDTH = 5120
Z_GLA_WIDTH = 3072
Z_U_OFF = 3072
Z_VG_OFF = 4096

V7X_LANES = 128
V7X_BF16_ROWS = 16
V7X_VMEM_BYTES = 64 * 1024 * 1024
VMEM_LIMIT_BYTES = V7X_VMEM_BYTES - 4 * 1024 * 1024
INPROJ_ROWS, INPROJ_COLS = 1024, 1024
MIX_ROWS = 512
FFN_ROWS, FFN_COLS = 1024, 512
PLE_ROWS, PLE_COL_CHUNK = 512, 1024
GLA_SEQ_ROWS = 4 * GLA_CHUNK
N_CAST_BLOCKS = 32
W_RING_SLOTS = 3


def _params(*sem):
    return pltpu.CompilerParams(dimension_semantics=sem, vmem_limit_bytes=VMEM_LIMIT_BYTES)


def _rms(x, g):
    return x * lax.rsqrt(jnp.mean(x * x, axis=-1, keepdims=True) + EPS) * g


def _dot(a, b):
    return jnp.dot(a, b, preferred_element_type=F32)


def _dot_tn(a, b):
    return lax.dot_general(a, b, (((0,), (0,)), ((), ())), preferred_element_type=F32)


def _dot_nt(a, b):
    return lax.dot_general(a, b, (((1,), (1,)), ((), ())), preferred_element_type=F32)


def _cast_specs(cast, n_blocks, step_of):
    in_specs, out_specs, out_shape = [], [], []
    for item in cast:
        w, k, n = item if isinstance(item, tuple) else (item, 0, 1)
        rows, cols = w.shape[0] // n_blocks, w.shape[1] // n
        assert rows * n_blocks == w.shape[0] and rows % V7X_BF16_ROWS == 0 and cols % V7X_LANES == 0
        step = lambda *ids: jnp.minimum(step_of(*ids), n_blocks - 1)
        in_specs.append(pl.BlockSpec((rows, cols), lambda *ids, step=step, k=k: (step(*ids), k)))
        out_specs.append(pl.BlockSpec((rows, cols), lambda *ids, step=step: (step(*ids), 0)))
        out_shape.append(jax.ShapeDtypeStruct((w.shape[0], cols), BF16))
    return in_specs, out_specs, out_shape


def _cast_args(cast):
    return [item[0] if isinstance(item, tuple) else item for item in cast]


def _cast_blocks(cast_in, cast_out):
    for src_ref, dst_ref in zip(cast_in, cast_out):
        dst_ref[...] = src_ref[...].astype(BF16)


def _inproj_kernel(*refs, emit, n_cast):
    x_ref, g_ref, w_ref, wl_ref = refs[:4]
    cast_in = refs[4:4 + n_cast]
    n_out = 4 if emit else 2
    z_ref, alow_ref = refs[4 + n_cast:6 + n_cast]
    emit_refs = refs[6 + n_cast:4 + n_cast + n_out]
    cast_out = refs[4 + n_cast + n_out:4 + 2 * n_cast + n_out]
    xb_scr, r_scr = refs[4 + 2 * n_cast + n_out:6 + 2 * n_cast + n_out]
    j = pl.program_id(1)

    if not emit:
        w_ring, w_sem, x_buf, x_sem = refs[6 + 2 * n_cast + n_out:]
        tn = w_ring.shape[1]
        n_j = pl.num_programs(1)
        n_steps = pl.num_programs(0) * n_j
        step = pl.program_id(0) * n_j + j

        def tile_copy(s):
            slot = lax.rem(s, W_RING_SLOTS)
            row = pl.multiple_of(lax.rem(s, n_j) * tn, tn)
            return pltpu.make_async_copy(w_ref.at[pl.ds(row, tn), :], w_ring.at[slot], w_sem.at[slot])

        @pl.when(step == 0)
        def _():
            for s in range(W_RING_SLOTS - 1):
                tile_copy(s).start()

        @pl.when(step + (W_RING_SLOTS - 1) < n_steps)
        def _():
            tile_copy(step + (W_RING_SLOTS - 1)).start()

        tile_copy(step).wait()
        w_slot = lax.rem(step, W_RING_SLOTS)

        tm = x_buf.shape[1]
        i = pl.program_id(0)

        def rows_copy(t):
            slot = lax.rem(t, 2)
            row = pl.multiple_of(t * tm, tm)
            return pltpu.make_async_copy(x_ref.at[pl.ds(row, tm), :], x_buf.at[slot], x_sem.at[slot])

        @pl.when(step == 0)
        def _():
            rows_copy(0).start()

        @pl.when(jnp.logical_and(j == 1, i + 1 < pl.num_programs(0)))
        def _():
            rows_copy(i + 1).start()

        @pl.when(j == 0)
        def _():
            rows_copy(i).wait()

        x_slot = lax.rem(i, 2)

    def weight(ref, emit_ref):
        if not emit:
            return ref[...]
        w16 = ref[...].astype(BF16)
        emit_ref[...] = w16
        return w16

    def z_tile(xb, r):
        w16 = weight(w_ref, emit_refs[0]) if emit else w_ring[w_slot]
        return r * _dot_nt(xb, w16)

    @pl.when(j == 0)
    def _():
        _cast_blocks(cast_in, cast_out)
        x = x_ref[...] if emit else x_buf[x_slot]
        xb = (x * g_ref[...]).astype(BF16)
        xb_scr[...] = xb
        r = lax.rsqrt(jnp.mean(x * x, axis=-1, keepdims=True) + EPS)
        r_scr[...] = r
        alow_ref[...] = r * _dot_nt(xb, weight(wl_ref, emit_refs[1] if emit else None))
        z_ref[...] = z_tile(xb, r).astype(BF16)

    @pl.when(j > 0)
    def _():
        _cast_blocks(cast_in, cast_out)
        z_ref[...] = z_tile(xb_scr[...], r_scr[...]).astype(BF16)


def _inproj(x, g_mix, w, wl, *, tm, tn, emit, cast=()):
    m = x.shape[0]
    n_a = Z_GLA_WIDTH // tn
    n_j = Z_WIDTH // tn
    if emit:
        def w_row(j):
            return pl.multiple_of(jnp.where(j < n_a, j * tn, IN_OFFSETS[5] + (j - n_a) * tn), V7X_BF16_ROWS)
        w_spec = pl.BlockSpec((pl.Element(tn), pl.Element(D_MODEL)), lambda i, j: (w_row(j), 0))
        wl_spec = pl.BlockSpec((V7X_LANES, D_MODEL), lambda i, j: (IN_OFFSETS[4] // V7X_LANES, 0))
        assert m == tm, "every weight tile must be visited exactly once"
    else:
        w_spec = pl.BlockSpec(memory_space=pl.ANY)
        wl_spec = pl.BlockSpec((V7X_LANES, D_MODEL), lambda i, j: (0, 0))
        assert (m // tm) * n_j >= W_RING_SLOTS - 1
    cast_in, cast_out, cast_shape = _cast_specs(cast, N_CAST_BLOCKS, lambda i, j: i * n_j + j)
    assert not cast or (m // tm) * n_j >= N_CAST_BLOCKS
    out_specs = [
        pl.BlockSpec((tm, tn), lambda i, j: (i, j)),
        pl.BlockSpec((tm, V7X_LANES), lambda i, j: (i, 0)),
    ]
    out_shape = [
        jax.ShapeDtypeStruct((m, Z_WIDTH), BF16),
        jax.ShapeDtypeStruct((m, V7X_LANES), F32),
    ]
    if emit:
        out_specs += [pl.BlockSpec((tn, D_MODEL), lambda i, j: (j, 0)),
                      pl.BlockSpec((V7X_LANES, D_MODEL), lambda i, j: (0, 0))]
        out_shape += [jax.ShapeDtypeStruct((Z_WIDTH, D_MODEL), BF16),
                      jax.ShapeDtypeStruct((V7X_LANES, D_MODEL), BF16)]
    x_spec = pl.BlockSpec((tm, D_MODEL), lambda i, j: (i, 0)) if emit else pl.BlockSpec(memory_space=pl.ANY)
    ring = [] if emit else [pltpu.VMEM((W_RING_SLOTS, tn, D_MODEL), BF16), pltpu.SemaphoreType.DMA((W_RING_SLOTS,)),
                            pltpu.VMEM((2, tm, D_MODEL), F32), pltpu.SemaphoreType.DMA((2,))]
    return pl.pallas_call(
        functools.partial(_inproj_kernel, emit=emit, n_cast=len(cast)),
        grid=(m // tm, n_j),
        in_specs=[
            x_spec,
            pl.BlockSpec((1, D_MODEL), lambda i, j: (0, 0)),
            w_spec,
            wl_spec,
        ] + cast_in,
        out_specs=out_specs + cast_out,
        out_shape=out_shape + cast_shape,
        scratch_shapes=[pltpu.VMEM((tm, D_MODEL), BF16), pltpu.VMEM((tm, 1), F32)] + ring,
        compiler_params=_params("parallel" if emit else "arbitrary", "arbitrary"),
        name="inproj",
    )(x, g_mix, w, wl, *_cast_args(cast))


def _group_mask(R, C):
    t = np.arange(R)[:, None]
    s = np.arange(R)[None, :]
    return ((t // C == s // C) & (s <= t)).astype(np.float32)


def _gla_tile(q_ref, k_ref, v_ref, r_ref, alow_ref, tri_ref, mask_ref, wa2_ref, ba_ref, gn_ref,
              o_ref, s_out_ref, b_scr, e_scr, *, C, s_scr=None, s0_ref=None):
    sequential = s_scr is not None
    R = q_ref.shape[0]
    n = R // C
    mid = C // 2
    W = GLA_KEY_WIDTH

    x = _dot(alow_ref[...].astype(BF16), wa2_ref[...].astype(BF16)) + ba_ref[...]
    lb = (jnp.minimum(x, 0.0) - jnp.log(1.0 + jnp.exp(-jnp.abs(x)))) * (1.0 / GLA_TAU)

    tri = tri_ref[...]
    lb_hi = lb.astype(BF16)
    lb_lo = (lb - lb_hi.astype(F32)).astype(BF16)
    b = _dot(tri, lb_hi) + _dot(tri, lb_lo)
    b3 = b.reshape(n, C, W)
    b_mid = jnp.broadcast_to(b3[:, mid:mid + 1, :], (n, C, W)).reshape(R, W)
    b_last = jnp.broadcast_to(b3[:, C - 1:C, :], (n, C, W)).reshape(R, W)

    q = q_ref[...].astype(F32) * (GLA_DK ** -0.5)
    k = k_ref[...].astype(F32)
    q1 = (q * jnp.exp(b)).astype(BF16)
    q2 = (q * jnp.exp(b - b_mid)).astype(BF16)
    k2 = (k * jnp.exp(b_mid - b)).astype(BF16)
    k3 = k * jnp.exp(b_last - b)

    e_scr[...] = jnp.zeros((V7X_LANES, GLA_DK), F32)
    for h in range(GLA_HEADS):
        b_scr[h] = b[:, h * GLA_DK:(h + 1) * GLA_DK]
        e_scr[h * n:(h + 1) * n, :] = jnp.exp(b_scr[h, pl.ds(C - 1, n, stride=C), :])
    d_cols = e_scr[...].T

    causal = mask_ref[...] > 0.0
    if not sequential:
        grp = lax.broadcasted_iota(jnp.int32, (R, 1), 0) >> (C.bit_length() - 1)

    for h in range(GLA_HEADS):
        ks = slice(h * GLA_DK, (h + 1) * GLA_DK)
        vs = slice(h * GLA_DV, (h + 1) * GLA_DV)
        v = v_ref[:, vs]
        att = jnp.where(causal, _dot_nt(q2[:, ks], k2[:, ks]), 0.0).astype(BF16)
        o = _dot(att, v)
        q1h = q1[:, ks]
        k3h = k3[:, ks]
        if sequential:
            s = s_scr[h]
            o_parts = []
            for g in range(n):
                rows = slice(g * C, (g + 1) * C)
                o_parts.append(o[rows] + _dot(q1h[rows], s.astype(BF16)))
                s = d_cols[:, h * n + g:h * n + g + 1] * s + _dot_tn(k3h[rows].astype(BF16), v[rows])
            s_scr[h] = s
            s_out_ref[0, h] = s
            o = jnp.concatenate(o_parts, axis=0)
        else:
            for g in range(n):
                s = s0_ref[g, h]
                o = o + jnp.where(grp == g, _dot(q1h, s.astype(BF16)), 0.0)
                k3g = jnp.where(grp == g, k3h, 0.0).astype(BF16)
                s_out_ref[g, h] = d_cols[:, h * n + g:h * n + g + 1] * s + _dot_tn(k3g, v)
        o = _rms(o, gn_ref[:, vs]) * jax.nn.silu(r_ref[:, vs].astype(F32))
        o_ref[:, vs] = o.astype(BF16)


N_GLA_TILE_IN = 7


def _gla_kernel(*refs, C_seq, C_step, n_cast):
    n = N_GLA_TILE_IN
    seq_in, step_in = refs[:n], refs[n:2 * n]
    s0_ref, wa2_ref, ba_ref, gn_ref = refs[2 * n:2 * n + 4]
    cast_in = refs[2 * n + 4:2 * n + 4 + n_cast]
    outs = refs[2 * n + 4 + n_cast:]
    o_seq, s_seq, o_step, s_step = outs[:4]
    cast_out = outs[4:4 + n_cast]
    s_scr, b_seq, e_seq, b_step, e_step = outs[4 + n_cast:]
    shared = (wa2_ref, ba_ref, gn_ref)

    @pl.when(pl.program_id(1) == 0)
    def _():
        s_scr[...] = jnp.zeros((GLA_HEADS, GLA_DK, GLA_DV), F32)

    _cast_blocks(cast_in, cast_out)
    _gla_tile(*seq_in, *shared, o_seq, s_seq, b_seq, e_seq, C=C_seq, s_scr=s_scr)
    _gla_tile(*step_in, *shared, o_step, s_step, b_step, e_step, C=C_step, s0_ref=s0_ref)


def _gla(zp, alow_p, zs, alow_s, s0, wa2_pad, b_a, g_norm, *, n_seq_p, len_p, n_seq_s, len_s, cast=()):
    R_p = GLA_SEQ_ROWS
    nt = len_p // R_p
    n_steps = n_seq_p * nt
    R_s = zs.shape[0] // n_steps
    assert R_s * n_steps == zs.shape[0] and R_s % len_s == 0 and R_s % V7X_BF16_ROWS == 0
    assert not cast or n_steps >= N_CAST_BLOCKS
    step_of = lambda a, c: a * nt + c
    const = lambda a, c: (0, 0)

    def tile_specs(R):
        return [
            pl.BlockSpec((R, GLA_KEY_WIDTH), lambda a, c: (step_of(a, c), 0)),
            pl.BlockSpec((R, GLA_KEY_WIDTH), lambda a, c: (step_of(a, c), 1)),
            pl.BlockSpec((R, GLA_WIDTH), lambda a, c: (step_of(a, c), 1)),
            pl.BlockSpec((R, GLA_WIDTH), lambda a, c: (step_of(a, c), 2)),
            pl.BlockSpec((R, V7X_LANES), lambda a, c: (step_of(a, c), 0)),
            pl.BlockSpec((R, R), const),
            pl.BlockSpec((R, R), const),
        ]

    def tile_args(z, alow, R, C):
        mask = _group_mask(R, C)
        return [z, z, z, z, alow, jnp.asarray(mask, BF16), jnp.asarray(mask, F32)]

    s_block_p = (1, GLA_HEADS, GLA_DK, GLA_DV)
    s_block_s = (R_s // len_s, GLA_HEADS, GLA_DK, GLA_DV)
    state_p = pl.BlockSpec(s_block_p, lambda a, c: (a, 0, 0, 0))
    state_s = pl.BlockSpec(s_block_s, lambda a, c: (step_of(a, c), 0, 0, 0))
    cast_in, cast_out, cast_shape = _cast_specs(cast, N_CAST_BLOCKS, step_of)
    scratch = [pltpu.VMEM((GLA_HEADS, GLA_DK, GLA_DV), F32)]
    for R in (R_p, R_s):
        scratch += [pltpu.VMEM((GLA_HEADS, R, GLA_DK), F32), pltpu.VMEM((V7X_LANES, GLA_DK), F32)]
    return pl.pallas_call(
        functools.partial(_gla_kernel, C_seq=GLA_CHUNK, C_step=len_s, n_cast=len(cast)),
        grid=(n_seq_p, nt),
        in_specs=tile_specs(R_p) + tile_specs(R_s) + [
            state_s,
            pl.BlockSpec((V7X_LANES, GLA_KEY_WIDTH), const),
            pl.BlockSpec((1, GLA_KEY_WIDTH), const),
            pl.BlockSpec((1, GLA_WIDTH), const),
        ] + cast_in,
        out_specs=[
            pl.BlockSpec((R_p, GLA_WIDTH), lambda a, c: (step_of(a, c), 0)),
            state_p,
            pl.BlockSpec((R_s, GLA_WIDTH), lambda a, c: (step_of(a, c), 0)),
            state_s,
        ] + cast_out,
        out_shape=[
            jax.ShapeDtypeStruct((zp.shape[0], GLA_WIDTH), BF16),
            jax.ShapeDtypeStruct((n_seq_p, GLA_HEADS, GLA_DK, GLA_DV), F32),
            jax.ShapeDtypeStruct((zs.shape[0], GLA_WIDTH), BF16),
            jax.ShapeDtypeStruct((n_seq_s, GLA_HEADS, GLA_DK, GLA_DV), F32),
        ] + cast_shape,
        scratch_shapes=scratch,
        compiler_params=_params("parallel", "arbitrary"),
        name="gla",
    )(*tile_args(zp, alow_p, R_p, GLA_CHUNK), *tile_args(zs, alow_s, R_s, len_s), s0, wa2_pad, b_a, g_norm,
      *_cast_args(cast))


def _mixout_tile(h_ref, o_ref, u_ref, vg_ref, gln_ref, bln_ref, ws_ref, bias_ref, gout_ref, wo_ref, wm_ref,
                 h1_ref, vn_ref, acc_scr, *, tm, blk):
    n_chunks = tm // GMLP_CHUNK
    col_w = D_MODEL // n_chunks
    shift = blk.bit_length() - 1
    row = lax.broadcasted_iota(jnp.int32, (GMLP_CHUNK, GMLP_CHUNK), 0)
    col = lax.broadcasted_iota(jnp.int32, (GMLP_CHUNK, GMLP_CHUNK), 1)
    causal = jnp.logical_and((row >> shift) == (col >> shift), col <= row)
    w = [jnp.where(causal, ws_ref[g], 0.0).astype(BF16) for g in range(GMLP_GROUPS)]
    m_parts = []
    for c in range(n_chunks):
        cols = slice(c * col_w, (c + 1) * col_w)
        acc_scr[:, cols] = h_ref[:, cols] + _dot(o_ref[...], wo_ref[:, cols])
        rows = pl.ds(c * GMLP_CHUNK, GMLP_CHUNK)
        vg = jax.nn.gelu(vg_ref[rows, :].astype(F32))
        mu = jnp.mean(vg, axis=-1, keepdims=True)
        vc = vg - mu
        vn = vc * lax.rsqrt(jnp.mean(vc * vc, axis=-1, keepdims=True) + EPS) * gln_ref[...] + bln_ref[...]
        if vn_ref is not None:
            vn_ref[rows, :] = vn
        vn16 = vn.astype(BF16)
        mixed = jnp.concatenate(
            [_dot(w[g], vn16[:, g * GMLP_GROUP_DIM:(g + 1) * GMLP_GROUP_DIM]) for g in range(GMLP_GROUPS)],
            axis=1) + bias_ref[...]
        mm = jax.nn.gelu(u_ref[rows, :].astype(F32)) * mixed
        m_parts.append(_rms(mm, gout_ref[...]).astype(BF16))
    h1_ref[...] = acc_scr[...] + _dot(jnp.concatenate(m_parts, axis=0), wm_ref[...])


def _mixout_kernel(hp_ref, hs_ref, op_ref, os_ref, up_ref, vgp_ref, us_ref, vgs_ref, gln_ref, bln_ref,
                   wsp_ref, biasp_ref, wss_ref, biass_ref, gout_ref, wo_ref, wm_ref, h1_ref, vn_ref, acc_scr,
                   *, tm, n_p, blk_p, blk_s):
    shared = (gln_ref, bln_ref)
    tail = (gout_ref, wo_ref, wm_ref, h1_ref)
    i = pl.program_id(0)

    @pl.when(i < n_p)
    def _():
        _mixout_tile(hp_ref, op_ref, up_ref, vgp_ref, *shared, wsp_ref, biasp_ref, *tail, None, acc_scr, tm=tm,
                     blk=blk_p)

    @pl.when(i >= n_p)
    def _():
        _mixout_tile(hs_ref, os_ref, us_ref, vgs_ref, *shared, wss_ref, biass_ref, *tail, vn_ref, acc_scr, tm=tm,
                     blk=blk_s)


def _gmlp_operands(w_s, b_s, blk):
    reps = GMLP_CHUNK // blk
    sel = jnp.asarray(np.tile(np.eye(blk, dtype=np.float32), (reps, 1)))
    w_mix = jnp.einsum("ra,gab,cb->grc", sel, w_s[:, :blk, :blk], sel, precision=lax.Precision.HIGHEST)
    bias = jnp.tile(jnp.repeat(b_s[:, :blk].T, GMLP_GROUP_DIM, axis=1), (reps, 1))
    return w_mix, bias


def _mixout(hp, hs, op, os_, zp, zs, g_ln, b_ln, w_s, b_s, g_out, w_out, *, tm, len_p, len_s):
    n_p, n_s = hp.shape[0] // tm, hs.shape[0] // tm
    blk_p, blk_s = min(len_p, GMLP_CHUNK), min(len_s, GMLP_CHUNK)
    pidx = lambda i: jnp.minimum(i, n_p - 1)
    sidx = lambda i: jnp.maximum(i - n_p, 0)
    once = pl.Buffered(1)
    vec = pl.BlockSpec((1, GMLP_WIDTH), lambda i: (0, 0))
    mixw = pl.BlockSpec((GMLP_GROUPS, GMLP_CHUNK, GMLP_CHUNK), lambda i: (0, 0, 0))
    biasb = pl.BlockSpec((GMLP_CHUNK, GMLP_WIDTH), lambda i: (0, 0))
    return pl.pallas_call(
        functools.partial(_mixout_kernel, tm=tm, n_p=n_p, blk_p=blk_p, blk_s=blk_s),
        grid=(n_p + n_s,),
        in_specs=[
            pl.BlockSpec((tm, D_MODEL), lambda i: (pidx(i), 0)),
            pl.BlockSpec((tm, D_MODEL), lambda i: (sidx(i), 0)),
            pl.BlockSpec((tm, GLA_WIDTH), lambda i: (pidx(i), 0)),
            pl.BlockSpec((tm, GLA_WIDTH), lambda i: (sidx(i), 0)),
            pl.BlockSpec((tm, GMLP_WIDTH), lambda i: (pidx(i), Z_U_OFF // GMLP_WIDTH)),
            pl.BlockSpec((tm, GMLP_WIDTH), lambda i: (pidx(i), Z_VG_OFF // GMLP_WIDTH)),
            pl.BlockSpec((tm, GMLP_WIDTH), lambda i: (sidx(i), Z_U_OFF // GMLP_WIDTH)),
            pl.BlockSpec((tm, GMLP_WIDTH), lambda i: (sidx(i), Z_VG_OFF // GMLP_WIDTH)),
            vec, vec,
            mixw, biasb, mixw, biasb,
            vec,
            pl.BlockSpec((GLA_WIDTH, D_MODEL), lambda i: (0, 0), pipeline_mode=once),
            pl.BlockSpec((GMLP_WIDTH, D_MODEL), lambda i: (1, 0), pipeline_mode=once),
        ],
        out_specs=[
            pl.BlockSpec((tm, D_MODEL), lambda i: (i, 0)),
            pl.BlockSpec((tm, GMLP_WIDTH), lambda i: (sidx(i), 0)),
        ],
        out_shape=[
            jax.ShapeDtypeStruct((hp.shape[0] + hs.shape[0], D_MODEL), F32),
            jax.ShapeDtypeStruct((hs.shape[0], GMLP_WIDTH), F32),
        ],
        scratch_shapes=[pltpu.VMEM((tm, D_MODEL), F32)],
        compiler_params=_params("arbitrary"),
        name="mixout",
    )(hp, hs, op, os_, zp, zp, zs, zs, g_ln, b_ln, *_gmlp_operands(w_s, b_s, blk_p), *_gmlp_operands(w_s, b_s, blk_s),
      g_out, w_out, w_out)


def _ffn_kernel(h_ref, g_ref, wg_ref, wu_ref, wo_ref, o_ref, hb_scr, r_scr):
    j = pl.program_id(1)

    def tile(hb, r):
        gate = r * _dot(hb, wg_ref[...])
        up = r * _dot(hb, wu_ref[...])
        return _dot((jax.nn.silu(gate) * up).astype(BF16), wo_ref[...])

    @pl.when(j == 0)
    def _():
        h = h_ref[...]
        hb = (h * g_ref[...]).astype(BF16)
        hb_scr[...] = hb
        r = lax.rsqrt(jnp.mean(h * h, axis=-1, keepdims=True) + EPS)
        r_scr[...] = r
        o_ref[...] = h + tile(hb, r)

    @pl.when(j > 0)
    def _():
        o_ref[...] += tile(hb_scr[...], r_scr[...])


def _ffn(h1, g_ffn, w_gate, w_up, w_out, tm, tf):
    m = h1.shape[0]
    return pl.pallas_call(
        _ffn_kernel,
        grid=(m // tm, D_FF // tf),
        in_specs=[
            pl.BlockSpec((tm, D_MODEL), lambda i, j: (i, 0)),
            pl.BlockSpec((1, D_MODEL), lambda i, j: (0, 0)),
            pl.BlockSpec((D_MODEL, tf), lambda i, j: (0, j)),
            pl.BlockSpec((D_MODEL, tf), lambda i, j: (0, j)),
            pl.BlockSpec((tf, D_MODEL), lambda i, j: (j, 0)),
        ],
        out_specs=pl.BlockSpec((tm, D_MODEL), lambda i, j: (i, 0)),
        out_shape=jax.ShapeDtypeStruct((m, D_MODEL), F32),
        scratch_shapes=[pltpu.VMEM((tm, D_MODEL), BF16), pltpu.VMEM((tm, 1), F32)],
        compiler_params=_params("parallel", "arbitrary"),
        name="ffn",
    )(h1, g_ffn, w_gate, w_up, w_out)


def _ple_tile(h_ref, p_ref, wp_ref, gp_ref, gg_ref, wg_ref, gf_ref, y_ref, t_scr):
    h = h_ref[...]
    r = lax.rsqrt(jnp.mean(h * h, axis=-1, keepdims=True) + EPS)
    hb = (h * gg_ref[...]).astype(BF16)
    pe = _rms(_dot(p_ref[...].astype(BF16), wp_ref[...]), gp_ref[...])
    ssq = jnp.zeros((h.shape[0], 1), F32)
    for c in range(D_MODEL // PLE_COL_CHUNK):
        cols = slice(c * PLE_COL_CHUNK, (c + 1) * PLE_COL_CHUNK)
        t = h[:, cols] + pe[:, cols] * jax.nn.sigmoid(r * _dot(hb, wg_ref[:, cols]))
        ssq = ssq + jnp.sum(t * t, axis=-1, keepdims=True)
        t_scr[:, cols] = t
    y_ref[...] = t_scr[...] * lax.rsqrt(ssq / D_MODEL + EPS) * gf_ref[...]


def _ple_kernel(h_ref, pp_ref, ps_ref, wp_ref, gp_ref, gg_ref, wg_ref, gf_ref, yp_ref, ys_ref, t_scr, *, n_p):
    params = (wp_ref, gp_ref, gg_ref, wg_ref, gf_ref)
    i = pl.program_id(0)

    @pl.when(i < n_p)
    def _():
        _ple_tile(h_ref, pp_ref, *params, yp_ref, t_scr)

    @pl.when(i >= n_p)
    def _():
        _ple_tile(h_ref, ps_ref, *params, ys_ref, t_scr)


def _ple(h, pp, ps, w_ple, g_ple, g_gate, w_gate, g_final, tm):
    n_p, n_s = pp.shape[0] // tm, ps.shape[0] // tm
    pidx = lambda i: (jnp.minimum(i, n_p - 1), 0)
    sidx = lambda i: (jnp.maximum(i - n_p, 0), 0)
    vec = pl.BlockSpec((1, D_MODEL), lambda i: (0, 0))
    return pl.pallas_call(
        functools.partial(_ple_kernel, n_p=n_p),
        grid=(n_p + n_s,),
        in_specs=[
            pl.BlockSpec((tm, D_MODEL), lambda i: (i, 0)),
            pl.BlockSpec((tm, PLE_DIM), pidx),
            pl.BlockSpec((tm, PLE_DIM), sidx),
            pl.BlockSpec((PLE_DIM, D_MODEL), lambda i: (0, 0)),
            vec, vec,
            pl.BlockSpec((D_MODEL, D_MODEL), lambda i: (0, 0), pipeline_mode=pl.Buffered(1)),
            vec,
        ],
        out_specs=[pl.BlockSpec((tm, D_MODEL), pidx), pl.BlockSpec((tm, D_MODEL), sidx)],
        out_shape=[jax.ShapeDtypeStruct((pp.shape[0], D_MODEL), F32),
                   jax.ShapeDtypeStruct((ps.shape[0], D_MODEL), F32)],
        scratch_shapes=[pltpu.VMEM((tm, D_MODEL), F32)],
        compiler_params=_params("arbitrary"),
        name="ple",
    )(h, pp, ps, w_ple, g_ple, g_gate, w_gate, g_final)


def kernel(x_prompt, x_sample, state_gla, p_prompt, p_sample, g_mix, w_in, w_a2, b_a, g_gla_norm, g_gmlp_ln,
           b_gmlp_ln, w_s, b_s, g_gmlp_out, w_out, g_ffn, w_ffn_in, w_ffn_out, w_ple, g_ple, g_ple_gate,
           w_ple_gate, g_final):
    batch, seq, _ = x_prompt.shape
    dec_batch, dec_seq, _ = x_sample.shape
    depth = w_in.shape[0]
    assert depth == 1, "one layer: the prompt / sample passes below are not chained over depth"
    row = lambda a: a.reshape(1, -1)
    wts = dict(
        g_mix=row(g_mix[0]),
        wa2_pad=jnp.pad(w_a2[0], ((0, V7X_LANES - GLA_GATE_RANK), (0, 0))),
        b_a=row(b_a[0]),
        g_gla_norm=row(g_gla_norm[0]),
        g_gmlp_ln=row(g_gmlp_ln[0]),
        b_gmlp_ln=row(b_gmlp_ln[0]),
        w_s=w_s[0],
        b_s=b_s[0],
        g_gmlp_out=row(g_gmlp_out[0]),
        g_ffn=row(g_ffn[0]),
        w_ple=w_ple[0].astype(BF16),
        g_ple=row(g_ple[0]),
        g_ple_gate=row(g_ple_gate[0]),
        g_final=row(g_final),
    )
    xp = x_prompt.reshape(batch * seq, D_MODEL)
    xs = x_sample.reshape(dec_batch * dec_seq, D_MODEL)
    wt_in = jnp.swapaxes(w_in, 1, 2)[0]
    zs, alow_s, wt16, wl16 = _inproj(xs, wts["g_mix"], wt_in, wt_in, tm=xs.shape[0], tn=INPROJ_COLS, emit=True)
    zp, alow_p, wts["w_out"], wts["w_ffn_out"], w_gate16 = _inproj(
        xp, wts["g_mix"], wt16, wl16, tm=INPROJ_ROWS, tn=INPROJ_COLS, emit=False,
        cast=(w_out[0], w_ffn_out[0], (w_ffn_in[0], 0, 2)))
    op, sp, os_, ss, w_up16, wts["w_ple_gate"] = _gla(
        zp, alow_p, zs, alow_s, state_gla[0], wts["wa2_pad"], wts["b_a"], wts["g_gla_norm"],
        n_seq_p=batch, len_p=seq, n_seq_s=dec_batch, len_s=dec_seq, cast=((w_ffn_in[0], 1, 2), w_ple_gate[0]))
    h1, vs = _mixout(xp, xs, op, os_, zp, zs, wts["g_gmlp_ln"], wts["b_gmlp_ln"], wts["w_s"], wts["b_s"],
                     wts["g_gmlp_out"], wts["w_out"], tm=MIX_ROWS, len_p=seq, len_s=dec_seq)
    h2 = _ffn(h1, wts["g_ffn"], w_gate16, w_up16, wts["w_ffn_out"], tm=FFN_ROWS, tf=FFN_COLS)
    yp, ys = _ple(h2, p_prompt[0].reshape(batch * seq, PLE_DIM), p_sample[0].reshape(dec_batch * dec_seq, PLE_DIM),
                  wts["w_ple"], wts["g_ple"], wts["g_ple_gate"], wts["w_ple_gate"], wts["g_final"], tm=PLE_ROWS)
    return (yp.reshape(batch, seq, D_MODEL), ys.reshape(dec_batch, dec_seq, D_MODEL), sp[None], ss[None],
            vs.reshape(1, dec_batch, dec_seq, GMLP_WIDTH))
```

```python
import functools

import numpy as np

import jax
import jax.numpy as jnp
from jax import lax
from jax.experimental import pallas as pl
from jax.experimental.pallas import tpu as pltpu

F32 = jnp.float32
BF16 = jnp.bfloat16

D_MODEL = 2048
GLA_WIDTH = 1024
GMLP_WIDTH = 1024
GLA_HEADS = 4
GLA_DV = 256
GLA_DK = 128
GLA_KEY_WIDTH = 512
GLA_GATE_RANK = 16
GLA_TAU = 16.0
GLA_CHUNK = 64
GMLP_GROUPS = 8
GMLP_GROUP_DIM = 128
GMLP_CHUNK = 128
D_FF = 5632
PLE_DIM = 256
EPS = 1e-6
IN_OFFSETS = (0, 512, 1024, 2048, 3072, 3088, 4112, 5136)
Z_WIDTH = 5120
Z_GLA_WIDTH = 3072
Z_U_OFF = 3072
Z_VG_OFF = 4096

V7X_LANES = 128
V7X_BF16_ROWS = 16
V7X_VMEM_BYTES = 64 * 1024 * 1024
VMEM_LIMIT_BYTES = V7X_VMEM_BYTES - 4 * 1024 * 1024
INPROJ_ROWS, INPROJ_COLS = 1024, 1024
MIX_ROWS = 512
FFN_ROWS, FFN_COLS = 1024, 512
PLE_ROWS, PLE_COL_CHUNK = 512, 1024
GLA_SEQ_ROWS = 8 * GLA_CHUNK
N_CAST_BLOCKS = 32
W_RING_SLOTS = 3


def _params(*sem):
    return pltpu.CompilerParams(dimension_semantics=sem, vmem_limit_bytes=VMEM_LIMIT_BYTES)


def _rms(x, g):
    return x * lax.rsqrt(jnp.mean(x * x, axis=-1, keepdims=True) + EPS) * g


def _dot(a, b):
    return jnp.dot(a, b, preferred_element_type=F32)


def _dot_tn(a, b):
    return lax.dot_general(a, b, (((0,), (0,)), ((), ())), preferred_element_type=F32)


def _dot_nt(a, b):
    return lax.dot_general(a, b, (((1,), (1,)), ((), ())), preferred_element_type=F32)


def _cast_specs(cast, n_blocks, step_of):
    in_specs, out_specs, out_shape = [], [], []
    for item in cast:
        w, k, n = item if isinstance(item, tuple) else (item, 0, 1)
        rows, cols = w.shape[0] // n_blocks, w.shape[1] // n
        assert rows * n_blocks == w.shape[0] and rows % V7X_BF16_ROWS == 0 and cols % V7X_LANES == 0
        step = lambda *ids: jnp.minimum(step_of(*ids), n_blocks - 1)
        in_specs.append(pl.BlockSpec((rows, cols), lambda *ids, step=step, k=k: (step(*ids), k)))
        out_specs.append(pl.BlockSpec((rows, cols), lambda *ids, step=step: (step(*ids), 0)))
        out_shape.append(jax.ShapeDtypeStruct((w.shape[0], cols), BF16))
    return in_specs, out_specs, out_shape


def _cast_args(cast):
    return [item[0] if isinstance(item, tuple) else item for item in cast]


def _cast_blocks(cast_in, cast_out):
    for src_ref, dst_ref in zip(cast_in, cast_out):
        dst_ref[...] = src_ref[...].astype(BF16)


def _inproj_kernel(*refs, emit, n_cast):
    x_ref, g_ref, w_ref, wl_ref = refs[:4]
    cast_in = refs[4:4 + n_cast]
    n_out = 4 if emit else 2
    z_ref, alow_ref = refs[4 + n_cast:6 + n_cast]
    emit_refs = refs[6 + n_cast:4 + n_cast + n_out]
    cast_out = refs[4 + n_cast + n_out:4 + 2 * n_cast + n_out]
    xb_scr, r_scr = refs[4 + 2 * n_cast + n_out:6 + 2 * n_cast + n_out]
    j = pl.program_id(1)

    if not emit:
        w_ring, w_sem, x_buf, x_sem = refs[6 + 2 * n_cast + n_out:]
        tn = w_ring.shape[1]
        n_j = pl.num_programs(1)
        n_steps = pl.num_programs(0) * n_j
        step = pl.program_id(0) * n_j + j

        def tile_copy(s):
            slot = lax.rem(s, W_RING_SLOTS)
            row = pl.multiple_of(lax.rem(s, n_j) * tn, tn)
            return pltpu.make_async_copy(w_ref.at[pl.ds(row, tn), :], w_ring.at[slot], w_sem.at[slot])

        @pl.when(step == 0)
        def _():
            for s in range(W_RING_SLOTS - 1):
                tile_copy(s).start()

        @pl.when(step + (W_RING_SLOTS - 1) < n_steps)
        def _():
            tile_copy(step + (W_RING_SLOTS - 1)).start()

        tile_copy(step).wait()
        w_slot = lax.rem(step, W_RING_SLOTS)

        tm = x_buf.shape[1]
        i = pl.program_id(0)

        def rows_copy(t):
            slot = lax.rem(t, 2)
            row = pl.multiple_of(t * tm, tm)
            return pltpu.make_async_copy(x_ref.at[pl.ds(row, tm), :], x_buf.at[slot], x_sem.at[slot])

        @pl.when(step == 0)
        def _():
            rows_copy(0).start()

        @pl.when(jnp.logical_and(j == 1, i + 1 < pl.num_programs(0)))
        def _():
            rows_copy(i + 1).start()

        @pl.when(j == 0)
        def _():
            rows_copy(i).wait()

        x_slot = lax.rem(i, 2)

    def weight(ref, emit_ref):
        if not emit:
            return ref[...]
        w16 = ref[...].astype(BF16)
        emit_ref[...] = w16
        return w16

    def z_tile(xb, r):
        w16 = weight(w_ref, emit_refs[0]) if emit else w_ring[w_slot]
        return r * _dot_nt(xb, w16)

    @pl.when(j == 0)
    def _():
        _cast_blocks(cast_in, cast_out)
        x = x_ref[...] if emit else x_buf[x_slot]
        xb = (x * g_ref[...]).astype(BF16)
        xb_scr[...] = xb
        r = lax.rsqrt(jnp.mean(x * x, axis=-1, keepdims=True) + EPS)
        r_scr[...] = r
        alow_ref[...] = r * _dot_nt(xb, weight(wl_ref, emit_refs[1] if emit else None))
        z_ref[...] = z_tile(xb, r).astype(BF16)

    @pl.when(j > 0)
    def _():
        _cast_blocks(cast_in, cast_out)
        z_ref[...] = z_tile(xb_scr[...], r_scr[...]).astype(BF16)


def _inproj(x, g_mix, w, wl, *, tm, tn, emit, cast=()):
    m = x.shape[0]
    n_a = Z_GLA_WIDTH // tn
    n_j = Z_WIDTH // tn
    if emit:
        def w_row(j):
            return pl.multiple_of(jnp.where(j < n_a, j * tn, IN_OFFSETS[5] + (j - n_a) * tn), V7X_BF16_ROWS)
        w_spec = pl.BlockSpec((pl.Element(tn), pl.Element(D_MODEL)), lambda i, j: (w_row(j), 0))
        wl_spec = pl.BlockSpec((V7X_LANES, D_MODEL), lambda i, j: (IN_OFFSETS[4] // V7X_LANES, 0))
        assert m == tm, "every weight tile must be visited exactly once"
    else:
        w_spec = pl.BlockSpec(memory_space=pl.ANY)
        wl_spec = pl.BlockSpec((V7X_LANES, D_MODEL), lambda i, j: (0, 0))
        assert (m // tm) * n_j >= W_RING_SLOTS - 1
    cast_in, cast_out, cast_shape = _cast_specs(cast, N_CAST_BLOCKS, lambda i, j: i * n_j + j)
    assert not cast or (m // tm) * n_j >= N_CAST_BLOCKS
    out_specs = [
        pl.BlockSpec((tm, tn), lambda i, j: (i, j)),
        pl.BlockSpec((tm, V7X_LANES), lambda i, j: (i, 0)),
    ]
    out_shape = [
        jax.ShapeDtypeStruct((m, Z_WIDTH), BF16),
        jax.ShapeDtypeStruct((m, V7X_LANES), F32),
    ]
    if emit:
        out_specs += [pl.BlockSpec((tn, D_MODEL), lambda i, j: (j, 0)),
                      pl.BlockSpec((V7X_LANES, D_MODEL), lambda i, j: (0, 0))]
        out_shape += [jax.ShapeDtypeStruct((Z_WIDTH, D_MODEL), BF16),
                      jax.ShapeDtypeStruct((V7X_LANES, D_MODEL), BF16)]
    x_spec = pl.BlockSpec((tm, D_MODEL), lambda i, j: (i, 0)) if emit else pl.BlockSpec(memory_space=pl.ANY)
    ring = [] if emit else [pltpu.VMEM((W_RING_SLOTS, tn, D_MODEL), BF16), pltpu.SemaphoreType.DMA((W_RING_SLOTS,)),
                            pltpu.VMEM((2, tm, D_MODEL), F32), pltpu.SemaphoreType.DMA((2,))]
    return pl.pallas_call(
        functools.partial(_inproj_kernel, emit=emit, n_cast=len(cast)),
        grid=(m // tm, n_j),
        in_specs=[
            x_spec,
            pl.BlockSpec((1, D_MODEL), lambda i, j: (0, 0)),
            w_spec,
            wl_spec,
        ] + cast_in,
        out_specs=out_specs + cast_out,
        out_shape=out_shape + cast_shape,
        scratch_shapes=[pltpu.VMEM((tm, D_MODEL), BF16), pltpu.VMEM((tm, 1), F32)] + ring,
        compiler_params=_params("parallel" if emit else "arbitrary", "arbitrary"),
        name="inproj",
    )(x, g_mix, w, wl, *_cast_args(cast))


def _group_mask(R, C):
    t = np.arange(R)[:, None]
    s = np.arange(R)[None, :]
    return ((t // C == s // C) & (s <= t)).astype(np.float32)


def _gla_tile(q_ref, k_ref, v_ref, r_ref, alow_ref, tri_ref, mask_ref, wa2_ref, ba_ref, gn_ref,
              o_ref, s_out_ref, b_scr, e_scr, *, C, s_scr=None, s0_ref=None):
    sequential = s_scr is not None
    R = q_ref.shape[0]
    n = R // C
    mid = C // 2
    W = GLA_KEY_WIDTH

    x = _dot(alow_ref[...].astype(BF16), wa2_ref[...].astype(BF16)) + ba_ref[...]
    lb = (jnp.minimum(x, 0.0) - jnp.log(1.0 + jnp.exp(-jnp.abs(x)))) * (1.0 / GLA_TAU)

    tri = tri_ref[...]
    lb_hi = lb.astype(BF16)
    lb_lo = (lb - lb_hi.astype(F32)).astype(BF16)
    b = _dot(tri, lb_hi) + _dot(tri, lb_lo)
    b3 = b.reshape(n, C, W)
    b_mid = jnp.broadcast_to(b3[:, mid:mid + 1, :], (n, C, W)).reshape(R, W)
    b_last = jnp.broadcast_to(b3[:, C - 1:C, :], (n, C, W)).reshape(R, W)

    q = q_ref[...].astype(F32) * (GLA_DK ** -0.5)
    k = k_ref[...].astype(F32)
    q1 = (q * jnp.exp(b)).astype(BF16)
    q2 = (q * jnp.exp(b - b_mid)).astype(BF16)
    k2 = (k * jnp.exp(b_mid - b)).astype(BF16)
    k3 = k * jnp.exp(b_last - b)

    e_scr[...] = jnp.zeros((V7X_LANES, GLA_DK), F32)
    for h in range(GLA_HEADS):
        b_scr[h] = b[:, h * GLA_DK:(h + 1) * GLA_DK]
        e_scr[h * n:(h + 1) * n, :] = jnp.exp(b_scr[h, pl.ds(C - 1, n, stride=C), :])
    d_cols = e_scr[...].T

    causal = mask_ref[...] > 0.0
    if not sequential:
        grp = lax.broadcasted_iota(jnp.int32, (R, 1), 0) >> (C.bit_length() - 1)

    for h in range(GLA_HEADS):
        ks = slice(h * GLA_DK, (h + 1) * GLA_DK)
        vs = slice(h * GLA_DV, (h + 1) * GLA_DV)
        v = v_ref[:, vs]
        att = jnp.where(causal, _dot_nt(q2[:, ks], k2[:, ks]), 0.0).astype(BF16)
        o = _dot(att, v)
        q1h = q1[:, ks]
        k3h = k3[:, ks]
        if sequential:
            s = s_scr[h]
            o_parts = []
            for g in range(n):
                rows = slice(g * C, (g + 1) * C)
                o_parts.append(o[rows] + _dot(q1h[rows], s.astype(BF16)))
                s = d_cols[:, h * n + g:h * n + g + 1] * s + _dot_tn(k3h[rows].astype(BF16), v[rows])
            s_scr[h] = s
            s_out_ref[0, h] = s
            o = jnp.concatenate(o_parts, axis=0)
        else:
            for g in range(n):
                s = s0_ref[g, h]
                o = o + jnp.where(grp == g, _dot(q1h, s.astype(BF16)), 0.0)
                k3g = jnp.where(grp == g, k3h, 0.0).astype(BF16)
                s_out_ref[g, h] = d_cols[:, h * n + g:h * n + g + 1] * s + _dot_tn(k3g, v)
        o = _rms(o, gn_ref[:, vs]) * jax.nn.silu(r_ref[:, vs].astype(F32))
        o_ref[:, vs] = o.astype(BF16)


N_GLA_TILE_IN = 7


def _gla_kernel(*refs, C_seq, C_step, n_cast):
    n = N_GLA_TILE_IN
    seq_in, step_in = refs[:n], refs[n:2 * n]
    s0_ref, wa2_ref, ba_ref, gn_ref = refs[2 * n:2 * n + 4]
    cast_in = refs[2 * n + 4:2 * n + 4 + n_cast]
    outs = refs[2 * n + 4 + n_cast:]
    o_seq, s_seq, o_step, s_step = outs[:4]
    cast_out = outs[4:4 + n_cast]
    s_scr, b_seq, e_seq, b_step, e_step = outs[4 + n_cast:]
    shared = (wa2_ref, ba_ref, gn_ref)

    @pl.when(pl.program_id(1) == 0)
    def _():
        s_scr[...] = jnp.zeros((GLA_HEADS, GLA_DK, GLA_DV), F32)

    _cast_blocks(cast_in, cast_out)
    _gla_tile(*seq_in, *shared, o_seq, s_seq, b_seq, e_seq, C=C_seq, s_scr=s_scr)
    _gla_tile(*step_in, *shared, o_step, s_step, b_step, e_step, C=C_step, s0_ref=s0_ref)


def _gla(zp, alow_p, zs, alow_s, s0, wa2_pad, b_a, g_norm, *, n_seq_p, len_p, n_seq_s, len_s, cast=()):
    R_p = GLA_SEQ_ROWS
    nt = len_p // R_p
    n_steps = n_seq_p * nt
    R_s = zs.shape[0] // n_steps
    assert R_s * n_steps == zs.shape[0] and R_s % len_s == 0 and R_s % V7X_BF16_ROWS == 0
    step_of = lambda a, c: a * nt + c
    const = lambda a, c: (0, 0)

    def tile_specs(R):
        return [
            pl.BlockSpec((R, GLA_KEY_WIDTH), lambda a, c: (step_of(a, c), 0)),
            pl.BlockSpec((R, GLA_KEY_WIDTH), lambda a, c: (step_of(a, c), 1)),
            pl.BlockSpec((R, GLA_WIDTH), lambda a, c: (step_of(a, c), 1)),
            pl.BlockSpec((R, GLA_WIDTH), lambda a, c: (step_of(a, c), 2)),
            pl.BlockSpec((R, V7X_LANES), lambda a, c: (step_of(a, c), 0)),
            pl.BlockSpec((R, R), const),
            pl.BlockSpec((R, R), const),
        ]

    def tile_args(z, alow, R, C):
        mask = _group_mask(R, C)
        return [z, z, z, z, alow, jnp.asarray(mask, BF16), jnp.asarray(mask, F32)]

    s_block_p = (1, GLA_HEADS, GLA_DK, GLA_DV)
    s_block_s = (R_s // len_s, GLA_HEADS, GLA_DK, GLA_DV)
    state_p = pl.BlockSpec(s_block_p, lambda a, c: (a, 0, 0, 0))
    state_s = pl.BlockSpec(s_block_s, lambda a, c: (step_of(a, c), 0, 0, 0))
    cast_in, cast_out, cast_shape = _cast_specs(cast, n_steps, step_of)
    scratch = [pltpu.VMEM((GLA_HEADS, GLA_DK, GLA_DV), F32)]
    for R in (R_p, R_s):
        scratch += [pltpu.VMEM((GLA_HEADS, R, GLA_DK), F32), pltpu.VMEM((V7X_LANES, GLA_DK), F32)]
    return pl.pallas_call(
        functools.partial(_gla_kernel, C_seq=GLA_CHUNK, C_step=len_s, n_cast=len(cast)),
        grid=(n_seq_p, nt),
        in_specs=tile_specs(R_p) + tile_specs(R_s) + [
            state_s,
            pl.BlockSpec((V7X_LANES, GLA_KEY_WIDTH), const),
            pl.BlockSpec((1, GLA_KEY_WIDTH), const),
            pl.BlockSpec((1, GLA_WIDTH), const),
        ] + cast_in,
        out_specs=[
            pl.BlockSpec((R_p, GLA_WIDTH), lambda a, c: (step_of(a, c), 0)),
            state_p,
            pl.BlockSpec((R_s, GLA_WIDTH), lambda a, c: (step_of(a, c), 0)),
            state_s,
        ] + cast_out,
        out_shape=[
            jax.ShapeDtypeStruct((zp.shape[0], GLA_WIDTH), BF16),
            jax.ShapeDtypeStruct((n_seq_p, GLA_HEADS, GLA_DK, GLA_DV), F32),
            jax.ShapeDtypeStruct((zs.shape[0], GLA_WIDTH), BF16),
            jax.ShapeDtypeStruct((n_seq_s, GLA_HEADS, GLA_DK, GLA_DV), F32),
        ] + cast_shape,
        scratch_shapes=scratch,
        compiler_params=_params("parallel", "arbitrary"),
        name="gla",
    )(*tile_args(zp, alow_p, R_p, GLA_CHUNK), *tile_args(zs, alow_s, R_s, len_s), s0, wa2_pad, b_a, g_norm,
      *_cast_args(cast))


def _mixout_tile(h_ref, o_ref, u_ref, vg_ref, gln_ref, bln_ref, ws_ref, bias_ref, gout_ref, wo_ref, wm_ref,
                 h1_ref, vn_ref, acc_scr, *, tm, blk):
    n_chunks = tm // GMLP_CHUNK
    col_w = D_MODEL // n_chunks
    shift = blk.bit_length() - 1
    row = lax.broadcasted_iota(jnp.int32, (GMLP_CHUNK, GMLP_CHUNK), 0)
    col = lax.broadcasted_iota(jnp.int32, (GMLP_CHUNK, GMLP_CHUNK), 1)
    causal = jnp.logical_and((row >> shift) == (col >> shift), col <= row)
    w = [jnp.where(causal, ws_ref[g], 0.0).astype(BF16) for g in range(GMLP_GROUPS)]
    m_parts = []
    for c in range(n_chunks):
        cols = slice(c * col_w, (c + 1) * col_w)
        acc_scr[:, cols] = h_ref[:, cols] + _dot(o_ref[...], wo_ref[:, cols])
        rows = pl.ds(c * GMLP_CHUNK, GMLP_CHUNK)
        vg = jax.nn.gelu(vg_ref[rows, :].astype(F32))
        mu = jnp.mean(vg, axis=-1, keepdims=True)
        vc = vg - mu
        vn = vc * lax.rsqrt(jnp.mean(vc * vc, axis=-1, keepdims=True) + EPS) * gln_ref[...] + bln_ref[...]
        if vn_ref is not None:
            vn_ref[rows, :] = vn
        vn16 = vn.astype(BF16)
        mixed = jnp.concatenate(
            [_dot(w[g], vn16[:, g * GMLP_GROUP_DIM:(g + 1) * GMLP_GROUP_DIM]) for g in range(GMLP_GROUPS)],
            axis=1) + bias_ref[...]
        mm = jax.nn.gelu(u_ref[rows, :].astype(F32)) * mixed
        m_parts.append(_rms(mm, gout_ref[...]).astype(BF16))
    h1_ref[...] = acc_scr[...] + _dot(jnp.concatenate(m_parts, axis=0), wm_ref[...])


def _mixout_kernel(hp_ref, hs_ref, op_ref, os_ref, up_ref, vgp_ref, us_ref, vgs_ref, gln_ref, bln_ref,
                   wsp_ref, biasp_ref, wss_ref, biass_ref, gout_ref, wo_ref, wm_ref, h1_ref, vn_ref, acc_scr,
                   *, tm, n_p, blk_p, blk_s):
    shared = (gln_ref, bln_ref)
    tail = (gout_ref, wo_ref, wm_ref, h1_ref)
    i = pl.program_id(0)

    @pl.when(i < n_p)
    def _():
        _mixout_tile(hp_ref, op_ref, up_ref, vgp_ref, *shared, wsp_ref, biasp_ref, *tail, None, acc_scr, tm=tm,
                     blk=blk_p)

    @pl.when(i >= n_p)
    def _():
        _mixout_tile(hs_ref, os_ref, us_ref, vgs_ref, *shared, wss_ref, biass_ref, *tail, vn_ref, acc_scr, tm=tm,
                     blk=blk_s)


def _gmlp_operands(w_s, b_s, blk):
    reps = GMLP_CHUNK // blk
    sel = jnp.asarray(np.tile(np.eye(blk, dtype=np.float32), (reps, 1)))
    w_mix = jnp.einsum("ra,gab,cb->grc", sel, w_s[:, :blk, :blk], sel, precision=lax.Precision.HIGHEST)
    bias = jnp.tile(jnp.repeat(b_s[:, :blk].T, GMLP_GROUP_DIM, axis=1), (reps, 1))
    return w_mix, bias


def _mixout(hp, hs, op, os_, zp, zs, g_ln, b_ln, w_s, b_s, g_out, w_out, *, tm, len_p, len_s):
    n_p, n_s = hp.shape[0] // tm, hs.shape[0] // tm
    blk_p, blk_s = min(len_p, GMLP_CHUNK), min(len_s, GMLP_CHUNK)
    pidx = lambda i: jnp.minimum(i, n_p - 1)
    sidx = lambda i: jnp.maximum(i - n_p, 0)
    once = pl.Buffered(1)
    vec = pl.BlockSpec((1, GMLP_WIDTH), lambda i: (0, 0))
    mixw = pl.BlockSpec((GMLP_GROUPS, GMLP_CHUNK, GMLP_CHUNK), lambda i: (0, 0, 0))
    biasb = pl.BlockSpec((GMLP_CHUNK, GMLP_WIDTH), lambda i: (0, 0))
    return pl.pallas_call(
        functools.partial(_mixout_kernel, tm=tm, n_p=n_p, blk_p=blk_p, blk_s=blk_s),
        grid=(n_p + n_s,),
        in_specs=[
            pl.BlockSpec((tm, D_MODEL), lambda i: (pidx(i), 0)),
            pl.BlockSpec((tm, D_MODEL), lambda i: (sidx(i), 0)),
            pl.BlockSpec((tm, GLA_WIDTH), lambda i: (pidx(i), 0)),
            pl.BlockSpec((tm, GLA_WIDTH), lambda i: (sidx(i), 0)),
            pl.BlockSpec((tm, GMLP_WIDTH), lambda i: (pidx(i), Z_U_OFF // GMLP_WIDTH)),
            pl.BlockSpec((tm, GMLP_WIDTH), lambda i: (pidx(i), Z_VG_OFF // GMLP_WIDTH)),
            pl.BlockSpec((tm, GMLP_WIDTH), lambda i: (sidx(i), Z_U_OFF // GMLP_WIDTH)),
            pl.BlockSpec((tm, GMLP_WIDTH), lambda i: (sidx(i), Z_VG_OFF // GMLP_WIDTH)),
            vec, vec,
            mixw, biasb, mixw, biasb,
            vec,
            pl.BlockSpec((GLA_WIDTH, D_MODEL), lambda i: (0, 0), pipeline_mode=once),
            pl.BlockSpec((GMLP_WIDTH, D_MODEL), lambda i: (1, 0), pipeline_mode=once),
        ],
        out_specs=[
            pl.BlockSpec((tm, D_MODEL), lambda i: (i, 0)),
            pl.BlockSpec((tm, GMLP_WIDTH), lambda i: (sidx(i), 0)),
        ],
        out_shape=[
            jax.ShapeDtypeStruct((hp.shape[0] + hs.shape[0], D_MODEL), F32),
            jax.ShapeDtypeStruct((hs.shape[0], GMLP_WIDTH), F32),
        ],
        scratch_shapes=[pltpu.VMEM((tm, D_MODEL), F32)],
        compiler_params=_params("arbitrary"),
        name="mixout",
    )(hp, hs, op, os_, zp, zp, zs, zs, g_ln, b_ln, *_gmlp_operands(w_s, b_s, blk_p), *_gmlp_operands(w_s, b_s, blk_s),
      g_out, w_out, w_out)


def _ffn_kernel(h_ref, g_ref, wg_ref, wu_ref, wo_ref, o_ref, hb_scr, r_scr):
    j = pl.program_id(1)

    def tile(hb, r):
        gate = r * _dot(hb, wg_ref[...])
        up = r * _dot(hb, wu_ref[...])
        return _dot((jax.nn.silu(gate) * up).astype(BF16), wo_ref[...])

    @pl.when(j == 0)
    def _():
        h = h_ref[...]
        hb = (h * g_ref[...]).astype(BF16)
        hb_scr[...] = hb
        r = lax.rsqrt(jnp.mean(h * h, axis=-1, keepdims=True) + EPS)
        r_scr[...] = r
        o_ref[...] = h + tile(hb, r)

    @pl.when(j > 0)
    def _():
        o_ref[...] += tile(hb_scr[...], r_scr[...])


def _ffn(h1, g_ffn, w_gate, w_up, w_out, tm, tf):
    m = h1.shape[0]
    return pl.pallas_call(
        _ffn_kernel,
        grid=(m // tm, D_FF // tf),
        in_specs=[
            pl.BlockSpec((tm, D_MODEL), lambda i, j: (i, 0)),
            pl.BlockSpec((1, D_MODEL), lambda i, j: (0, 0)),
            pl.BlockSpec((D_MODEL, tf), lambda i, j: (0, j)),
            pl.BlockSpec((D_MODEL, tf), lambda i, j: (0, j)),
            pl.BlockSpec((tf, D_MODEL), lambda i, j: (j, 0)),
        ],
        out_specs=pl.BlockSpec((tm, D_MODEL), lambda i, j: (i, 0)),
        out_shape=jax.ShapeDtypeStruct((m, D_MODEL), F32),
        scratch_shapes=[pltpu.VMEM((tm, D_MODEL), BF16), pltpu.VMEM((tm, 1), F32)],
        compiler_params=_params("parallel", "arbitrary"),
        name="ffn",
    )(h1, g_ffn, w_gate, w_up, w_out)


def _ple_tile(h_ref, p_ref, wp_ref, gp_ref, gg_ref, wg_ref, gf_ref, y_ref, t_scr):
    h = h_ref[...]
    r = lax.rsqrt(jnp.mean(h * h, axis=-1, keepdims=True) + EPS)
    hb = (h * gg_ref[...]).astype(BF16)
    pe = _rms(_dot(p_ref[...].astype(BF16), wp_ref[...]), gp_ref[...])
    ssq = jnp.zeros((h.shape[0], 1), F32)
    for c in range(D_MODEL // PLE_COL_CHUNK):
        cols = slice(c * PLE_COL_CHUNK, (c + 1) * PLE_COL_CHUNK)
        t = h[:, cols] + pe[:, cols] * jax.nn.sigmoid(r * _dot(hb, wg_ref[:, cols]))
        ssq = ssq + jnp.sum(t * t, axis=-1, keepdims=True)
        t_scr[:, cols] = t
    y_ref[...] = t_scr[...] * lax.rsqrt(ssq / D_MODEL + EPS) * gf_ref[...]


def _ple_kernel(h_ref, pp_ref, ps_ref, wp_ref, gp_ref, gg_ref, wg_ref, gf_ref, yp_ref, ys_ref, t_scr, *, n_p):
    params = (wp_ref, gp_ref, gg_ref, wg_ref, gf_ref)
    i = pl.program_id(0)

    @pl.when(i < n_p)
    def _():
        _ple_tile(h_ref, pp_ref, *params, yp_ref, t_scr)

    @pl.when(i >= n_p)
    def _():
        _ple_tile(h_ref, ps_ref, *params, ys_ref, t_scr)


def _ple(h, pp, ps, w_ple, g_ple, g_gate, w_gate, g_final, tm):
    n_p, n_s = pp.shape[0] // tm, ps.shape[0] // tm
    pidx = lambda i: (jnp.minimum(i, n_p - 1), 0)
    sidx = lambda i: (jnp.maximum(i - n_p, 0), 0)
    vec = pl.BlockSpec((1, D_MODEL), lambda i: (0, 0))
    return pl.pallas_call(
        functools.partial(_ple_kernel, n_p=n_p),
        grid=(n_p + n_s,),
        in_specs=[
            pl.BlockSpec((tm, D_MODEL), lambda i: (i, 0)),
            pl.BlockSpec((tm, PLE_DIM), pidx),
            pl.BlockSpec((tm, PLE_DIM), sidx),
            pl.BlockSpec((PLE_DIM, D_MODEL), lambda i: (0, 0)),
            vec, vec,
            pl.BlockSpec((D_MODEL, D_MODEL), lambda i: (0, 0), pipeline_mode=pl.Buffered(1)),
            vec,
        ],
        out_specs=[pl.BlockSpec((tm, D_MODEL), pidx), pl.BlockSpec((tm, D_MODEL), sidx)],
        out_shape=[jax.ShapeDtypeStruct((pp.shape[0], D_MODEL), F32),
                   jax.ShapeDtypeStruct((ps.shape[0], D_MODEL), F32)],
        scratch_shapes=[pltpu.VMEM((tm, D_MODEL), F32)],
        compiler_params=_params("arbitrary"),
        name="ple",
    )(h, pp, ps, w_ple, g_ple, g_gate, w_gate, g_final)


def kernel(x_prompt, x_sample, state_gla, p_prompt, p_sample, g_mix, w_in, w_a2, b_a, g_gla_norm, g_gmlp_ln,
           b_gmlp_ln, w_s, b_s, g_gmlp_out, w_out, g_ffn, w_ffn_in, w_ffn_out, w_ple, g_ple, g_ple_gate,
           w_ple_gate, g_final):
    batch, seq, _ = x_prompt.shape
    dec_batch, dec_seq, _ = x_sample.shape
    depth = w_in.shape[0]
    assert depth == 1, "one layer: the prompt / sample passes below are not chained over depth"
    row = lambda a: a.reshape(1, -1)
    wts = dict(
        g_mix=row(g_mix[0]),
        wa2_pad=jnp.pad(w_a2[0], ((0, V7X_LANES - GLA_GATE_RANK), (0, 0))),
        b_a=row(b_a[0]),
        g_gla_norm=row(g_gla_norm[0]),
        g_gmlp_ln=row(g_gmlp_ln[0]),
        b_gmlp_ln=row(b_gmlp_ln[0]),
        w_s=w_s[0],
        b_s=b_s[0],
        g_gmlp_out=row(g_gmlp_out[0]),
        g_ffn=row(g_ffn[0]),
        w_ple=w_ple[0].astype(BF16),
        g_ple=row(g_ple[0]),
        g_ple_gate=row(g_ple_gate[0]),
        g_final=row(g_final),
    )
    xp = x_prompt.reshape(batch * seq, D_MODEL)
    xs = x_sample.reshape(dec_batch * dec_seq, D_MODEL)
    wt_in = jnp.swapaxes(w_in, 1, 2)[0]
    zs, alow_s, wt16, wl16 = _inproj(xs, wts["g_mix"], wt_in, wt_in, tm=xs.shape[0], tn=INPROJ_COLS, emit=True)
    zp, alow_p, wts["w_out"], wts["w_ffn_out"], w_gate16 = _inproj(
        xp, wts["g_mix"], wt16, wl16, tm=INPROJ_ROWS, tn=INPROJ_COLS, emit=False,
        cast=(w_out[0], w_ffn_out[0], (w_ffn_in[0], 0, 2)))
    op, sp, os_, ss, w_up16, wts["w_ple_gate"] = _gla(
        zp, alow_p, zs, alow_s, state_gla[0], wts["wa2_pad"], wts["b_a"], wts["g_gla_norm"],
        n_seq_p=batch, len_p=seq, n_seq_s=dec_batch, len_s=dec_seq, cast=((w_ffn_in[0], 1, 2), w_ple_gate[0]))
    h1, vs = _mixout(xp, xs, op, os_, zp, zs, wts["g_gmlp_ln"], wts["b_gmlp_ln"], wts["w_s"], wts["b_s"],
                     wts["g_gmlp_out"], wts["w_out"], tm=MIX_ROWS, len_p=seq, len_s=dec_seq)
    h2 = _ffn(h1, wts["g_ffn"], w_gate16, w_up16, wts["w_ffn_out"], tm=FFN_ROWS, tf=FFN_COLS)
    yp, ys = _ple(h2, p_prompt[0].reshape(batch * seq, PLE_DIM), p_sample[0].reshape(dec_batch * dec_seq, PLE_DIM),
                  wts["w_ple"], wts["g_ple"], wts["g_ple_gate"], wts["w_ple_gate"], wts["g_final"], tm=PLE_ROWS)
    return (yp.reshape(batch, seq, D_MODEL), ys.reshape(dec_batch, dec_seq, D_MODEL), sp[None], ss[None],
            vs.reshape(1, dec_batch, dec_seq, GMLP_WIDTH))
```

```python
import functools

import numpy as np

import jax
import jax.numpy as jnp
from jax import lax
from jax.experimental import pallas as pl
from jax.experimental.pallas import tpu as pltpu

F32 = jnp.float32
BF16 = jnp.bfloat16

D_MODEL = 2048
GLA_WIDTH = 1024
GMLP_WIDTH = 1024
GLA_HEADS = 4
GLA_DV = 256
GLA_DK = 128
GLA_KEY_WIDTH = 512
GLA_GATE_RANK = 16
GLA_TAU = 16.0
GLA_CHUNK = 64
GMLP_GROUPS = 8
GMLP_GROUP_DIM = 128
GMLP_CHUNK = 128
D_FF = 5632
PLE_DIM = 256
EPS = 1e-6
IN_OFFSETS = (0, 512, 1024, 2048, 3072, 3088, 4112, 5136)
Z_WIDTH = 5120
Z_GLA_WIDTH = 3072
Z_U_OFF = 3072
Z_VG_OFF = 4096

V7X_LANES = 128
V7X_BF16_ROWS = 16
V7X_VMEM_BYTES = 64 * 1024 * 1024
VMEM_LIMIT_BYTES = V7X_VMEM_BYTES - 4 * 1024 * 1024
INPROJ_ROWS, INPROJ_COLS = 1024, 1024
INPROJ_COLS_RING = 1280
MIX_ROWS = 512
FFN_ROWS, FFN_COLS = 1024, 512
PLE_ROWS, PLE_COL_CHUNK = 512, 1024
GLA_SEQ_ROWS = 8 * GLA_CHUNK
N_CAST_BLOCKS = 32
W_RING_SLOTS = 3


def _params(*sem):
    return pltpu.CompilerParams(dimension_semantics=sem, vmem_limit_bytes=VMEM_LIMIT_BYTES)


def _rms(x, g):
    return x * lax.rsqrt(jnp.mean(x * x, axis=-1, keepdims=True) + EPS) * g


def _dot(a, b):
    return jnp.dot(a, b, preferred_element_type=F32)


def _dot_tn(a, b):
    return lax.dot_general(a, b, (((0,), (0,)), ((), ())), preferred_element_type=F32)


def _dot_nt(a, b):
    return lax.dot_general(a, b, (((1,), (1,)), ((), ())), preferred_element_type=F32)


def _cast_specs(cast, n_blocks, step_of):
    in_specs, out_specs, out_shape = [], [], []
    for item in cast:
        w, k, n = item if isinstance(item, tuple) else (item, 0, 1)
        rows, cols = w.shape[0] // n_blocks, w.shape[1] // n
        assert rows * n_blocks == w.shape[0] and rows % V7X_BF16_ROWS == 0 and cols % V7X_LANES == 0
        step = lambda *ids: jnp.minimum(step_of(*ids), n_blocks - 1)
        in_specs.append(pl.BlockSpec((rows, cols), lambda *ids, step=step, k=k: (step(*ids), k)))
        out_specs.append(pl.BlockSpec((rows, cols), lambda *ids, step=step: (step(*ids), 0)))
        out_shape.append(jax.ShapeDtypeStruct((w.shape[0], cols), BF16))
    return in_specs, out_specs, out_shape


def _cast_args(cast):
    return [item[0] if isinstance(item, tuple) else item for item in cast]


def _cast_blocks(cast_in, cast_out):
    for src_ref, dst_ref in zip(cast_in, cast_out):
        dst_ref[...] = src_ref[...].astype(BF16)


def _inproj_kernel(*refs, emit, n_cast):
    x_ref, g_ref, w_ref, wl_ref = refs[:4]
    cast_in = refs[4:4 + n_cast]
    n_out = 4 if emit else 2
    z_ref, alow_ref = refs[4 + n_cast:6 + n_cast]
    emit_refs = refs[6 + n_cast:4 + n_cast + n_out]
    cast_out = refs[4 + n_cast + n_out:4 + 2 * n_cast + n_out]
    xb_scr, r_scr = refs[4 + 2 * n_cast + n_out:6 + 2 * n_cast + n_out]
    j = pl.program_id(1)

    if not emit:
        w_ring, w_sem, x_buf, x_sem = refs[6 + 2 * n_cast + n_out:]
        tn = w_ring.shape[1]
        n_j = pl.num_programs(1)
        n_steps = pl.num_programs(0) * n_j
        step = pl.program_id(0) * n_j + j

        def tile_copy(s):
            slot = lax.rem(s, W_RING_SLOTS)
            row = pl.multiple_of(lax.rem(s, n_j) * tn, tn)
            return pltpu.make_async_copy(w_ref.at[pl.ds(row, tn), :], w_ring.at[slot], w_sem.at[slot])

        @pl.when(step == 0)
        def _():
            for s in range(W_RING_SLOTS - 1):
                tile_copy(s).start()

        @pl.when(step + (W_RING_SLOTS - 1) < n_steps)
        def _():
            tile_copy(step + (W_RING_SLOTS - 1)).start()

        tile_copy(step).wait()
        w_slot = lax.rem(step, W_RING_SLOTS)

        tm = x_buf.shape[1]
        i = pl.program_id(0)

        def rows_copy(t):
            slot = lax.rem(t, 2)
            row = pl.multiple_of(t * tm, tm)
            return pltpu.make_async_copy(x_ref.at[pl.ds(row, tm), :], x_buf.at[slot], x_sem.at[slot])

        @pl.when(step == 0)
        def _():
            rows_copy(0).start()

        @pl.when(jnp.logical_and(j == 1, i + 1 < pl.num_programs(0)))
        def _():
            rows_copy(i + 1).start()

        @pl.when(j == 0)
        def _():
            rows_copy(i).wait()

        x_slot = lax.rem(i, 2)

    def weight(ref, emit_ref):
        if not emit:
            return ref[...]
        w16 = ref[...].astype(BF16)
        emit_ref[...] = w16
        return w16

    def z_tile(xb, r):
        w16 = weight(w_ref, emit_refs[0]) if emit else w_ring[w_slot]
        return r * _dot_nt(xb, w16)

    @pl.when(j == 0)
    def _():
        _cast_blocks(cast_in, cast_out)
        x = x_ref[...] if emit else x_buf[x_slot]
        xb = (x * g_ref[...]).astype(BF16)
        xb_scr[...] = xb
        r = lax.rsqrt(jnp.mean(x * x, axis=-1, keepdims=True) + EPS)
        r_scr[...] = r
        alow_ref[...] = r * _dot_nt(xb, weight(wl_ref, emit_refs[1] if emit else None))
        z_ref[...] = z_tile(xb, r).astype(BF16)

    @pl.when(j > 0)
    def _():
        _cast_blocks(cast_in, cast_out)
        z_ref[...] = z_tile(xb_scr[...], r_scr[...]).astype(BF16)


def _inproj(x, g_mix, w, wl, *, tm, tn, emit, cast=()):
    m = x.shape[0]
    n_a = Z_GLA_WIDTH // tn
    n_j = Z_WIDTH // tn
    if emit:
        def w_row(j):
            return pl.multiple_of(jnp.where(j < n_a, j * tn, IN_OFFSETS[5] + (j - n_a) * tn), V7X_BF16_ROWS)
        w_spec = pl.BlockSpec((pl.Element(tn), pl.Element(D_MODEL)), lambda i, j: (w_row(j), 0))
        wl_spec = pl.BlockSpec((V7X_LANES, D_MODEL), lambda i, j: (IN_OFFSETS[4] // V7X_LANES, 0))
        assert m == tm, "every weight tile must be visited exactly once"
    else:
        w_spec = pl.BlockSpec(memory_space=pl.ANY)
        wl_spec = pl.BlockSpec((V7X_LANES, D_MODEL), lambda i, j: (0, 0))
        assert (m // tm) * n_j >= W_RING_SLOTS - 1
    cast_in, cast_out, cast_shape = _cast_specs(cast, N_CAST_BLOCKS, lambda i, j: i * n_j + j)
    assert not cast or (m // tm) * n_j >= N_CAST_BLOCKS
    out_specs = [
        pl.BlockSpec((tm, tn), lambda i, j: (i, j)),
        pl.BlockSpec((tm, V7X_LANES), lambda i, j: (i, 0)),
    ]
    out_shape = [
        jax.ShapeDtypeStruct((m, Z_WIDTH), BF16),
        jax.ShapeDtypeStruct((m, V7X_LANES), F32),
    ]
    if emit:
        out_specs += [pl.BlockSpec((tn, D_MODEL), lambda i, j: (j, 0)),
                      pl.BlockSpec((V7X_LANES, D_MODEL), lambda i, j: (0, 0))]
        out_shape += [jax.ShapeDtypeStruct((Z_WIDTH, D_MODEL), BF16),
                      jax.ShapeDtypeStruct((V7X_LANES, D_MODEL), BF16)]
    x_spec = pl.BlockSpec((tm, D_MODEL), lambda i, j: (i, 0)) if emit else pl.BlockSpec(memory_space=pl.ANY)
    ring = [] if emit else [pltpu.VMEM((W_RING_SLOTS, tn, D_MODEL), BF16), pltpu.SemaphoreType.DMA((W_RING_SLOTS,)),
                            pltpu.VMEM((2, tm, D_MODEL), F32), pltpu.SemaphoreType.DMA((2,))]
    return pl.pallas_call(
        functools.partial(_inproj_kernel, emit=emit, n_cast=len(cast)),
        grid=(m // tm, n_j),
        in_specs=[
            x_spec,
            pl.BlockSpec((1, D_MODEL), lambda i, j: (0, 0)),
            w_spec,
            wl_spec,
        ] + cast_in,
        out_specs=out_specs + cast_out,
        out_shape=out_shape + cast_shape,
        scratch_shapes=[pltpu.VMEM((tm, D_MODEL), BF16), pltpu.VMEM((tm, 1), F32)] + ring,
        compiler_params=_params("parallel" if emit else "arbitrary", "arbitrary"),
        name="inproj",
    )(x, g_mix, w, wl, *_cast_args(cast))


def _group_mask(R, C):
    t = np.arange(R)[:, None]
    s = np.arange(R)[None, :]
    return ((t // C == s // C) & (s <= t)).astype(np.float32)


def _gla_tile(q_ref, k_ref, v_ref, r_ref, alow_ref, tri_ref, mask_ref, wa2_ref, ba_ref, gn_ref,
              o_ref, s_out_ref, b_scr, e_scr, *, C, s_scr=None, s0_ref=None):
    sequential = s_scr is not None
    R = q_ref.shape[0]
    n = R // C
    mid = C // 2
    W = GLA_KEY_WIDTH

    x = _dot(alow_ref[...].astype(BF16), wa2_ref[...].astype(BF16)) + ba_ref[...]
    lb = (jnp.minimum(x, 0.0) - jnp.log(1.0 + jnp.exp(-jnp.abs(x)))) * (1.0 / GLA_TAU)

    tri = tri_ref[...]
    lb_hi = lb.astype(BF16)
    lb_lo = (lb - lb_hi.astype(F32)).astype(BF16)
    b = _dot(tri, lb_hi) + _dot(tri, lb_lo)
    b3 = b.reshape(n, C, W)
    b_mid = jnp.broadcast_to(b3[:, mid:mid + 1, :], (n, C, W)).reshape(R, W)
    b_last = jnp.broadcast_to(b3[:, C - 1:C, :], (n, C, W)).reshape(R, W)

    q = q_ref[...].astype(F32) * (GLA_DK ** -0.5)
    k = k_ref[...].astype(F32)
    q1 = (q * jnp.exp(b)).astype(BF16)
    q2 = (q * jnp.exp(b - b_mid)).astype(BF16)
    k2 = (k * jnp.exp(b_mid - b)).astype(BF16)
    k3 = k * jnp.exp(b_last - b)

    e_scr[...] = jnp.zeros((V7X_LANES, GLA_DK), F32)
    for h in range(GLA_HEADS):
        b_scr[h] = b[:, h * GLA_DK:(h + 1) * GLA_DK]
        e_scr[h * n:(h + 1) * n, :] = jnp.exp(b_scr[h, pl.ds(C - 1, n, stride=C), :])
    d_cols = e_scr[...].T

    causal = mask_ref[...] > 0.0
    if not sequential:
        grp = lax.broadcasted_iota(jnp.int32, (R, 1), 0) >> (C.bit_length() - 1)

    for h in range(GLA_HEADS):
        ks = slice(h * GLA_DK, (h + 1) * GLA_DK)
        vs = slice(h * GLA_DV, (h + 1) * GLA_DV)
        v = v_ref[:, vs]
        att = jnp.where(causal, _dot_nt(q2[:, ks], k2[:, ks]), 0.0).astype(BF16)
        o = _dot(att, v)
        q1h = q1[:, ks]
        k3h = k3[:, ks]
        if sequential:
            s = s_scr[h]
            o_parts = []
            for g in range(n):
                rows = slice(g * C, (g + 1) * C)
                o_parts.append(o[rows] + _dot(q1h[rows], s.astype(BF16)))
                s = d_cols[:, h * n + g:h * n + g + 1] * s + _dot_tn(k3h[rows].astype(BF16), v[rows])
            s_scr[h] = s
            s_out_ref[0, h] = s
            o = jnp.concatenate(o_parts, axis=0)
        else:
            for g in range(n):
                s = s0_ref[g, h]
                o = o + jnp.where(grp == g, _dot(q1h, s.astype(BF16)), 0.0)
                k3g = jnp.where(grp == g, k3h, 0.0).astype(BF16)
                s_out_ref[g, h] = d_cols[:, h * n + g:h * n + g + 1] * s + _dot_tn(k3g, v)
        o = _rms(o, gn_ref[:, vs]) * jax.nn.silu(r_ref[:, vs].astype(F32))
        o_ref[:, vs] = o.astype(BF16)


N_GLA_TILE_IN = 7


def _gla_kernel(*refs, C_seq, C_step, n_cast):
    n = N_GLA_TILE_IN
    seq_in, step_in = refs[:n], refs[n:2 * n]
    s0_ref, wa2_ref, ba_ref, gn_ref = refs[2 * n:2 * n + 4]
    cast_in = refs[2 * n + 4:2 * n + 4 + n_cast]
    outs = refs[2 * n + 4 + n_cast:]
    o_seq, s_seq, o_step, s_step = outs[:4]
    cast_out = outs[4:4 + n_cast]
    s_scr, b_seq, e_seq, b_step, e_step = outs[4 + n_cast:]
    shared = (wa2_ref, ba_ref, gn_ref)

    @pl.when(pl.program_id(1) == 0)
    def _():
        s_scr[...] = jnp.zeros((GLA_HEADS, GLA_DK, GLA_DV), F32)

    _cast_blocks(cast_in, cast_out)
    _gla_tile(*seq_in, *shared, o_seq, s_seq, b_seq, e_seq, C=C_seq, s_scr=s_scr)
    _gla_tile(*step_in, *shared, o_step, s_step, b_step, e_step, C=C_step, s0_ref=s0_ref)


def _gla(zp, alow_p, zs, alow_s, s0, wa2_pad, b_a, g_norm, *, n_seq_p, len_p, n_seq_s, len_s, cast=()):
    R_p = GLA_SEQ_ROWS
    nt = len_p // R_p
    n_steps = n_seq_p * nt
    R_s = zs.shape[0] // n_steps
    assert R_s * n_steps == zs.shape[0] and R_s % len_s == 0 and R_s % V7X_BF16_ROWS == 0
    step_of = lambda a, c: a * nt + c
    const = lambda a, c: (0, 0)

    def tile_specs(R):
        return [
            pl.BlockSpec((R, GLA_KEY_WIDTH), lambda a, c: (step_of(a, c), 0)),
            pl.BlockSpec((R, GLA_KEY_WIDTH), lambda a, c: (step_of(a, c), 1)),
            pl.BlockSpec((R, GLA_WIDTH), lambda a, c: (step_of(a, c), 1)),
            pl.BlockSpec((R, GLA_WIDTH), lambda a, c: (step_of(a, c), 2)),
            pl.BlockSpec((R, V7X_LANES), lambda a, c: (step_of(a, c), 0)),
            pl.BlockSpec((R, R), const),
            pl.BlockSpec((R, R), const),
        ]

    def tile_args(z, alow, R, C):
        mask = _group_mask(R, C)
        return [z, z, z, z, alow, jnp.asarray(mask, BF16), jnp.asarray(mask, F32)]

    s_block_p = (1, GLA_HEADS, GLA_DK, GLA_DV)
    s_block_s = (R_s // len_s, GLA_HEADS, GLA_DK, GLA_DV)
    state_p = pl.BlockSpec(s_block_p, lambda a, c: (a, 0, 0, 0))
    state_s = pl.BlockSpec(s_block_s, lambda a, c: (step_of(a, c), 0, 0, 0))
    cast_in, cast_out, cast_shape = _cast_specs(cast, n_steps, step_of)
    scratch = [pltpu.VMEM((GLA_HEADS, GLA_DK, GLA_DV), F32)]
    for R in (R_p, R_s):
        scratch += [pltpu.VMEM((GLA_HEADS, R, GLA_DK), F32), pltpu.VMEM((V7X_LANES, GLA_DK), F32)]
    return pl.pallas_call(
        functools.partial(_gla_kernel, C_seq=GLA_CHUNK, C_step=len_s, n_cast=len(cast)),
        grid=(n_seq_p, nt),
        in_specs=tile_specs(R_p) + tile_specs(R_s) + [
            state_s,
            pl.BlockSpec((V7X_LANES, GLA_KEY_WIDTH), const),
            pl.BlockSpec((1, GLA_KEY_WIDTH), const),
            pl.BlockSpec((1, GLA_WIDTH), const),
        ] + cast_in,
        out_specs=[
            pl.BlockSpec((R_p, GLA_WIDTH), lambda a, c: (step_of(a, c), 0)),
            state_p,
            pl.BlockSpec((R_s, GLA_WIDTH), lambda a, c: (step_of(a, c), 0)),
            state_s,
        ] + cast_out,
        out_shape=[
            jax.ShapeDtypeStruct((zp.shape[0], GLA_WIDTH), BF16),
            jax.ShapeDtypeStruct((n_seq_p, GLA_HEADS, GLA_DK, GLA_DV), F32),
            jax.ShapeDtypeStruct((zs.shape[0], GLA_WIDTH), BF16),
            jax.ShapeDtypeStruct((n_seq_s, GLA_HEADS, GLA_DK, GLA_DV), F32),
        ] + cast_shape,
        scratch_shapes=scratch,
        compiler_params=_params("parallel", "arbitrary"),
        name="gla",
    )(*tile_args(zp, alow_p, R_p, GLA_CHUNK), *tile_args(zs, alow_s, R_s, len_s), s0, wa2_pad, b_a, g_norm,
      *_cast_args(cast))


def _mixout_tile(h_ref, o_ref, u_ref, vg_ref, gln_ref, bln_ref, ws_ref, bias_ref, gout_ref, wo_ref, wm_ref,
                 h1_ref, vn_ref, acc_scr, *, tm, blk):
    n_chunks = tm // GMLP_CHUNK
    col_w = D_MODEL // n_chunks
    shift = blk.bit_length() - 1
    row = lax.broadcasted_iota(jnp.int32, (GMLP_CHUNK, GMLP_CHUNK), 0)
    col = lax.broadcasted_iota(jnp.int32, (GMLP_CHUNK, GMLP_CHUNK), 1)
    causal = jnp.logical_and((row >> shift) == (col >> shift), col <= row)
    w = [jnp.where(causal, ws_ref[g], 0.0).astype(BF16) for g in range(GMLP_GROUPS)]
    m_parts = []
    for c in range(n_chunks):
        cols = slice(c * col_w, (c + 1) * col_w)
        acc_scr[:, cols] = h_ref[:, cols] + _dot(o_ref[...], wo_ref[:, cols])
        rows = pl.ds(c * GMLP_CHUNK, GMLP_CHUNK)
        vg = jax.nn.gelu(vg_ref[rows, :].astype(F32))
        mu = jnp.mean(vg, axis=-1, keepdims=True)
        vc = vg - mu
        vn = vc * lax.rsqrt(jnp.mean(vc * vc, axis=-1, keepdims=True) + EPS) * gln_ref[...] + bln_ref[...]
        if vn_ref is not None:
            vn_ref[rows, :] = vn
        vn16 = vn.astype(BF16)
        mixed = jnp.concatenate(
            [_dot(w[g], vn16[:, g * GMLP_GROUP_DIM:(g + 1) * GMLP_GROUP_DIM]) for g in range(GMLP_GROUPS)],
            axis=1) + bias_ref[...]
        mm = jax.nn.gelu(u_ref[rows, :].astype(F32)) * mixed
        m_parts.append(_rms(mm, gout_ref[...]).astype(BF16))
    h1_ref[...] = acc_scr[...] + _dot(jnp.concatenate(m_parts, axis=0), wm_ref[...])


def _mixout_kernel(hp_ref, hs_ref, op_ref, os_ref, up_ref, vgp_ref, us_ref, vgs_ref, gln_ref, bln_ref,
                   wsp_ref, biasp_ref, wss_ref, biass_ref, gout_ref, wo_ref, wm_ref, h1_ref, vn_ref, acc_scr,
                   *, tm, n_p, blk_p, blk_s):
    shared = (gln_ref, bln_ref)
    tail = (gout_ref, wo_ref, wm_ref, h1_ref)
    i = pl.program_id(0)

    @pl.when(i < n_p)
    def _():
        _mixout_tile(hp_ref, op_ref, up_ref, vgp_ref, *shared, wsp_ref, biasp_ref, *tail, None, acc_scr, tm=tm,
                     blk=blk_p)

    @pl.when(i >= n_p)
    def _():
        _mixout_tile(hs_ref, os_ref, us_ref, vgs_ref, *shared, wss_ref, biass_ref, *tail, vn_ref, acc_scr, tm=tm,
                     blk=blk_s)


def _gmlp_operands(w_s, b_s, blk):
    reps = GMLP_CHUNK // blk
    sel = jnp.asarray(np.tile(np.eye(blk, dtype=np.float32), (reps, 1)))
    w_mix = jnp.einsum("ra,gab,cb->grc", sel, w_s[:, :blk, :blk], sel, precision=lax.Precision.HIGHEST)
    bias = jnp.tile(jnp.repeat(b_s[:, :blk].T, GMLP_GROUP_DIM, axis=1), (reps, 1))
    return w_mix, bias


def _mixout(hp, hs, op, os_, zp, zs, g_ln, b_ln, w_s, b_s, g_out, w_out, *, tm, len_p, len_s):
    n_p, n_s = hp.shape[0] // tm, hs.shape[0] // tm
    blk_p, blk_s = min(len_p, GMLP_CHUNK), min(len_s, GMLP_CHUNK)
    pidx = lambda i: jnp.minimum(i, n_p - 1)
    sidx = lambda i: jnp.maximum(i - n_p, 0)
    once = pl.Buffered(1)
    vec = pl.BlockSpec((1, GMLP_WIDTH), lambda i: (0, 0))
    mixw = pl.BlockSpec((GMLP_GROUPS, GMLP_CHUNK, GMLP_CHUNK), lambda i: (0, 0, 0))
    biasb = pl.BlockSpec((GMLP_CHUNK, GMLP_WIDTH), lambda i: (0, 0))
    return pl.pallas_call(
        functools.partial(_mixout_kernel, tm=tm, n_p=n_p, blk_p=blk_p, blk_s=blk_s),
        grid=(n_p + n_s,),
        in_specs=[
            pl.BlockSpec((tm, D_MODEL), lambda i: (pidx(i), 0)),
            pl.BlockSpec((tm, D_MODEL), lambda i: (sidx(i), 0)),
            pl.BlockSpec((tm, GLA_WIDTH), lambda i: (pidx(i), 0)),
            pl.BlockSpec((tm, GLA_WIDTH), lambda i: (sidx(i), 0)),
            pl.BlockSpec((tm, GMLP_WIDTH), lambda i: (pidx(i), Z_U_OFF // GMLP_WIDTH)),
            pl.BlockSpec((tm, GMLP_WIDTH), lambda i: (pidx(i), Z_VG_OFF // GMLP_WIDTH)),
            pl.BlockSpec((tm, GMLP_WIDTH), lambda i: (sidx(i), Z_U_OFF // GMLP_WIDTH)),
            pl.BlockSpec((tm, GMLP_WIDTH), lambda i: (sidx(i), Z_VG_OFF // GMLP_WIDTH)),
            vec, vec,
            mixw, biasb, mixw, biasb,
            vec,
            pl.BlockSpec((GLA_WIDTH, D_MODEL), lambda i: (0, 0), pipeline_mode=once),
            pl.BlockSpec((GMLP_WIDTH, D_MODEL), lambda i: (1, 0), pipeline_mode=once),
        ],
        out_specs=[
            pl.BlockSpec((tm, D_MODEL), lambda i: (i, 0)),
            pl.BlockSpec((tm, GMLP_WIDTH), lambda i: (sidx(i), 0)),
        ],
        out_shape=[
            jax.ShapeDtypeStruct((hp.shape[0] + hs.shape[0], D_MODEL), F32),
            jax.ShapeDtypeStruct((hs.shape[0], GMLP_WIDTH), F32),
        ],
        scratch_shapes=[pltpu.VMEM((tm, D_MODEL), F32)],
        compiler_params=_params("arbitrary"),
        name="mixout",
    )(hp, hs, op, os_, zp, zp, zs, zs, g_ln, b_ln, *_gmlp_operands(w_s, b_s, blk_p), *_gmlp_operands(w_s, b_s, blk_s),
      g_out, w_out, w_out)


def _ffn_kernel(h_ref, g_ref, wg_ref, wu_ref, wo_ref, o_ref, hb_scr, r_scr):
    j = pl.program_id(1)

    def tile(hb, r):
        gate = r * _dot(hb, wg_ref[...])
        up = r * _dot(hb, wu_ref[...])
        return _dot((jax.nn.silu(gate) * up).astype(BF16), wo_ref[...])

    @pl.when(j == 0)
    def _():
        h = h_ref[...]
        hb = (h * g_ref[...]).astype(BF16)
        hb_scr[...] = hb
        r = lax.rsqrt(jnp.mean(h * h, axis=-1, keepdims=True) + EPS)
        r_scr[...] = r
        o_ref[...] = h + tile(hb, r)

    @pl.when(j > 0)
    def _():
        o_ref[...] += tile(hb_scr[...], r_scr[...])


def _ffn(h1, g_ffn, w_gate, w_up, w_out, tm, tf):
    m = h1.shape[0]
    return pl.pallas_call(
        _ffn_kernel,
        grid=(m // tm, D_FF // tf),
        in_specs=[
            pl.BlockSpec((tm, D_MODEL), lambda i, j: (i, 0)),
            pl.BlockSpec((1, D_MODEL), lambda i, j: (0, 0)),
            pl.BlockSpec((D_MODEL, tf), lambda i, j: (0, j)),
            pl.BlockSpec((D_MODEL, tf), lambda i, j: (0, j)),
            pl.BlockSpec((tf, D_MODEL), lambda i, j: (j, 0)),
        ],
        out_specs=pl.BlockSpec((tm, D_MODEL), lambda i, j: (i, 0)),
        out_shape=jax.ShapeDtypeStruct((m, D_MODEL), F32),
        scratch_shapes=[pltpu.VMEM((tm, D_MODEL), BF16), pltpu.VMEM((tm, 1), F32)],
        compiler_params=_params("parallel", "arbitrary"),
        name="ffn",
    )(h1, g_ffn, w_gate, w_up, w_out)


def _ple_tile(h_ref, p_ref, wp_ref, gp_ref, gg_ref, wg_ref, gf_ref, y_ref, t_scr):
    h = h_ref[...]
    r = lax.rsqrt(jnp.mean(h * h, axis=-1, keepdims=True) + EPS)
    hb = (h * gg_ref[...]).astype(BF16)
    pe = _rms(_dot(p_ref[...].astype(BF16), wp_ref[...]), gp_ref[...])
    ssq = jnp.zeros((h.shape[0], 1), F32)
    for c in range(D_MODEL // PLE_COL_CHUNK):
        cols = slice(c * PLE_COL_CHUNK, (c + 1) * PLE_COL_CHUNK)
        t = h[:, cols] + pe[:, cols] * jax.nn.sigmoid(r * _dot(hb, wg_ref[:, cols]))
        ssq = ssq + jnp.sum(t * t, axis=-1, keepdims=True)
        t_scr[:, cols] = t
    y_ref[...] = t_scr[...] * lax.rsqrt(ssq / D_MODEL + EPS) * gf_ref[...]


def _ple_kernel(h_ref, pp_ref, ps_ref, wp_ref, gp_ref, gg_ref, wg_ref, gf_ref, yp_ref, ys_ref, t_scr, *, n_p):
    params = (wp_ref, gp_ref, gg_ref, wg_ref, gf_ref)
    i = pl.program_id(0)

    @pl.when(i < n_p)
    def _():
        _ple_tile(h_ref, pp_ref, *params, yp_ref, t_scr)

    @pl.when(i >= n_p)
    def _():
        _ple_tile(h_ref, ps_ref, *params, ys_ref, t_scr)


def _ple(h, pp, ps, w_ple, g_ple, g_gate, w_gate, g_final, tm):
    n_p, n_s = pp.shape[0] // tm, ps.shape[0] // tm
    pidx = lambda i: (jnp.minimum(i, n_p - 1), 0)
    sidx = lambda i: (jnp.maximum(i - n_p, 0), 0)
    vec = pl.BlockSpec((1, D_MODEL), lambda i: (0, 0))
    return pl.pallas_call(
        functools.partial(_ple_kernel, n_p=n_p),
        grid=(n_p + n_s,),
        in_specs=[
            pl.BlockSpec((tm, D_MODEL), lambda i: (i, 0)),
            pl.BlockSpec((tm, PLE_DIM), pidx),
            pl.BlockSpec((tm, PLE_DIM), sidx),
            pl.BlockSpec((PLE_DIM, D_MODEL), lambda i: (0, 0)),
            vec, vec,
            pl.BlockSpec((D_MODEL, D_MODEL), lambda i: (0, 0), pipeline_mode=pl.Buffered(1)),
            vec,
        ],
        out_specs=[pl.BlockSpec((tm, D_MODEL), pidx), pl.BlockSpec((tm, D_MODEL), sidx)],
        out_shape=[jax.ShapeDtypeStruct((pp.shape[0], D_MODEL), F32),
                   jax.ShapeDtypeStruct((ps.shape[0], D_MODEL), F32)],
        scratch_shapes=[pltpu.VMEM((tm, D_MODEL), F32)],
        compiler_params=_params("arbitrary"),
        name="ple",
    )(h, pp, ps, w_ple, g_ple, g_gate, w_gate, g_final)


def kernel(x_prompt, x_sample, state_gla, p_prompt, p_sample, g_mix, w_in, w_a2, b_a, g_gla_norm, g_gmlp_ln,
           b_gmlp_ln, w_s, b_s, g_gmlp_out, w_out, g_ffn, w_ffn_in, w_ffn_out, w_ple, g_ple, g_ple_gate,
           w_ple_gate, g_final):
    batch, seq, _ = x_prompt.shape
    dec_batch, dec_seq, _ = x_sample.shape
    depth = w_in.shape[0]
    assert depth == 1, "one layer: the prompt / sample passes below are not chained over depth"
    row = lambda a: a.reshape(1, -1)
    wts = dict(
        g_mix=row(g_mix[0]),
        wa2_pad=jnp.pad(w_a2[0], ((0, V7X_LANES - GLA_GATE_RANK), (0, 0))),
        b_a=row(b_a[0]),
        g_gla_norm=row(g_gla_norm[0]),
        g_gmlp_ln=row(g_gmlp_ln[0]),
        b_gmlp_ln=row(b_gmlp_ln[0]),
        w_s=w_s[0],
        b_s=b_s[0],
        g_gmlp_out=row(g_gmlp_out[0]),
        g_ffn=row(g_ffn[0]),
        w_ple=w_ple[0].astype(BF16),
        g_ple=row(g_ple[0]),
        g_ple_gate=row(g_ple_gate[0]),
        g_final=row(g_final),
    )
    xp = x_prompt.reshape(batch * seq, D_MODEL)
    xs = x_sample.reshape(dec_batch * dec_seq, D_MODEL)
    wt_in = jnp.swapaxes(w_in, 1, 2)[0]
    zs, alow_s, wt16, wl16 = _inproj(xs, wts["g_mix"], wt_in, wt_in, tm=xs.shape[0], tn=INPROJ_COLS, emit=True)
    zp, alow_p, wts["w_out"], wts["w_ffn_out"], w_gate16 = _inproj(
        xp, wts["g_mix"], wt16, wl16, tm=INPROJ_ROWS, tn=INPROJ_COLS_RING, emit=False,
        cast=(w_out[0], w_ffn_out[0], (w_ffn_in[0], 0, 2)))
    op, sp, os_, ss, w_up16, wts["w_ple_gate"] = _gla(
        zp, alow_p, zs, alow_s, state_gla[0], wts["wa2_pad"], wts["b_a"], wts["g_gla_norm"],
        n_seq_p=batch, len_p=seq, n_seq_s=dec_batch, len_s=dec_seq, cast=((w_ffn_in[0], 1, 2), w_ple_gate[0]))
    h1, vs = _mixout(xp, xs, op, os_, zp, zs, wts["g_gmlp_ln"], wts["b_gmlp_ln"], wts["w_s"], wts["b_s"],
                     wts["g_gmlp_out"], wts["w_out"], tm=MIX_ROWS, len_p=seq, len_s=dec_seq)
    h2 = _ffn(h1, wts["g_ffn"], w_gate16, w_up16, wts["w_ffn_out"], tm=FFN_ROWS, tf=FFN_COLS)
    yp, ys = _ple(h2, p_prompt[0].reshape(batch * seq, PLE_DIM), p_sample[0].reshape(dec_batch * dec_seq, PLE_DIM),
                  wts["w_ple"], wts["g_ple"], wts["g_ple_gate"], wts["w_ple_gate"], wts["g_final"], tm=PLE_ROWS)
    return (yp.reshape(batch, seq, D_MODEL), ys.reshape(dec_batch, dec_seq, D_MODEL), sp[None], ss[None],
            vs.reshape(1, dec_batch, dec_seq, GMLP_WIDTH))
```

```python
import functools

import numpy as np

import jax
import jax.numpy as jnp
from jax import lax
from jax.experimental import pallas as pl
from jax.experimental.pallas import tpu as pltpu

F32 = jnp.float32
BF16 = jnp.bfloat16

D_MODEL = 2048
GLA_WIDTH = 1024
GMLP_WIDTH = 1024
GLA_HEADS = 4
GLA_DV = 256
GLA_DK = 128
GLA_KEY_WIDTH = 512
GLA_GATE_RANK = 16
GLA_TAU = 16.0
GLA_CHUNK = 64
GMLP_GROUPS = 8
GMLP_GROUP_DIM = 128
GMLP_CHUNK = 128
D_FF = 5632
PLE_DIM = 256
EPS = 1e-6
IN_OFFSETS = (0, 512, 1024, 2048, 3072, 3088, 4112, 5136)
Z_WIDTH = 5120
Z_GLA_WIDTH = 3072
Z_U_OFF = 3072
Z_VG_OFF = 4096

V7X_LANES = 128
V7X_BF16_ROWS = 16
V7X_VMEM_BYTES = 64 * 1024 * 1024
VMEM_LIMIT_BYTES = V7X_VMEM_BYTES - 4 * 1024 * 1024
INPROJ_ROWS, INPROJ_COLS = 1024, 1024
INPROJ_COLS_RING = 1024
MIX_ROWS = 512
FFN_ROWS, FFN_COLS = 1024, 512
PLE_ROWS, PLE_COL_CHUNK = 512, 1024
GLA_SEQ_ROWS = 8 * GLA_CHUNK
N_CAST_BLOCKS = 32
W_RING_SLOTS = 3


def _params(*sem):
    return pltpu.CompilerParams(dimension_semantics=sem, vmem_limit_bytes=VMEM_LIMIT_BYTES)


def _rms(x, g):
    return x * lax.rsqrt(jnp.mean(x * x, axis=-1, keepdims=True) + EPS) * g


def _dot(a, b):
    return jnp.dot(a, b, preferred_element_type=F32)


def _dot_tn(a, b):
    return lax.dot_general(a, b, (((0,), (0,)), ((), ())), preferred_element_type=F32)


def _dot_nt(a, b):
    return lax.dot_general(a, b, (((1,), (1,)), ((), ())), preferred_element_type=F32)


def _cast_specs(cast, n_blocks, step_of):
    in_specs, out_specs, out_shape = [], [], []
    for item in cast:
        w, k, n = item if isinstance(item, tuple) else (item, 0, 1)
        rows, cols = w.shape[0] // n_blocks, w.shape[1] // n
        assert rows * n_blocks == w.shape[0] and rows % V7X_BF16_ROWS == 0 and cols % V7X_LANES == 0
        step = lambda *ids: jnp.minimum(step_of(*ids), n_blocks - 1)
        in_specs.append(pl.BlockSpec((rows, cols), lambda *ids, step=step, k=k: (step(*ids), k)))
        out_specs.append(pl.BlockSpec((rows, cols), lambda *ids, step=step: (step(*ids), 0)))
        out_shape.append(jax.ShapeDtypeStruct((w.shape[0], cols), BF16))
    return in_specs, out_specs, out_shape


def _cast_args(cast):
    return [item[0] if isinstance(item, tuple) else item for item in cast]


def _cast_blocks(cast_in, cast_out):
    for src_ref, dst_ref in zip(cast_in, cast_out):
        dst_ref[...] = src_ref[...].astype(BF16)


def _inproj_kernel(*refs, emit, n_cast, cast_cols=()):
    x_ref, g_ref, w_ref, wl_ref = refs[:4]
    cast_in = refs[4:4 + n_cast]
    n_out = 4 if emit else 2
    z_ref, alow_ref = refs[4 + n_cast:6 + n_cast]
    emit_refs = refs[6 + n_cast:4 + n_cast + n_out]
    cast_out = refs[4 + n_cast + n_out:4 + 2 * n_cast + n_out]
    xb_scr, r_scr = refs[4 + 2 * n_cast + n_out:6 + 2 * n_cast + n_out]
    j = pl.program_id(1)

    if not emit:
        w_ring, w_sem, x_buf, x_sem = refs[6 + 2 * n_cast + n_out:10 + 2 * n_cast + n_out]
        cast_rings = refs[10 + 2 * n_cast + n_out:]
        tn = w_ring.shape[1]
        n_j = pl.num_programs(1)
        n_steps = pl.num_programs(0) * n_j
        step = pl.program_id(0) * n_j + j

        def tile_copy(s):
            slot = lax.rem(s, W_RING_SLOTS)
            row = pl.multiple_of(lax.rem(s, n_j) * tn, tn)
            return pltpu.make_async_copy(w_ref.at[pl.ds(row, tn), :], w_ring.at[slot], w_sem.at[slot])

        @pl.when(step == 0)
        def _():
            for s in range(W_RING_SLOTS - 1):
                tile_copy(s).start()

        @pl.when(step + (W_RING_SLOTS - 1) < n_steps)
        def _():
            tile_copy(step + (W_RING_SLOTS - 1)).start()

        tile_copy(step).wait()
        w_slot = lax.rem(step, W_RING_SLOTS)

        tm = x_buf.shape[1]
        i = pl.program_id(0)

        def rows_copy(t):
            slot = lax.rem(t, 2)
            row = pl.multiple_of(t * tm, tm)
            return pltpu.make_async_copy(x_ref.at[pl.ds(row, tm), :], x_buf.at[slot], x_sem.at[slot])

        @pl.when(step == 0)
        def _():
            rows_copy(0).start()

        @pl.when(jnp.logical_and(j == 1, i + 1 < pl.num_programs(0)))
        def _():
            rows_copy(i + 1).start()

        @pl.when(j == 0)
        def _():
            rows_copy(i).wait()

        x_slot = lax.rem(i, 2)

        def cast_copy(k, b):
            ring, sem = cast_rings[2 * k], cast_rings[2 * k + 1]
            rows, cols = ring.shape[1], ring.shape[2]
            slot = lax.rem(b, W_RING_SLOTS)
            src = cast_in[k].at[pl.ds(pl.multiple_of(b * rows, 8), rows), pl.ds(cast_cols[k] * cols, cols)]
            return pltpu.make_async_copy(src, ring.at[slot], sem.at[slot])

        @pl.when(step == 0)
        def _():
            for k in range(n_cast):
                for b in range(W_RING_SLOTS - 1):
                    cast_copy(k, b).start()

        @pl.when(step + (W_RING_SLOTS - 1) < N_CAST_BLOCKS)
        def _():
            for k in range(n_cast):
                cast_copy(k, step + (W_RING_SLOTS - 1)).start()

        @pl.when(step < N_CAST_BLOCKS)
        def _():
            for k in range(n_cast):
                cast_copy(k, step).wait()

        cast_slot = lax.rem(jnp.minimum(step, N_CAST_BLOCKS - 1), W_RING_SLOTS)

    def cast_side_job():
        if emit:
            _cast_blocks(cast_in, cast_out)
        else:
            for k in range(n_cast):
                cast_out[k][...] = cast_rings[2 * k][cast_slot].astype(BF16)

    def weight(ref, emit_ref):
        if not emit:
            return ref[...]
        w16 = ref[...].astype(BF16)
        emit_ref[...] = w16
        return w16

    def z_tile(xb, r):
        w16 = weight(w_ref, emit_refs[0]) if emit else w_ring[w_slot]
        return r * _dot_nt(xb, w16)

    @pl.when(j == 0)
    def _():
        cast_side_job()
        x = x_ref[...] if emit else x_buf[x_slot]
        xb = (x * g_ref[...]).astype(BF16)
        xb_scr[...] = xb
        r = lax.rsqrt(jnp.mean(x * x, axis=-1, keepdims=True) + EPS)
        r_scr[...] = r
        alow_ref[...] = r * _dot_nt(xb, weight(wl_ref, emit_refs[1] if emit else None))
        z_ref[...] = z_tile(xb, r).astype(BF16)

    @pl.when(j > 0)
    def _():
        cast_side_job()
        z_ref[...] = z_tile(xb_scr[...], r_scr[...]).astype(BF16)


def _inproj(x, g_mix, w, wl, *, tm, tn, emit, cast=()):
    m = x.shape[0]
    n_a = Z_GLA_WIDTH // tn
    n_j = Z_WIDTH // tn
    if emit:
        def w_row(j):
            return pl.multiple_of(jnp.where(j < n_a, j * tn, IN_OFFSETS[5] + (j - n_a) * tn), V7X_BF16_ROWS)
        w_spec = pl.BlockSpec((pl.Element(tn), pl.Element(D_MODEL)), lambda i, j: (w_row(j), 0))
        wl_spec = pl.BlockSpec((V7X_LANES, D_MODEL), lambda i, j: (IN_OFFSETS[4] // V7X_LANES, 0))
        assert m == tm, "every weight tile must be visited exactly once"
    else:
        w_spec = pl.BlockSpec(memory_space=pl.ANY)
        wl_spec = pl.BlockSpec((V7X_LANES, D_MODEL), lambda i, j: (0, 0))
        assert (m // tm) * n_j >= W_RING_SLOTS - 1
    cast_in, cast_out, cast_shape = _cast_specs(cast, N_CAST_BLOCKS, lambda i, j: i * n_j + j)
    assert not cast or (m // tm) * n_j >= N_CAST_BLOCKS
    out_specs = [
        pl.BlockSpec((tm, tn), lambda i, j: (i, j)),
        pl.BlockSpec((tm, V7X_LANES), lambda i, j: (i, 0)),
    ]
    out_shape = [
        jax.ShapeDtypeStruct((m, Z_WIDTH), BF16),
        jax.ShapeDtypeStruct((m, V7X_LANES), F32),
    ]
    if emit:
        out_specs += [pl.BlockSpec((tn, D_MODEL), lambda i, j: (j, 0)),
                      pl.BlockSpec((V7X_LANES, D_MODEL), lambda i, j: (0, 0))]
        out_shape += [jax.ShapeDtypeStruct((Z_WIDTH, D_MODEL), BF16),
                      jax.ShapeDtypeStruct((V7X_LANES, D_MODEL), BF16)]
    x_spec = pl.BlockSpec((tm, D_MODEL), lambda i, j: (i, 0)) if emit else pl.BlockSpec(memory_space=pl.ANY)
    ring = [] if emit else [pltpu.VMEM((W_RING_SLOTS, tn, D_MODEL), BF16), pltpu.SemaphoreType.DMA((W_RING_SLOTS,)),
                            pltpu.VMEM((2, tm, D_MODEL), F32), pltpu.SemaphoreType.DMA((2,))]
    cast_cols = tuple(item[1] if isinstance(item, tuple) else 0 for item in cast)
    if not emit:
        ring += [t for spec in cast_in
                 for t in (pltpu.VMEM((W_RING_SLOTS,) + tuple(spec.block_shape), F32),
                           pltpu.SemaphoreType.DMA((W_RING_SLOTS,)))]
        cast_in = [pl.BlockSpec(memory_space=pl.ANY) for _ in cast]
    return pl.pallas_call(
        functools.partial(_inproj_kernel, emit=emit, n_cast=len(cast), cast_cols=cast_cols),
        grid=(m // tm, n_j),
        in_specs=[
            x_spec,
            pl.BlockSpec((1, D_MODEL), lambda i, j: (0, 0)),
            w_spec,
            wl_spec,
        ] + cast_in,
        out_specs=out_specs + cast_out,
        out_shape=out_shape + cast_shape,
        scratch_shapes=[pltpu.VMEM((tm, D_MODEL), BF16), pltpu.VMEM((tm, 1), F32)] + ring,
        compiler_params=_params("parallel" if emit else "arbitrary", "arbitrary"),
        name="inproj",
    )(x, g_mix, w, wl, *_cast_args(cast))


def _group_mask(R, C):
    t = np.arange(R)[:, None]
    s = np.arange(R)[None, :]
    return ((t // C == s // C) & (s <= t)).astype(np.float32)


def _gla_tile(q_ref, k_ref, v_ref, r_ref, alow_ref, tri_ref, mask_ref, wa2_ref, ba_ref, gn_ref,
              o_ref, s_out_ref, b_scr, e_scr, *, C, s_scr=None, s0_ref=None):
    sequential = s_scr is not None
    R = q_ref.shape[0]
    n = R // C
    mid = C // 2
    W = GLA_KEY_WIDTH

    x = _dot(alow_ref[...].astype(BF16), wa2_ref[...].astype(BF16)) + ba_ref[...]
    lb = (jnp.minimum(x, 0.0) - jnp.log(1.0 + jnp.exp(-jnp.abs(x)))) * (1.0 / GLA_TAU)

    tri = tri_ref[...]
    lb_hi = lb.astype(BF16)
    lb_lo = (lb - lb_hi.astype(F32)).astype(BF16)
    b = _dot(tri, lb_hi) + _dot(tri, lb_lo)
    b3 = b.reshape(n, C, W)
    b_mid = jnp.broadcast_to(b3[:, mid:mid + 1, :], (n, C, W)).reshape(R, W)
    b_last = jnp.broadcast_to(b3[:, C - 1:C, :], (n, C, W)).reshape(R, W)

    q = q_ref[...].astype(F32) * (GLA_DK ** -0.5)
    k = k_ref[...].astype(F32)
    q1 = (q * jnp.exp(b)).astype(BF16)
    q2 = (q * jnp.exp(b - b_mid)).astype(BF16)
    k2 = (k * jnp.exp(b_mid - b)).astype(BF16)
    k3 = k * jnp.exp(b_last - b)

    e_scr[...] = jnp.zeros((V7X_LANES, GLA_DK), F32)
    for h in range(GLA_HEADS):
        b_scr[h] = b[:, h * GLA_DK:(h + 1) * GLA_DK]
        e_scr[h * n:(h + 1) * n, :] = jnp.exp(b_scr[h, pl.ds(C - 1, n, stride=C), :])
    d_cols = e_scr[...].T

    causal = mask_ref[...] > 0.0
    if not sequential:
        grp = lax.broadcasted_iota(jnp.int32, (R, 1), 0) >> (C.bit_length() - 1)

    for h in range(GLA_HEADS):
        ks = slice(h * GLA_DK, (h + 1) * GLA_DK)
        vs = slice(h * GLA_DV, (h + 1) * GLA_DV)
        v = v_ref[:, vs]
        att = jnp.where(causal, _dot_nt(q2[:, ks], k2[:, ks]), 0.0).astype(BF16)
        o = _dot(att, v)
        q1h = q1[:, ks]
        k3h = k3[:, ks]
        if sequential:
            s = s_scr[h]
            o_parts = []
            for g in range(n):
                rows = slice(g * C, (g + 1) * C)
                o_parts.append(o[rows] + _dot(q1h[rows], s.astype(BF16)))
                s = d_cols[:, h * n + g:h * n + g + 1] * s + _dot_tn(k3h[rows].astype(BF16), v[rows])
            s_scr[h] = s
            s_out_ref[0, h] = s
            o = jnp.concatenate(o_parts, axis=0)
        else:
            for g in range(n):
                s = s0_ref[g, h]
                o = o + jnp.where(grp == g, _dot(q1h, s.astype(BF16)), 0.0)
                k3g = jnp.where(grp == g, k3h, 0.0).astype(BF16)
                s_out_ref[g, h] = d_cols[:, h * n + g:h * n + g + 1] * s + _dot_tn(k3g, v)
        o = _rms(o, gn_ref[:, vs]) * jax.nn.silu(r_ref[:, vs].astype(F32))
        o_ref[:, vs] = o.astype(BF16)


N_GLA_TILE_IN = 7


def _gla_kernel(*refs, C_seq, C_step, n_cast):
    n = N_GLA_TILE_IN
    seq_in, step_in = refs[:n], refs[n:2 * n]
    s0_ref, wa2_ref, ba_ref, gn_ref = refs[2 * n:2 * n + 4]
    cast_in = refs[2 * n + 4:2 * n + 4 + n_cast]
    outs = refs[2 * n + 4 + n_cast:]
    o_seq, s_seq, o_step, s_step = outs[:4]
    cast_out = outs[4:4 + n_cast]
    s_scr, b_seq, e_seq, b_step, e_step = outs[4 + n_cast:]
    shared = (wa2_ref, ba_ref, gn_ref)

    @pl.when(pl.program_id(1) == 0)
    def _():
        s_scr[...] = jnp.zeros((GLA_HEADS, GLA_DK, GLA_DV), F32)

    _cast_blocks(cast_in, cast_out)
    _gla_tile(*seq_in, *shared, o_seq, s_seq, b_seq, e_seq, C=C_seq, s_scr=s_scr)
    _gla_tile(*step_in, *shared, o_step, s_step, b_step, e_step, C=C_step, s0_ref=s0_ref)


def _gla(zp, alow_p, zs, alow_s, s0, wa2_pad, b_a, g_norm, *, n_seq_p, len_p, n_seq_s, len_s, cast=()):
    R_p = GLA_SEQ_ROWS
    nt = len_p // R_p
    n_steps = n_seq_p * nt
    R_s = zs.shape[0] // n_steps
    assert R_s * n_steps == zs.shape[0] and R_s % len_s == 0 and R_s % V7X_BF16_ROWS == 0
    step_of = lambda a, c: a * nt + c
    const = lambda a, c: (0, 0)

    def tile_specs(R):
        return [
            pl.BlockSpec((R, GLA_KEY_WIDTH), lambda a, c: (step_of(a, c), 0)),
            pl.BlockSpec((R, GLA_KEY_WIDTH), lambda a, c: (step_of(a, c), 1)),
            pl.BlockSpec((R, GLA_WIDTH), lambda a, c: (step_of(a, c), 1)),
            pl.BlockSpec((R, GLA_WIDTH), lambda a, c: (step_of(a, c), 2)),
            pl.BlockSpec((R, V7X_LANES), lambda a, c: (step_of(a, c), 0)),
            pl.BlockSpec((R, R), const),
            pl.BlockSpec((R, R), const),
        ]

    def tile_args(z, alow, R, C):
        mask = _group_mask(R, C)
        return [z, z, z, z, alow, jnp.asarray(mask, BF16), jnp.asarray(mask, F32)]

    s_block_p = (1, GLA_HEADS, GLA_DK, GLA_DV)
    s_block_s = (R_s // len_s, GLA_HEADS, GLA_DK, GLA_DV)
    state_p = pl.BlockSpec(s_block_p, lambda a, c: (a, 0, 0, 0))
    state_s = pl.BlockSpec(s_block_s, lambda a, c: (step_of(a, c), 0, 0, 0))
    cast_in, cast_out, cast_shape = _cast_specs(cast, n_steps, step_of)
    scratch = [pltpu.VMEM((GLA_HEADS, GLA_DK, GLA_DV), F32)]
    for R in (R_p, R_s):
        scratch += [pltpu.VMEM((GLA_HEADS, R, GLA_DK), F32), pltpu.VMEM((V7X_LANES, GLA_DK), F32)]
    return pl.pallas_call(
        functools.partial(_gla_kernel, C_seq=GLA_CHUNK, C_step=len_s, n_cast=len(cast)),
        grid=(n_seq_p, nt),
        in_specs=tile_specs(R_p) + tile_specs(R_s) + [
            state_s,
            pl.BlockSpec((V7X_LANES, GLA_KEY_WIDTH), const),
            pl.BlockSpec((1, GLA_KEY_WIDTH), const),
            pl.BlockSpec((1, GLA_WIDTH), const),
        ] + cast_in,
        out_specs=[
            pl.BlockSpec((R_p, GLA_WIDTH), lambda a, c: (step_of(a, c), 0)),
            state_p,
            pl.BlockSpec((R_s, GLA_WIDTH), lambda a, c: (step_of(a, c), 0)),
            state_s,
        ] + cast_out,
        out_shape=[
            jax.ShapeDtypeStruct((zp.shape[0], GLA_WIDTH), BF16),
            jax.ShapeDtypeStruct((n_seq_p, GLA_HEADS, GLA_DK, GLA_DV), F32),
            jax.ShapeDtypeStruct((zs.shape[0], GLA_WIDTH), BF16),
            jax.ShapeDtypeStruct((n_seq_s, GLA_HEADS, GLA_DK, GLA_DV), F32),
        ] + cast_shape,
        scratch_shapes=scratch,
        compiler_params=_params("parallel", "arbitrary"),
        name="gla",
    )(*tile_args(zp, alow_p, R_p, GLA_CHUNK), *tile_args(zs, alow_s, R_s, len_s), s0, wa2_pad, b_a, g_norm,
      *_cast_args(cast))


def _mixout_tile(h_ref, o_ref, u_ref, vg_ref, gln_ref, bln_ref, ws_ref, bias_ref, gout_ref, wo_ref, wm_ref,
                 h1_ref, vn_ref, acc_scr, *, tm, blk):
    n_chunks = tm // GMLP_CHUNK
    col_w = D_MODEL // n_chunks
    shift = blk.bit_length() - 1
    row = lax.broadcasted_iota(jnp.int32, (GMLP_CHUNK, GMLP_CHUNK), 0)
    col = lax.broadcasted_iota(jnp.int32, (GMLP_CHUNK, GMLP_CHUNK), 1)
    causal = jnp.logical_and((row >> shift) == (col >> shift), col <= row)
    w = [jnp.where(causal, ws_ref[g], 0.0).astype(BF16) for g in range(GMLP_GROUPS)]
    m_parts = []
    for c in range(n_chunks):
        cols = slice(c * col_w, (c + 1) * col_w)
        acc_scr[:, cols] = h_ref[:, cols] + _dot(o_ref[...], wo_ref[:, cols])
        rows = pl.ds(c * GMLP_CHUNK, GMLP_CHUNK)
        vg = jax.nn.gelu(vg_ref[rows, :].astype(F32))
        mu = jnp.mean(vg, axis=-1, keepdims=True)
        vc = vg - mu
        vn = vc * lax.rsqrt(jnp.mean(vc * vc, axis=-1, keepdims=True) + EPS) * gln_ref[...] + bln_ref[...]
        if vn_ref is not None:
            vn_ref[rows, :] = vn
        vn16 = vn.astype(BF16)
        mixed = jnp.concatenate(
            [_dot(w[g], vn16[:, g * GMLP_GROUP_DIM:(g + 1) * GMLP_GROUP_DIM]) for g in range(GMLP_GROUPS)],
            axis=1) + bias_ref[...]
        mm = jax.nn.gelu(u_ref[rows, :].astype(F32)) * mixed
        m_parts.append(_rms(mm, gout_ref[...]).astype(BF16))
    h1_ref[...] = acc_scr[...] + _dot(jnp.concatenate(m_parts, axis=0), wm_ref[...])


def _mixout_kernel(hp_ref, hs_ref, op_ref, os_ref, up_ref, vgp_ref, us_ref, vgs_ref, gln_ref, bln_ref,
                   wsp_ref, biasp_ref, wss_ref, biass_ref, gout_ref, wo_ref, wm_ref, h1_ref, vn_ref, acc_scr,
                   *, tm, n_p, blk_p, blk_s):
    shared = (gln_ref, bln_ref)
    tail = (gout_ref, wo_ref, wm_ref, h1_ref)
    i = pl.program_id(0)

    @pl.when(i < n_p)
    def _():
        _mixout_tile(hp_ref, op_ref, up_ref, vgp_ref, *shared, wsp_ref, biasp_ref, *tail, None, acc_scr, tm=tm,
                     blk=blk_p)

    @pl.when(i >= n_p)
    def _():
        _mixout_tile(hs_ref, os_ref, us_ref, vgs_ref, *shared, wss_ref, biass_ref, *tail, vn_ref, acc_scr, tm=tm,
                     blk=blk_s)


def _gmlp_operands(w_s, b_s, blk):
    reps = GMLP_CHUNK // blk
    sel = jnp.asarray(np.tile(np.eye(blk, dtype=np.float32), (reps, 1)))
    w_mix = jnp.einsum("ra,gab,cb->grc", sel, w_s[:, :blk, :blk], sel, precision=lax.Precision.HIGHEST)
    bias = jnp.tile(jnp.repeat(b_s[:, :blk].T, GMLP_GROUP_DIM, axis=1), (reps, 1))
    return w_mix, bias


def _mixout(hp, hs, op, os_, zp, zs, g_ln, b_ln, w_s, b_s, g_out, w_out, *, tm, len_p, len_s):
    n_p, n_s = hp.shape[0] // tm, hs.shape[0] // tm
    blk_p, blk_s = min(len_p, GMLP_CHUNK), min(len_s, GMLP_CHUNK)
    pidx = lambda i: jnp.minimum(i, n_p - 1)
    sidx = lambda i: jnp.maximum(i - n_p, 0)
    once = pl.Buffered(1)
    vec = pl.BlockSpec((1, GMLP_WIDTH), lambda i: (0, 0))
    mixw = pl.BlockSpec((GMLP_GROUPS, GMLP_CHUNK, GMLP_CHUNK), lambda i: (0, 0, 0))
    biasb = pl.BlockSpec((GMLP_CHUNK, GMLP_WIDTH), lambda i: (0, 0))
    return pl.pallas_call(
        functools.partial(_mixout_kernel, tm=tm, n_p=n_p, blk_p=blk_p, blk_s=blk_s),
        grid=(n_p + n_s,),
        in_specs=[
            pl.BlockSpec((tm, D_MODEL), lambda i: (pidx(i), 0)),
            pl.BlockSpec((tm, D_MODEL), lambda i: (sidx(i), 0)),
            pl.BlockSpec((tm, GLA_WIDTH), lambda i: (pidx(i), 0)),
            pl.BlockSpec((tm, GLA_WIDTH), lambda i: (sidx(i), 0)),
            pl.BlockSpec((tm, GMLP_WIDTH), lambda i: (pidx(i), Z_U_OFF // GMLP_WIDTH)),
            pl.BlockSpec((tm, GMLP_WIDTH), lambda i: (pidx(i), Z_VG_OFF // GMLP_WIDTH)),
            pl.BlockSpec((tm, GMLP_WIDTH), lambda i: (sidx(i), Z_U_OFF // GMLP_WIDTH)),
            pl.BlockSpec((tm, GMLP_WIDTH), lambda i: (sidx(i), Z_VG_OFF // GMLP_WIDTH)),
            vec, vec,
            mixw, biasb, mixw, biasb,
            vec,
            pl.BlockSpec((GLA_WIDTH, D_MODEL), lambda i: (0, 0), pipeline_mode=once),
            pl.BlockSpec((GMLP_WIDTH, D_MODEL), lambda i: (1, 0), pipeline_mode=once),
        ],
        out_specs=[
            pl.BlockSpec((tm, D_MODEL), lambda i: (i, 0)),
            pl.BlockSpec((tm, GMLP_WIDTH), lambda i: (sidx(i), 0)),
        ],
        out_shape=[
            jax.ShapeDtypeStruct((hp.shape[0] + hs.shape[0], D_MODEL), F32),
            jax.ShapeDtypeStruct((hs.shape[0], GMLP_WIDTH), F32),
        ],
        scratch_shapes=[pltpu.VMEM((tm, D_MODEL), F32)],
        compiler_params=_params("arbitrary"),
        name="mixout",
    )(hp, hs, op, os_, zp, zp, zs, zs, g_ln, b_ln, *_gmlp_operands(w_s, b_s, blk_p), *_gmlp_operands(w_s, b_s, blk_s),
      g_out, w_out, w_out)


def _ffn_kernel(h_ref, g_ref, wg_ref, wu_ref, wo_ref, o_ref, hb_scr, r_scr):
    j = pl.program_id(1)

    def tile(hb, r):
        gate = r * _dot(hb, wg_ref[...])
        up = r * _dot(hb, wu_ref[...])
        return _dot((jax.nn.silu(gate) * up).astype(BF16), wo_ref[...])

    @pl.when(j == 0)
    def _():
        h = h_ref[...]
        hb = (h * g_ref[...]).astype(BF16)
        hb_scr[...] = hb
        r = lax.rsqrt(jnp.mean(h * h, axis=-1, keepdims=True) + EPS)
        r_scr[...] = r
        o_ref[...] = h + tile(hb, r)

    @pl.when(j > 0)
    def _():
        o_ref[...] += tile(hb_scr[...], r_scr[...])


def _ffn(h1, g_ffn, w_gate, w_up, w_out, tm, tf):
    m = h1.shape[0]
    return pl.pallas_call(
        _ffn_kernel,
        grid=(m // tm, D_FF // tf),
        in_specs=[
            pl.BlockSpec((tm, D_MODEL), lambda i, j: (i, 0)),
            pl.BlockSpec((1, D_MODEL), lambda i, j: (0, 0)),
            pl.BlockSpec((D_MODEL, tf), lambda i, j: (0, j)),
            pl.BlockSpec((D_MODEL, tf), lambda i, j: (0, j)),
            pl.BlockSpec((tf, D_MODEL), lambda i, j: (j, 0)),
        ],
        out_specs=pl.BlockSpec((tm, D_MODEL), lambda i, j: (i, 0)),
        out_shape=jax.ShapeDtypeStruct((m, D_MODEL), F32),
        scratch_shapes=[pltpu.VMEM((tm, D_MODEL), BF16), pltpu.VMEM((tm, 1), F32)],
        compiler_params=_params("parallel", "arbitrary"),
        name="ffn",
    )(h1, g_ffn, w_gate, w_up, w_out)


def _ple_tile(h_ref, p_ref, wp_ref, gp_ref, gg_ref, wg_ref, gf_ref, y_ref, t_scr):
    h = h_ref[...]
    r = lax.rsqrt(jnp.mean(h * h, axis=-1, keepdims=True) + EPS)
    hb = (h * gg_ref[...]).astype(BF16)
    pe = _rms(_dot(p_ref[...].astype(BF16), wp_ref[...]), gp_ref[...])
    ssq = jnp.zeros((h.shape[0], 1), F32)
    for c in range(D_MODEL // PLE_COL_CHUNK):
        cols = slice(c * PLE_COL_CHUNK, (c + 1) * PLE_COL_CHUNK)
        t = h[:, cols] + pe[:, cols] * jax.nn.sigmoid(r * _dot(hb, wg_ref[:, cols]))
        ssq = ssq + jnp.sum(t * t, axis=-1, keepdims=True)
        t_scr[:, cols] = t
    y_ref[...] = t_scr[...] * lax.rsqrt(ssq / D_MODEL + EPS) * gf_ref[...]


def _ple_kernel(h_ref, pp_ref, ps_ref, wp_ref, gp_ref, gg_ref, wg_ref, gf_ref, yp_ref, ys_ref, t_scr, *, n_p):
    params = (wp_ref, gp_ref, gg_ref, wg_ref, gf_ref)
    i = pl.program_id(0)

    @pl.when(i < n_p)
    def _():
        _ple_tile(h_ref, pp_ref, *params, yp_ref, t_scr)

    @pl.when(i >= n_p)
    def _():
        _ple_tile(h_ref, ps_ref, *params, ys_ref, t_scr)


def _ple(h, pp, ps, w_ple, g_ple, g_gate, w_gate, g_final, tm):
    n_p, n_s = pp.shape[0] // tm, ps.shape[0] // tm
    pidx = lambda i: (jnp.minimum(i, n_p - 1), 0)
    sidx = lambda i: (jnp.maximum(i - n_p, 0), 0)
    vec = pl.BlockSpec((1, D_MODEL), lambda i: (0, 0))
    return pl.pallas_call(
        functools.partial(_ple_kernel, n_p=n_p),
        grid=(n_p + n_s,),
        in_specs=[
            pl.BlockSpec((tm, D_MODEL), lambda i: (i, 0)),
            pl.BlockSpec((tm, PLE_DIM), pidx),
            pl.BlockSpec((tm, PLE_DIM), sidx),
            pl.BlockSpec((PLE_DIM, D_MODEL), lambda i: (0, 0)),
            vec, vec,
            pl.BlockSpec((D_MODEL, D_MODEL), lambda i: (0, 0), pipeline_mode=pl.Buffered(1)),
            vec,
        ],
        out_specs=[pl.BlockSpec((tm, D_MODEL), pidx), pl.BlockSpec((tm, D_MODEL), sidx)],
        out_shape=[jax.ShapeDtypeStruct((pp.shape[0], D_MODEL), F32),
                   jax.ShapeDtypeStruct((ps.shape[0], D_MODEL), F32)],
        scratch_shapes=[pltpu.VMEM((tm, D_MODEL), F32)],
        compiler_params=_params("arbitrary"),
        name="ple",
    )(h, pp, ps, w_ple, g_ple, g_gate, w_gate, g_final)


def kernel(x_prompt, x_sample, state_gla, p_prompt, p_sample, g_mix, w_in, w_a2, b_a, g_gla_norm, g_gmlp_ln,
           b_gmlp_ln, w_s, b_s, g_gmlp_out, w_out, g_ffn, w_ffn_in, w_ffn_out, w_ple, g_ple, g_ple_gate,
           w_ple_gate, g_final):
    batch, seq, _ = x_prompt.shape
    dec_batch, dec_seq, _ = x_sample.shape
    depth = w_in.shape[0]
    assert depth == 1, "one layer: the prompt / sample passes below are not chained over depth"
    row = lambda a: a.reshape(1, -1)
    wts = dict(
        g_mix=row(g_mix[0]),
        wa2_pad=jnp.pad(w_a2[0], ((0, V7X_LANES - GLA_GATE_RANK), (0, 0))),
        b_a=row(b_a[0]),
        g_gla_norm=row(g_gla_norm[0]),
        g_gmlp_ln=row(g_gmlp_ln[0]),
        b_gmlp_ln=row(b_gmlp_ln[0]),
        w_s=w_s[0],
        b_s=b_s[0],
        g_gmlp_out=row(g_gmlp_out[0]),
        g_ffn=row(g_ffn[0]),
        w_ple=w_ple[0].astype(BF16),
        g_ple=row(g_ple[0]),
        g_ple_gate=row(g_ple_gate[0]),
        g_final=row(g_final),
    )
    xp = x_prompt.reshape(batch * seq, D_MODEL)
    xs = x_sample.reshape(dec_batch * dec_seq, D_MODEL)
    wt_in = jnp.swapaxes(w_in, 1, 2)[0]
    zs, alow_s, wt16, wl16 = _inproj(xs, wts["g_mix"], wt_in, wt_in, tm=xs.shape[0], tn=INPROJ_COLS, emit=True)
    zp, alow_p, wts["w_out"], wts["w_ffn_out"], w_gate16 = _inproj(
        xp, wts["g_mix"], wt16, wl16, tm=INPROJ_ROWS, tn=INPROJ_COLS_RING, emit=False,
        cast=(w_out[0], w_ffn_out[0], (w_ffn_in[0], 0, 2)))
    op, sp, os_, ss, w_up16, wts["w_ple_gate"] = _gla(
        zp, alow_p, zs, alow_s, state_gla[0], wts["wa2_pad"], wts["b_a"], wts["g_gla_norm"],
        n_seq_p=batch, len_p=seq, n_seq_s=dec_batch, len_s=dec_seq, cast=((w_ffn_in[0], 1, 2), w_ple_gate[0]))
    h1, vs = _mixout(xp, xs, op, os_, zp, zs, wts["g_gmlp_ln"], wts["b_gmlp_ln"], wts["w_s"], wts["b_s"],
                     wts["g_gmlp_out"], wts["w_out"], tm=MIX_ROWS, len_p=seq, len_s=dec_seq)
    h2 = _ffn(h1, wts["g_ffn"], w_gate16, w_up16, wts["w_ffn_out"], tm=FFN_ROWS, tf=FFN_COLS)
    yp, ys = _ple(h2, p_prompt[0].reshape(batch * seq, PLE_DIM), p_sample[0].reshape(dec_batch * dec_seq, PLE_DIM),
                  wts["w_ple"], wts["g_ple"], wts["g_ple_gate"], wts["w_ple_gate"], wts["g_final"], tm=PLE_ROWS)
    return (yp.reshape(batch, seq, D_MODEL), ys.reshape(dec_batch, dec_seq, D_MODEL), sp[None], ss[None],
            vs.reshape(1, dec_batch, dec_seq, GMLP_WIDTH))
```

```python
import functools

import numpy as np

import jax
import jax.numpy as jnp
from jax import lax
from jax.experimental import pallas as pl
from jax.experimental.pallas import tpu as pltpu

F32 = jnp.float32
BF16 = jnp.bfloat16

D_MODEL = 2048
GLA_WIDTH = 1024
GMLP_WIDTH = 1024
GLA_HEADS = 4
GLA_DV = 256
GLA_DK = 128
GLA_KEY_WIDTH = 512
GLA_GATE_RANK = 16
GLA_TAU = 16.0
GLA_CHUNK = 64
GMLP_GROUPS = 8
GMLP_GROUP_DIM = 128
GMLP_CHUNK = 128
D_FF = 5632
PLE_DIM = 256
EPS = 1e-6
IN_OFFSETS = (0, 512, 1024, 2048, 3072, 3088, 4112, 5136)
Z_WIDTH = 5120
Z_GLA_WIDTH = 3072
Z_U_OFF = 3072
Z_VG_OFF = 4096

V7X_LANES = 128
V7X_BF16_ROWS = 16
V7X_VMEM_BYTES = 64 * 1024 * 1024
VMEM_LIMIT_BYTES = V7X_VMEM_BYTES - 4 * 1024 * 1024
INPROJ_ROWS, INPROJ_COLS = 1024, 1024
INPROJ_COLS_RING = 1280
MIX_ROWS = 512
FFN_ROWS, FFN_COLS = 1024, 512
PLE_ROWS, PLE_COL_CHUNK = 512, 1024
GLA_SEQ_ROWS = 8 * GLA_CHUNK
N_CAST_BLOCKS = 32
W_RING_SLOTS = 3


def _params(*sem):
    return pltpu.CompilerParams(dimension_semantics=sem, vmem_limit_bytes=VMEM_LIMIT_BYTES)


def _rms(x, g):
    return x * lax.rsqrt(jnp.mean(x * x, axis=-1, keepdims=True) + EPS) * g


def _dot(a, b):
    return jnp.dot(a, b, preferred_element_type=F32)


def _dot_tn(a, b):
    return lax.dot_general(a, b, (((0,), (0,)), ((), ())), preferred_element_type=F32)


def _dot_nt(a, b):
    return lax.dot_general(a, b, (((1,), (1,)), ((), ())), preferred_element_type=F32)


def _cast_specs(cast, n_blocks, step_of):
    in_specs, out_specs, out_shape = [], [], []
    for item in cast:
        w, k, n = item if isinstance(item, tuple) else (item, 0, 1)
        rows, cols = w.shape[0] // n_blocks, w.shape[1] // n
        assert rows * n_blocks == w.shape[0] and rows % V7X_BF16_ROWS == 0 and cols % V7X_LANES == 0
        step = lambda *ids: jnp.minimum(step_of(*ids), n_blocks - 1)
        in_specs.append(pl.BlockSpec((rows, cols), lambda *ids, step=step, k=k: (step(*ids), k)))
        out_specs.append(pl.BlockSpec((rows, cols), lambda *ids, step=step: (step(*ids), 0)))
        out_shape.append(jax.ShapeDtypeStruct((w.shape[0], cols), BF16))
    return in_specs, out_specs, out_shape


def _cast_args(cast):
    return [item[0] if isinstance(item, tuple) else item for item in cast]


def _cast_blocks(cast_in, cast_out):
    for src_ref, dst_ref in zip(cast_in, cast_out):
        dst_ref[...] = src_ref[...].astype(BF16)


def _inproj_kernel(*refs, emit, n_cast):
    x_ref, g_ref, w_ref, wl_ref = refs[:4]
    cast_in = refs[4:4 + n_cast]
    n_out = 4 if emit else 2
    z_ref, alow_ref = refs[4 + n_cast:6 + n_cast]
    emit_refs = refs[6 + n_cast:4 + n_cast + n_out]
    cast_out = refs[4 + n_cast + n_out:4 + 2 * n_cast + n_out]
    xb_scr, r_scr = refs[4 + 2 * n_cast + n_out:6 + 2 * n_cast + n_out]
    j = pl.program_id(1)

    if not emit:
        w_ring, w_sem, x_buf, x_sem = refs[6 + 2 * n_cast + n_out:]
        tn = w_ring.shape[1]
        n_j = pl.num_programs(1)
        n_steps = pl.num_programs(0) * n_j
        step = pl.program_id(0) * n_j + j

        def tile_copy(s):
            slot = lax.rem(s, W_RING_SLOTS)
            row = pl.multiple_of(lax.rem(s, n_j) * tn, tn)
            return pltpu.make_async_copy(w_ref.at[pl.ds(row, tn), :], w_ring.at[slot], w_sem.at[slot])

        @pl.when(step == 0)
        def _():
            for s in range(W_RING_SLOTS - 1):
                tile_copy(s).start()

        @pl.when(step + (W_RING_SLOTS - 1) < n_steps)
        def _():
            tile_copy(step + (W_RING_SLOTS - 1)).start()

        tile_copy(step).wait()
        w_slot = lax.rem(step, W_RING_SLOTS)

        tm = x_buf.shape[1]
        i = pl.program_id(0)

        def rows_copy(t):
            slot = lax.rem(t, 2)
            row = pl.multiple_of(t * tm, tm)
            return pltpu.make_async_copy(x_ref.at[pl.ds(row, tm), :], x_buf.at[slot], x_sem.at[slot])

        @pl.when(step == 0)
        def _():
            rows_copy(0).start()

        @pl.when(jnp.logical_and(j == 1, i + 1 < pl.num_programs(0)))
        def _():
            rows_copy(i + 1).start()

        @pl.when(j == 0)
        def _():
            rows_copy(i).wait()

        x_slot = lax.rem(i, 2)

    def weight(ref, emit_ref):
        if not emit:
            return ref[...]
        w16 = ref[...].astype(BF16)
        emit_ref[...] = w16
        return w16

    def z_tile(xb, r):
        w16 = weight(w_ref, emit_refs[0]) if emit else w_ring[w_slot]
        return r * _dot_nt(xb, w16)

    @pl.when(j == 0)
    def _():
        _cast_blocks(cast_in, cast_out)
        x = x_ref[...] if emit else x_buf[x_slot]
        xb = (x * g_ref[...]).astype(BF16)
        xb_scr[...] = xb
        r = lax.rsqrt(jnp.mean(x * x, axis=-1, keepdims=True) + EPS)
        r_scr[...] = r
        alow_ref[...] = r * _dot_nt(xb, weight(wl_ref, emit_refs[1] if emit else None))
        z_ref[...] = z_tile(xb, r).astype(BF16)

    @pl.when(j > 0)
    def _():
        _cast_blocks(cast_in, cast_out)
        z_ref[...] = z_tile(xb_scr[...], r_scr[...]).astype(BF16)


def _inproj(x, g_mix, w, wl, *, tm, tn, emit, cast=()):
    m = x.shape[0]
    n_a = Z_GLA_WIDTH // tn
    n_j = Z_WIDTH // tn
    if emit:
        def w_row(j):
            return pl.multiple_of(jnp.where(j < n_a, j * tn, IN_OFFSETS[5] + (j - n_a) * tn), V7X_BF16_ROWS)
        w_spec = pl.BlockSpec((pl.Element(tn), pl.Element(D_MODEL)), lambda i, j: (w_row(j), 0))
        wl_spec = pl.BlockSpec((V7X_LANES, D_MODEL), lambda i, j: (IN_OFFSETS[4] // V7X_LANES, 0))
        assert m == tm, "every weight tile must be visited exactly once"
    else:
        w_spec = pl.BlockSpec(memory_space=pl.ANY)
        wl_spec = pl.BlockSpec((V7X_LANES, D_MODEL), lambda i, j: (0, 0))
        assert (m // tm) * n_j >= W_RING_SLOTS - 1
    cast_in, cast_out, cast_shape = _cast_specs(cast, N_CAST_BLOCKS, lambda i, j: i * n_j + j)
    assert not cast or (m // tm) * n_j >= N_CAST_BLOCKS
    out_specs = [
        pl.BlockSpec((tm, tn), lambda i, j: (i, j)),
        pl.BlockSpec((tm, V7X_LANES), lambda i, j: (i, 0)),
    ]
    out_shape = [
        jax.ShapeDtypeStruct((m, Z_WIDTH), BF16),
        jax.ShapeDtypeStruct((m, V7X_LANES), F32),
    ]
    if emit:
        out_specs += [pl.BlockSpec((tn, D_MODEL), lambda i, j: (j, 0)),
                      pl.BlockSpec((V7X_LANES, D_MODEL), lambda i, j: (0, 0))]
        out_shape += [jax.ShapeDtypeStruct((Z_WIDTH, D_MODEL), BF16),
                      jax.ShapeDtypeStruct((V7X_LANES, D_MODEL), BF16)]
    x_spec = pl.BlockSpec((tm, D_MODEL), lambda i, j: (i, 0)) if emit else pl.BlockSpec(memory_space=pl.ANY)
    ring = [] if emit else [pltpu.VMEM((W_RING_SLOTS, tn, D_MODEL), BF16), pltpu.SemaphoreType.DMA((W_RING_SLOTS,)),
                            pltpu.VMEM((2, tm, D_MODEL), F32), pltpu.SemaphoreType.DMA((2,))]
    return pl.pallas_call(
        functools.partial(_inproj_kernel, emit=emit, n_cast=len(cast)),
        grid=(m // tm, n_j),
        in_specs=[
            x_spec,
            pl.BlockSpec((1, D_MODEL), lambda i, j: (0, 0)),
            w_spec,
            wl_spec,
        ] + cast_in,
        out_specs=out_specs + cast_out,
        out_shape=out_shape + cast_shape,
        scratch_shapes=[pltpu.VMEM((tm, D_MODEL), BF16), pltpu.VMEM((tm, 1), F32)] + ring,
        compiler_params=_params("parallel" if emit else "arbitrary", "arbitrary"),
        name="inproj",
    )(x, g_mix, w, wl, *_cast_args(cast))


def _group_mask(R, C):
    t = np.arange(R)[:, None]
    s = np.arange(R)[None, :]
    return ((t // C == s // C) & (s <= t)).astype(np.float32)


def _gla_tile(q_ref, k_ref, v_ref, r_ref, alow_ref, tri_ref, mask_ref, wa2_ref, ba_ref, gn_ref,
              o_ref, s_out_ref, b_scr, e_scr, *, C, s_scr=None, s0_ref=None):
    sequential = s_scr is not None
    R = q_ref.shape[0]
    n = R // C
    mid = C // 2
    W = GLA_KEY_WIDTH

    x = _dot(alow_ref[...].astype(BF16), wa2_ref[...].astype(BF16)) + ba_ref[...]
    lb = (jnp.minimum(x, 0.0) - jnp.log(1.0 + jnp.exp(-jnp.abs(x)))) * (1.0 / GLA_TAU)

    tri = tri_ref[...]
    lb_hi = lb.astype(BF16)
    lb_lo = (lb - lb_hi.astype(F32)).astype(BF16)
    b = _dot(tri, lb_hi) + _dot(tri, lb_lo)
    b3 = b.reshape(n, C, W)
    b_mid = jnp.broadcast_to(b3[:, mid:mid + 1, :], (n, C, W)).reshape(R, W)
    b_last = jnp.broadcast_to(b3[:, C - 1:C, :], (n, C, W)).reshape(R, W)

    q = q_ref[...].astype(F32) * (GLA_DK ** -0.5)
    k = k_ref[...].astype(F32)
    q1 = (q * jnp.exp(b)).astype(BF16)
    q2 = (q * jnp.exp(b - b_mid)).astype(BF16)
    k2 = (k * jnp.exp(b_mid - b)).astype(BF16)
    k3 = k * jnp.exp(b_last - b)

    e_scr[...] = jnp.zeros((V7X_LANES, GLA_DK), F32)
    for h in range(GLA_HEADS):
        b_scr[h] = b[:, h * GLA_DK:(h + 1) * GLA_DK]
        e_scr[h * n:(h + 1) * n, :] = jnp.exp(b_scr[h, pl.ds(C - 1, n, stride=C), :])
    d_cols = e_scr[...].T

    causal = mask_ref[...] > 0.0
    if not sequential:
        grp = lax.broadcasted_iota(jnp.int32, (R, 1), 0) >> (C.bit_length() - 1)

    for h in range(GLA_HEADS):
        ks = slice(h * GLA_DK, (h + 1) * GLA_DK)
        vs = slice(h * GLA_DV, (h + 1) * GLA_DV)
        v = v_ref[:, vs]
        att = jnp.where(causal, _dot_nt(q2[:, ks], k2[:, ks]), 0.0).astype(BF16)
        o = _dot(att, v)
        q1h = q1[:, ks]
        k3h = k3[:, ks]
        if sequential:
            s = s_scr[h]
            o_parts = []
            for g in range(n):
                rows = slice(g * C, (g + 1) * C)
                o_parts.append(o[rows] + _dot(q1h[rows], s.astype(BF16)))
                s = d_cols[:, h * n + g:h * n + g + 1] * s + _dot_tn(k3h[rows].astype(BF16), v[rows])
            s_scr[h] = s
            s_out_ref[0, h] = s
            o = jnp.concatenate(o_parts, axis=0)
        else:
            for g in range(n):
                s = s0_ref[g, h]
                o = o + jnp.where(grp == g, _dot(q1h, s.astype(BF16)), 0.0)
                k3g = jnp.where(grp == g, k3h, 0.0).astype(BF16)
                s_out_ref[g, h] = d_cols[:, h * n + g:h * n + g + 1] * s + _dot_tn(k3g, v)
        o = _rms(o, gn_ref[:, vs]) * jax.nn.silu(r_ref[:, vs].astype(F32))
        o_ref[:, vs] = o.astype(BF16)


N_GLA_TILE_IN = 7


def _gla_kernel(*refs, C_seq, C_step, n_cast):
    n = N_GLA_TILE_IN
    seq_in, step_in = refs[:n], refs[n:2 * n]
    s0_ref, wa2_ref, ba_ref, gn_ref = refs[2 * n:2 * n + 4]
    cast_in = refs[2 * n + 4:2 * n + 4 + n_cast]
    outs = refs[2 * n + 4 + n_cast:]
    o_seq, s_seq, o_step, s_step = outs[:4]
    cast_out = outs[4:4 + n_cast]
    s_scr, b_seq, e_seq, b_step, e_step, s0_ring, s0_sem = outs[4 + n_cast:]
    shared = (wa2_ref, ba_ref, gn_ref)

    n_steps = pl.num_programs(0) * pl.num_programs(1)
    step = pl.program_id(0) * pl.num_programs(1) + pl.program_id(1)
    n_grp = s0_ring.shape[1]

    def state_copy(t):
        slot = lax.rem(t, W_RING_SLOTS)
        return pltpu.make_async_copy(s0_ref.at[pl.ds(t * n_grp, n_grp)], s0_ring.at[slot], s0_sem.at[slot])

    @pl.when(step == 0)
    def _():
        for t in range(W_RING_SLOTS - 1):
            state_copy(t).start()

    @pl.when(step + (W_RING_SLOTS - 1) < n_steps)
    def _():
        state_copy(step + (W_RING_SLOTS - 1)).start()

    state_copy(step).wait()
    s0_tile = s0_ring.at[lax.rem(step, W_RING_SLOTS)]

    @pl.when(pl.program_id(1) == 0)
    def _():
        s_scr[...] = jnp.zeros((GLA_HEADS, GLA_DK, GLA_DV), F32)

    _cast_blocks(cast_in, cast_out)
    _gla_tile(*seq_in, *shared, o_seq, s_seq, b_seq, e_seq, C=C_seq, s_scr=s_scr)
    _gla_tile(*step_in, *shared, o_step, s_step, b_step, e_step, C=C_step, s0_ref=s0_tile)


def _gla(zp, alow_p, zs, alow_s, s0, wa2_pad, b_a, g_norm, *, n_seq_p, len_p, n_seq_s, len_s, cast=()):
    R_p = GLA_SEQ_ROWS
    nt = len_p // R_p
    n_steps = n_seq_p * nt
    R_s = zs.shape[0] // n_steps
    assert R_s * n_steps == zs.shape[0] and R_s % len_s == 0 and R_s % V7X_BF16_ROWS == 0
    step_of = lambda a, c: a * nt + c
    const = lambda a, c: (0, 0)

    def tile_specs(R):
        return [
            pl.BlockSpec((R, GLA_KEY_WIDTH), lambda a, c: (step_of(a, c), 0)),
            pl.BlockSpec((R, GLA_KEY_WIDTH), lambda a, c: (step_of(a, c), 1)),
            pl.BlockSpec((R, GLA_WIDTH), lambda a, c: (step_of(a, c), 1)),
            pl.BlockSpec((R, GLA_WIDTH), lambda a, c: (step_of(a, c), 2)),
            pl.BlockSpec((R, V7X_LANES), lambda a, c: (step_of(a, c), 0)),
            pl.BlockSpec((R, R), const),
            pl.BlockSpec((R, R), const),
        ]

    def tile_args(z, alow, R, C):
        mask = _group_mask(R, C)
        return [z, z, z, z, alow, jnp.asarray(mask, BF16), jnp.asarray(mask, F32)]

    s_block_p = (1, GLA_HEADS, GLA_DK, GLA_DV)
    s_block_s = (R_s // len_s, GLA_HEADS, GLA_DK, GLA_DV)
    state_p = pl.BlockSpec(s_block_p, lambda a, c: (a, 0, 0, 0))
    state_s = pl.BlockSpec(s_block_s, lambda a, c: (step_of(a, c), 0, 0, 0))
    cast_in, cast_out, cast_shape = _cast_specs(cast, n_steps, step_of)
    scratch = [pltpu.VMEM((GLA_HEADS, GLA_DK, GLA_DV), F32)]
    for R in (R_p, R_s):
        scratch += [pltpu.VMEM((GLA_HEADS, R, GLA_DK), F32), pltpu.VMEM((V7X_LANES, GLA_DK), F32)]
    assert n_steps >= W_RING_SLOTS - 1
    scratch += [pltpu.VMEM((W_RING_SLOTS,) + s_block_s, F32), pltpu.SemaphoreType.DMA((W_RING_SLOTS,))]
    return pl.pallas_call(
        functools.partial(_gla_kernel, C_seq=GLA_CHUNK, C_step=len_s, n_cast=len(cast)),
        grid=(n_seq_p, nt),
        in_specs=tile_specs(R_p) + tile_specs(R_s) + [
            pl.BlockSpec(memory_space=pl.ANY),
            pl.BlockSpec((V7X_LANES, GLA_KEY_WIDTH), const),
            pl.BlockSpec((1, GLA_KEY_WIDTH), const),
            pl.BlockSpec((1, GLA_WIDTH), const),
        ] + cast_in,
        out_specs=[
            pl.BlockSpec((R_p, GLA_WIDTH), lambda a, c: (step_of(a, c), 0)),
            state_p,
            pl.BlockSpec((R_s, GLA_WIDTH), lambda a, c: (step_of(a, c), 0)),
            state_s,
        ] + cast_out,
        out_shape=[
            jax.ShapeDtypeStruct((zp.shape[0], GLA_WIDTH), BF16),
            jax.ShapeDtypeStruct((n_seq_p, GLA_HEADS, GLA_DK, GLA_DV), F32),
            jax.ShapeDtypeStruct((zs.shape[0], GLA_WIDTH), BF16),
            jax.ShapeDtypeStruct((n_seq_s, GLA_HEADS, GLA_DK, GLA_DV), F32),
        ] + cast_shape,
        scratch_shapes=scratch,
        compiler_params=_params("arbitrary", "arbitrary"),
        name="gla",
    )(*tile_args(zp, alow_p, R_p, GLA_CHUNK), *tile_args(zs, alow_s, R_s, len_s), s0, wa2_pad, b_a, g_norm,
      *_cast_args(cast))


def _mixout_tile(h_ref, o_ref, u_ref, vg_ref, gln_ref, bln_ref, ws_ref, bias_ref, gout_ref, wo_ref, wm_ref,
                 h1_ref, vn_ref, acc_scr, *, tm, blk):
    n_chunks = tm // GMLP_CHUNK
    col_w = D_MODEL // n_chunks
    shift = blk.bit_length() - 1
    row = lax.broadcasted_iota(jnp.int32, (GMLP_CHUNK, GMLP_CHUNK), 0)
    col = lax.broadcasted_iota(jnp.int32, (GMLP_CHUNK, GMLP_CHUNK), 1)
    causal = jnp.logical_and((row >> shift) == (col >> shift), col <= row)
    w = [jnp.where(causal, ws_ref[g], 0.0).astype(BF16) for g in range(GMLP_GROUPS)]
    m_parts = []
    for c in range(n_chunks):
        cols = slice(c * col_w, (c + 1) * col_w)
        acc_scr[:, cols] = h_ref[:, cols] + _dot(o_ref[...], wo_ref[:, cols])
        rows = pl.ds(c * GMLP_CHUNK, GMLP_CHUNK)
        vg = jax.nn.gelu(vg_ref[rows, :].astype(F32))
        mu = jnp.mean(vg, axis=-1, keepdims=True)
        vc = vg - mu
        vn = vc * lax.rsqrt(jnp.mean(vc * vc, axis=-1, keepdims=True) + EPS) * gln_ref[...] + bln_ref[...]
        if vn_ref is not None:
            vn_ref[rows, :] = vn
        vn16 = vn.astype(BF16)
        mixed = jnp.concatenate(
            [_dot(w[g], vn16[:, g * GMLP_GROUP_DIM:(g + 1) * GMLP_GROUP_DIM]) for g in range(GMLP_GROUPS)],
            axis=1) + bias_ref[...]
        mm = jax.nn.gelu(u_ref[rows, :].astype(F32)) * mixed
        m_parts.append(_rms(mm, gout_ref[...]).astype(BF16))
    h1_ref[...] = acc_scr[...] + _dot(jnp.concatenate(m_parts, axis=0), wm_ref[...])


def _mixout_kernel(hp_ref, hs_ref, op_ref, os_ref, up_ref, vgp_ref, us_ref, vgs_ref, gln_ref, bln_ref,
                   wsp_ref, biasp_ref, wss_ref, biass_ref, gout_ref, wo_ref, wm_ref, h1_ref, vn_ref, acc_scr,
                   *, tm, n_p, blk_p, blk_s):
    shared = (gln_ref, bln_ref)
    tail = (gout_ref, wo_ref, wm_ref, h1_ref)
    i = pl.program_id(0)

    @pl.when(i < n_p)
    def _():
        _mixout_tile(hp_ref, op_ref, up_ref, vgp_ref, *shared, wsp_ref, biasp_ref, *tail, None, acc_scr, tm=tm,
                     blk=blk_p)

    @pl.when(i >= n_p)
    def _():
        _mixout_tile(hs_ref, os_ref, us_ref, vgs_ref, *shared, wss_ref, biass_ref, *tail, vn_ref, acc_scr, tm=tm,
                     blk=blk_s)


def _gmlp_operands(w_s, b_s, blk):
    reps = GMLP_CHUNK // blk
    sel = jnp.asarray(np.tile(np.eye(blk, dtype=np.float32), (reps, 1)))
    w_mix = jnp.einsum("ra,gab,cb->grc", sel, w_s[:, :blk, :blk], sel, precision=lax.Precision.HIGHEST)
    bias = jnp.tile(jnp.repeat(b_s[:, :blk].T, GMLP_GROUP_DIM, axis=1), (reps, 1))
    return w_mix, bias


def _mixout(hp, hs, op, os_, zp, zs, g_ln, b_ln, w_s, b_s, g_out, w_out, *, tm, len_p, len_s):
    n_p, n_s = hp.shape[0] // tm, hs.shape[0] // tm
    blk_p, blk_s = min(len_p, GMLP_CHUNK), min(len_s, GMLP_CHUNK)
    pidx = lambda i: jnp.minimum(i, n_p - 1)
    sidx = lambda i: jnp.maximum(i - n_p, 0)
    once = pl.Buffered(1)
    vec = pl.BlockSpec((1, GMLP_WIDTH), lambda i: (0, 0))
    mixw = pl.BlockSpec((GMLP_GROUPS, GMLP_CHUNK, GMLP_CHUNK), lambda i: (0, 0, 0))
    biasb = pl.BlockSpec((GMLP_CHUNK, GMLP_WIDTH), lambda i: (0, 0))
    return pl.pallas_call(
        functools.partial(_mixout_kernel, tm=tm, n_p=n_p, blk_p=blk_p, blk_s=blk_s),
        grid=(n_p + n_s,),
        in_specs=[
            pl.BlockSpec((tm, D_MODEL), lambda i: (pidx(i), 0)),
            pl.BlockSpec((tm, D_MODEL), lambda i: (sidx(i), 0)),
            pl.BlockSpec((tm, GLA_WIDTH), lambda i: (pidx(i), 0)),
            pl.BlockSpec((tm, GLA_WIDTH), lambda i: (sidx(i), 0)),
            pl.BlockSpec((tm, GMLP_WIDTH), lambda i: (pidx(i), Z_U_OFF // GMLP_WIDTH)),
            pl.BlockSpec((tm, GMLP_WIDTH), lambda i: (pidx(i), Z_VG_OFF // GMLP_WIDTH)),
            pl.BlockSpec((tm, GMLP_WIDTH), lambda i: (sidx(i), Z_U_OFF // GMLP_WIDTH)),
            pl.BlockSpec((tm, GMLP_WIDTH), lambda i: (sidx(i), Z_VG_OFF // GMLP_WIDTH)),
            vec, vec,
            mixw, biasb, mixw, biasb,
            vec,
            pl.BlockSpec((GLA_WIDTH, D_MODEL), lambda i: (0, 0), pipeline_mode=once),
            pl.BlockSpec((GMLP_WIDTH, D_MODEL), lambda i: (1, 0), pipeline_mode=once),
        ],
        out_specs=[
            pl.BlockSpec((tm, D_MODEL), lambda i: (i, 0)),
            pl.BlockSpec((tm, GMLP_WIDTH), lambda i: (sidx(i), 0)),
        ],
        out_shape=[
            jax.ShapeDtypeStruct((hp.shape[0] + hs.shape[0], D_MODEL), F32),
            jax.ShapeDtypeStruct((hs.shape[0], GMLP_WIDTH), F32),
        ],
        scratch_shapes=[pltpu.VMEM((tm, D_MODEL), F32)],
        compiler_params=_params("arbitrary"),
        name="mixout",
    )(hp, hs, op, os_, zp, zp, zs, zs, g_ln, b_ln, *_gmlp_operands(w_s, b_s, blk_p), *_gmlp_operands(w_s, b_s, blk_s),
      g_out, w_out, w_out)


def _ffn_kernel(h_ref, g_ref, wg_ref, wu_ref, wo_ref, o_ref, hb_scr, r_scr):
    j = pl.program_id(1)

    def tile(hb, r):
        gate = r * _dot(hb, wg_ref[...])
        up = r * _dot(hb, wu_ref[...])
        return _dot((jax.nn.silu(gate) * up).astype(BF16), wo_ref[...])

    @pl.when(j == 0)
    def _():
        h = h_ref[...]
        hb = (h * g_ref[...]).astype(BF16)
        hb_scr[...] = hb
        r = lax.rsqrt(jnp.mean(h * h, axis=-1, keepdims=True) + EPS)
        r_scr[...] = r
        o_ref[...] = h + tile(hb, r)

    @pl.when(j > 0)
    def _():
        o_ref[...] += tile(hb_scr[...], r_scr[...])


def _ffn(h1, g_ffn, w_gate, w_up, w_out, tm, tf):
    m = h1.shape[0]
    return pl.pallas_call(
        _ffn_kernel,
        grid=(m // tm, D_FF // tf),
        in_specs=[
            pl.BlockSpec((tm, D_MODEL), lambda i, j: (i, 0)),
            pl.BlockSpec((1, D_MODEL), lambda i, j: (0, 0)),
            pl.BlockSpec((D_MODEL, tf), lambda i, j: (0, j)),
            pl.BlockSpec((D_MODEL, tf), lambda i, j: (0, j)),
            pl.BlockSpec((tf, D_MODEL), lambda i, j: (j, 0)),
        ],
        out_specs=pl.BlockSpec((tm, D_MODEL), lambda i, j: (i, 0)),
        out_shape=jax.ShapeDtypeStruct((m, D_MODEL), F32),
        scratch_shapes=[pltpu.VMEM((tm, D_MODEL), BF16), pltpu.VMEM((tm, 1), F32)],
        compiler_params=_params("parallel", "arbitrary"),
        name="ffn",
    )(h1, g_ffn, w_gate, w_up, w_out)


def _ple_tile(h_ref, p_ref, wp_ref, gp_ref, gg_ref, wg_ref, gf_ref, y_ref, t_scr):
    h = h_ref[...]
    r = lax.rsqrt(jnp.mean(h * h, axis=-1, keepdims=True) + EPS)
    hb = (h * gg_ref[...]).astype(BF16)
    pe = _rms(_dot(p_ref[...].astype(BF16), wp_ref[...]), gp_ref[...])
    ssq = jnp.zeros((h.shape[0], 1), F32)
    for c in range(D_MODEL // PLE_COL_CHUNK):
        cols = slice(c * PLE_COL_CHUNK, (c + 1) * PLE_COL_CHUNK)
        t = h[:, cols] + pe[:, cols] * jax.nn.sigmoid(r * _dot(hb, wg_ref[:, cols]))
        ssq = ssq + jnp.sum(t * t, axis=-1, keepdims=True)
        t_scr[:, cols] = t
    y_ref[...] = t_scr[...] * lax.rsqrt(ssq / D_MODEL + EPS) * gf_ref[...]


def _ple_kernel(h_ref, pp_ref, ps_ref, wp_ref, gp_ref, gg_ref, wg_ref, gf_ref, yp_ref, ys_ref, t_scr, *, n_p):
    params = (wp_ref, gp_ref, gg_ref, wg_ref, gf_ref)
    i = pl.program_id(0)

    @pl.when(i < n_p)
    def _():
        _ple_tile(h_ref, pp_ref, *params, yp_ref, t_scr)

    @pl.when(i >= n_p)
    def _():
        _ple_tile(h_ref, ps_ref, *params, ys_ref, t_scr)


def _ple(h, pp, ps, w_ple, g_ple, g_gate, w_gate, g_final, tm):
    n_p, n_s = pp.shape[0] // tm, ps.shape[0] // tm
    pidx = lambda i: (jnp.minimum(i, n_p - 1), 0)
    sidx = lambda i: (jnp.maximum(i - n_p, 0), 0)
    vec = pl.BlockSpec((1, D_MODEL), lambda i: (0, 0))
    return pl.pallas_call(
        functools.partial(_ple_kernel, n_p=n_p),
        grid=(n_p + n_s,),
        in_specs=[
            pl.BlockSpec((tm, D_MODEL), lambda i: (i, 0)),
            pl.BlockSpec((tm, PLE_DIM), pidx),
            pl.BlockSpec((tm, PLE_DIM), sidx),
            pl.BlockSpec((PLE_DIM, D_MODEL), lambda i: (0, 0)),
            vec, vec,
            pl.BlockSpec((D_MODEL, D_MODEL), lambda i: (0, 0), pipeline_mode=pl.Buffered(1)),
            vec,
        ],
        out_specs=[pl.BlockSpec((tm, D_MODEL), pidx), pl.BlockSpec((tm, D_MODEL), sidx)],
        out_shape=[jax.ShapeDtypeStruct((pp.shape[0], D_MODEL), F32),
                   jax.ShapeDtypeStruct((ps.shape[0], D_MODEL), F32)],
        scratch_shapes=[pltpu.VMEM((tm, D_MODEL), F32)],
        compiler_params=_params("arbitrary"),
        name="ple",
    )(h, pp, ps, w_ple, g_ple, g_gate, w_gate, g_final)


def kernel(x_prompt, x_sample, state_gla, p_prompt, p_sample, g_mix, w_in, w_a2, b_a, g_gla_norm, g_gmlp_ln,
           b_gmlp_ln, w_s, b_s, g_gmlp_out, w_out, g_ffn, w_ffn_in, w_ffn_out, w_ple, g_ple, g_ple_gate,
           w_ple_gate, g_final):
    batch, seq, _ = x_prompt.shape
    dec_batch, dec_seq, _ = x_sample.shape
    depth = w_in.shape[0]
    assert depth == 1, "one layer: the prompt / sample passes below are not chained over depth"
    row = lambda a: a.reshape(1, -1)
    wts = dict(
        g_mix=row(g_mix[0]),
        wa2_pad=jnp.pad(w_a2[0], ((0, V7X_LANES - GLA_GATE_RANK), (0, 0))),
        b_a=row(b_a[0]),
        g_gla_norm=row(g_gla_norm[0]),
        g_gmlp_ln=row(g_gmlp_ln[0]),
        b_gmlp_ln=row(b_gmlp_ln[0]),
        w_s=w_s[0],
        b_s=b_s[0],
        g_gmlp_out=row(g_gmlp_out[0]),
        g_ffn=row(g_ffn[0]),
        w_ple=w_ple[0].astype(BF16),
        g_ple=row(g_ple[0]),
        g_ple_gate=row(g_ple_gate[0]),
        g_final=row(g_final),
    )
    xp = x_prompt.reshape(batch * seq, D_MODEL)
    xs = x_sample.reshape(dec_batch * dec_seq, D_MODEL)
    wt_in = jnp.swapaxes(w_in, 1, 2)[0]
    zs, alow_s, wt16, wl16 = _inproj(xs, wts["g_mix"], wt_in, wt_in, tm=xs.shape[0], tn=INPROJ_COLS, emit=True)
    zp, alow_p, wts["w_out"], wts["w_ffn_out"], w_gate16 = _inproj(
        xp, wts["g_mix"], wt16, wl16, tm=INPROJ_ROWS, tn=INPROJ_COLS_RING, emit=False,
        cast=(w_out[0], w_ffn_out[0], (w_ffn_in[0], 0, 2)))
    op, sp, os_, ss, w_up16, wts["w_ple_gate"] = _gla(
        zp, alow_p, zs, alow_s, state_gla[0], wts["wa2_pad"], wts["b_a"], wts["g_gla_norm"],
        n_seq_p=batch, len_p=seq, n_seq_s=dec_batch, len_s=dec_seq, cast=((w_ffn_in[0], 1, 2), w_ple_gate[0]))
    h1, vs = _mixout(xp, xs, op, os_, zp, zs, wts["g_gmlp_ln"], wts["b_gmlp_ln"], wts["w_s"], wts["b_s"],
                     wts["g_gmlp_out"], wts["w_out"], tm=MIX_ROWS, len_p=seq, len_s=dec_seq)
    h2 = _ffn(h1, wts["g_ffn"], w_gate16, w_up16, wts["w_ffn_out"], tm=FFN_ROWS, tf=FFN_COLS)
    yp, ys = _ple(h2, p_prompt[0].reshape(batch * seq, PLE_DIM), p_sample[0].reshape(dec_batch * dec_seq, PLE_DIM),
                  wts["w_ple"], wts["g_ple"], wts["g_ple_gate"], wts["w_ple_gate"], wts["g_final"], tm=PLE_ROWS)
    return (yp.reshape(batch, seq, D_MODEL), ys.reshape(dec_batch, dec_seq, D_MODEL), sp[None], ss[None],
            vs.reshape(1, dec_batch, dec_seq, GMLP_WIDTH))
```

```python
import functools

import numpy as np

import jax
import jax.numpy as jnp
from jax import lax
from jax.experimental import pallas as pl
from jax.experimental.pallas import tpu as pltpu

F32 = jnp.float32
BF16 = jnp.bfloat16

D_MODEL = 2048
GLA_WIDTH = 1024
GMLP_WIDTH = 1024
GLA_HEADS = 4
GLA_DV = 256
GLA_DK = 128
GLA_KEY_WIDTH = 512
GLA_GATE_RANK = 16
GLA_TAU = 16.0
GLA_CHUNK = 64
GMLP_GROUPS = 8
GMLP_GROUP_DIM = 128
GMLP_CHUNK = 128
D_FF = 5632
PLE_DIM = 256
EPS = 1e-6
IN_OFFSETS = (0, 512, 1024, 2048, 3072, 3088, 4112, 5136)
Z_WIDTH = 5120
Z_GLA_WIDTH = 3072
Z_U_OFF = 3072
Z_VG_OFF = 4096

V7X_LANES = 128
V7X_BF16_ROWS = 16
V7X_VMEM_BYTES = 64 * 1024 * 1024
VMEM_LIMIT_BYTES = V7X_VMEM_BYTES - 4 * 1024 * 1024
INPROJ_ROWS, INPROJ_COLS = 1024, 512
INPROJ_COLS_RING = 1280
MIX_ROWS = 512
FFN_ROWS, FFN_COLS = 1024, 512
PLE_ROWS, PLE_COL_CHUNK = 512, 1024
GLA_SEQ_ROWS = 8 * GLA_CHUNK
N_CAST_BLOCKS = 32
W_RING_SLOTS = 3


def _params(*sem):
    return pltpu.CompilerParams(dimension_semantics=sem, vmem_limit_bytes=VMEM_LIMIT_BYTES)


def _rms(x, g):
    return x * lax.rsqrt(jnp.mean(x * x, axis=-1, keepdims=True) + EPS) * g


def _dot(a, b):
    return jnp.dot(a, b, preferred_element_type=F32)


def _dot_tn(a, b):
    return lax.dot_general(a, b, (((0,), (0,)), ((), ())), preferred_element_type=F32)


def _dot_nt(a, b):
    return lax.dot_general(a, b, (((1,), (1,)), ((), ())), preferred_element_type=F32)


def _cast_specs(cast, n_blocks, step_of):
    in_specs, out_specs, out_shape = [], [], []
    for item in cast:
        w, k, n = item if isinstance(item, tuple) else (item, 0, 1)
        rows, cols = w.shape[0] // n_blocks, w.shape[1] // n
        assert rows * n_blocks == w.shape[0] and rows % V7X_BF16_ROWS == 0 and cols % V7X_LANES == 0
        step = lambda *ids: jnp.minimum(step_of(*ids), n_blocks - 1)
        in_specs.append(pl.BlockSpec((rows, cols), lambda *ids, step=step, k=k: (step(*ids), k)))
        out_specs.append(pl.BlockSpec((rows, cols), lambda *ids, step=step: (step(*ids), 0)))
        out_shape.append(jax.ShapeDtypeStruct((w.shape[0], cols), BF16))
    return in_specs, out_specs, out_shape


def _cast_args(cast):
    return [item[0] if isinstance(item, tuple) else item for item in cast]


def _cast_blocks(cast_in, cast_out):
    for src_ref, dst_ref in zip(cast_in, cast_out):
        dst_ref[...] = src_ref[...].astype(BF16)


def _inproj_kernel(*refs, emit, n_cast):
    x_ref, g_ref, w_ref, wl_ref = refs[:4]
    cast_in = refs[4:4 + n_cast]
    n_out = 4 if emit else 2
    z_ref, alow_ref = refs[4 + n_cast:6 + n_cast]
    emit_refs = refs[6 + n_cast:4 + n_cast + n_out]
    cast_out = refs[4 + n_cast + n_out:4 + 2 * n_cast + n_out]
    xb_scr, r_scr = refs[4 + 2 * n_cast + n_out:6 + 2 * n_cast + n_out]
    j = pl.program_id(1)

    if not emit:
        w_ring, w_sem, x_buf, x_sem = refs[6 + 2 * n_cast + n_out:]
        tn = w_ring.shape[1]
        n_j = pl.num_programs(1)
        n_steps = pl.num_programs(0) * n_j
        step = pl.program_id(0) * n_j + j

        def tile_copy(s):
            slot = lax.rem(s, W_RING_SLOTS)
            row = pl.multiple_of(lax.rem(s, n_j) * tn, tn)
            return pltpu.make_async_copy(w_ref.at[pl.ds(row, tn), :], w_ring.at[slot], w_sem.at[slot])

        @pl.when(step == 0)
        def _():
            for s in range(W_RING_SLOTS - 1):
                tile_copy(s).start()

        @pl.when(step + (W_RING_SLOTS - 1) < n_steps)
        def _():
            tile_copy(step + (W_RING_SLOTS - 1)).start()

        tile_copy(step).wait()
        w_slot = lax.rem(step, W_RING_SLOTS)

        tm = x_buf.shape[1]
        i = pl.program_id(0)

        def rows_copy(t):
            slot = lax.rem(t, 2)
            row = pl.multiple_of(t * tm, tm)
            return pltpu.make_async_copy(x_ref.at[pl.ds(row, tm), :], x_buf.at[slot], x_sem.at[slot])

        @pl.when(step == 0)
        def _():
            rows_copy(0).start()

        @pl.when(jnp.logical_and(j == 1, i + 1 < pl.num_programs(0)))
        def _():
            rows_copy(i + 1).start()

        @pl.when(j == 0)
        def _():
            rows_copy(i).wait()

        x_slot = lax.rem(i, 2)

    def weight(ref, emit_ref):
        if not emit:
            return ref[...]
        w16 = ref[...].astype(BF16)
        emit_ref[...] = w16
        return w16

    def z_tile(xb, r):
        w16 = weight(w_ref, emit_refs[0]) if emit else w_ring[w_slot]
        return r * _dot_nt(xb, w16)

    @pl.when(j == 0)
    def _():
        _cast_blocks(cast_in, cast_out)
        x = x_ref[...] if emit else x_buf[x_slot]
        xb = (x * g_ref[...]).astype(BF16)
        xb_scr[...] = xb
        r = lax.rsqrt(jnp.mean(x * x, axis=-1, keepdims=True) + EPS)
        r_scr[...] = r
        alow_ref[...] = r * _dot_nt(xb, weight(wl_ref, emit_refs[1] if emit else None))
        z_ref[...] = z_tile(xb, r).astype(BF16)

    @pl.when(j > 0)
    def _():
        _cast_blocks(cast_in, cast_out)
        z_ref[...] = z_tile(xb_scr[...], r_scr[...]).astype(BF16)


def _inproj(x, g_mix, w, wl, *, tm, tn, emit, cast=()):
    m = x.shape[0]
    n_a = Z_GLA_WIDTH // tn
    n_j = Z_WIDTH // tn
    if emit:
        def w_row(j):
            return pl.multiple_of(jnp.where(j < n_a, j * tn, IN_OFFSETS[5] + (j - n_a) * tn), V7X_BF16_ROWS)
        w_spec = pl.BlockSpec((pl.Element(tn), pl.Element(D_MODEL)), lambda i, j: (w_row(j), 0))
        wl_spec = pl.BlockSpec((V7X_LANES, D_MODEL), lambda i, j: (IN_OFFSETS[4] // V7X_LANES, 0))
        assert m == tm, "every weight tile must be visited exactly once"
    else:
        w_spec = pl.BlockSpec(memory_space=pl.ANY)
        wl_spec = pl.BlockSpec((V7X_LANES, D_MODEL), lambda i, j: (0, 0))
        assert (m // tm) * n_j >= W_RING_SLOTS - 1
    cast_in, cast_out, cast_shape = _cast_specs(cast, N_CAST_BLOCKS, lambda i, j: i * n_j + j)
    assert not cast or (m // tm) * n_j >= N_CAST_BLOCKS
    out_specs = [
        pl.BlockSpec((tm, tn), lambda i, j: (i, j)),
        pl.BlockSpec((tm, V7X_LANES), lambda i, j: (i, 0)),
    ]
    out_shape = [
        jax.ShapeDtypeStruct((m, Z_WIDTH), BF16),
        jax.ShapeDtypeStruct((m, V7X_LANES), F32),
    ]
    if emit:
        out_specs += [pl.BlockSpec((tn, D_MODEL), lambda i, j: (j, 0)),
                      pl.BlockSpec((V7X_LANES, D_MODEL), lambda i, j: (0, 0))]
        out_shape += [jax.ShapeDtypeStruct((Z_WIDTH, D_MODEL), BF16),
                      jax.ShapeDtypeStruct((V7X_LANES, D_MODEL), BF16)]
    x_spec = pl.BlockSpec((tm, D_MODEL), lambda i, j: (i, 0)) if emit else pl.BlockSpec(memory_space=pl.ANY)
    ring = [] if emit else [pltpu.VMEM((W_RING_SLOTS, tn, D_MODEL), BF16), pltpu.SemaphoreType.DMA((W_RING_SLOTS,)),
                            pltpu.VMEM((2, tm, D_MODEL), F32), pltpu.SemaphoreType.DMA((2,))]
    return pl.pallas_call(
        functools.partial(_inproj_kernel, emit=emit, n_cast=len(cast)),
        grid=(m // tm, n_j),
        in_specs=[
            x_spec,
            pl.BlockSpec((1, D_MODEL), lambda i, j: (0, 0)),
            w_spec,
            wl_spec,
        ] + cast_in,
        out_specs=out_specs + cast_out,
        out_shape=out_shape + cast_shape,
        scratch_shapes=[pltpu.VMEM((tm, D_MODEL), BF16), pltpu.VMEM((tm, 1), F32)] + ring,
        compiler_params=_params("parallel" if emit else "arbitrary", "arbitrary"),
        name="inproj",
    )(x, g_mix, w, wl, *_cast_args(cast))


def _group_mask(R, C):
    t = np.arange(R)[:, None]
    s = np.arange(R)[None, :]
    return ((t // C == s // C) & (s <= t)).astype(np.float32)


def _gla_tile(q_ref, k_ref, v_ref, r_ref, alow_ref, tri_ref, mask_ref, wa2_ref, ba_ref, gn_ref,
              o_ref, s_out_ref, b_scr, e_scr, *, C, s_scr=None, s0_ref=None):
    sequential = s_scr is not None
    R = q_ref.shape[0]
    n = R // C
    mid = C // 2
    W = GLA_KEY_WIDTH

    x = _dot(alow_ref[...].astype(BF16), wa2_ref[...].astype(BF16)) + ba_ref[...]
    lb = (jnp.minimum(x, 0.0) - jnp.log(1.0 + jnp.exp(-jnp.abs(x)))) * (1.0 / GLA_TAU)

    tri = tri_ref[...]
    lb_hi = lb.astype(BF16)
    lb_lo = (lb - lb_hi.astype(F32)).astype(BF16)
    b = _dot(tri, lb_hi) + _dot(tri, lb_lo)
    b3 = b.reshape(n, C, W)
    b_mid = jnp.broadcast_to(b3[:, mid:mid + 1, :], (n, C, W)).reshape(R, W)
    b_last = jnp.broadcast_to(b3[:, C - 1:C, :], (n, C, W)).reshape(R, W)

    q = q_ref[...].astype(F32) * (GLA_DK ** -0.5)
    k = k_ref[...].astype(F32)
    q1 = (q * jnp.exp(b)).astype(BF16)
    q2 = (q * jnp.exp(b - b_mid)).astype(BF16)
    k2 = (k * jnp.exp(b_mid - b)).astype(BF16)
    k3 = k * jnp.exp(b_last - b)

    e_scr[...] = jnp.zeros((V7X_LANES, GLA_DK), F32)
    for h in range(GLA_HEADS):
        b_scr[h] = b[:, h * GLA_DK:(h + 1) * GLA_DK]
        e_scr[h * n:(h + 1) * n, :] = jnp.exp(b_scr[h, pl.ds(C - 1, n, stride=C), :])
    d_cols = e_scr[...].T

    causal = mask_ref[...] > 0.0
    if not sequential:
        grp = lax.broadcasted_iota(jnp.int32, (R, 1), 0) >> (C.bit_length() - 1)

    for h in range(GLA_HEADS):
        ks = slice(h * GLA_DK, (h + 1) * GLA_DK)
        vs = slice(h * GLA_DV, (h + 1) * GLA_DV)
        v = v_ref[:, vs]
        att = jnp.where(causal, _dot_nt(q2[:, ks], k2[:, ks]), 0.0).astype(BF16)
        o = _dot(att, v)
        q1h = q1[:, ks]
        k3h = k3[:, ks]
        if sequential:
            s = s_scr[h]
            o_parts = []
            for g in range(n):
                rows = slice(g * C, (g + 1) * C)
                o_parts.append(o[rows] + _dot(q1h[rows], s.astype(BF16)))
                s = d_cols[:, h * n + g:h * n + g + 1] * s + _dot_tn(k3h[rows].astype(BF16), v[rows])
            s_scr[h] = s
            s_out_ref[0, h] = s
            o = jnp.concatenate(o_parts, axis=0)
        else:
            for g in range(n):
                s = s0_ref[g, h]
                o = o + jnp.where(grp == g, _dot(q1h, s.astype(BF16)), 0.0)
                k3g = jnp.where(grp == g, k3h, 0.0).astype(BF16)
                s_out_ref[g, h] = d_cols[:, h * n + g:h * n + g + 1] * s + _dot_tn(k3g, v)
        o = _rms(o, gn_ref[:, vs]) * jax.nn.silu(r_ref[:, vs].astype(F32))
        o_ref[:, vs] = o.astype(BF16)


N_GLA_TILE_IN = 7


def _gla_kernel(*refs, C_seq, C_step, n_cast):
    n = N_GLA_TILE_IN
    seq_in, step_in = refs[:n], refs[n:2 * n]
    s0_ref, wa2_ref, ba_ref, gn_ref = refs[2 * n:2 * n + 4]
    cast_in = refs[2 * n + 4:2 * n + 4 + n_cast]
    outs = refs[2 * n + 4 + n_cast:]
    o_seq, s_seq, o_step, s_step = outs[:4]
    cast_out = outs[4:4 + n_cast]
    s_scr, b_seq, e_seq, b_step, e_step = outs[4 + n_cast:]
    shared = (wa2_ref, ba_ref, gn_ref)

    @pl.when(pl.program_id(1) == 0)
    def _():
        s_scr[...] = jnp.zeros((GLA_HEADS, GLA_DK, GLA_DV), F32)

    _cast_blocks(cast_in, cast_out)
    _gla_tile(*seq_in, *shared, o_seq, s_seq, b_seq, e_seq, C=C_seq, s_scr=s_scr)
    _gla_tile(*step_in, *shared, o_step, s_step, b_step, e_step, C=C_step, s0_ref=s0_ref)


def _gla(zp, alow_p, zs, alow_s, s0, wa2_pad, b_a, g_norm, *, n_seq_p, len_p, n_seq_s, len_s, cast=()):
    R_p = GLA_SEQ_ROWS
    nt = len_p // R_p
    n_steps = n_seq_p * nt
    R_s = zs.shape[0] // n_steps
    assert R_s * n_steps == zs.shape[0] and R_s % len_s == 0 and R_s % V7X_BF16_ROWS == 0
    step_of = lambda a, c: a * nt + c
    const = lambda a, c: (0, 0)

    def tile_specs(R):
        return [
            pl.BlockSpec((R, GLA_KEY_WIDTH), lambda a, c: (step_of(a, c), 0)),
            pl.BlockSpec((R, GLA_KEY_WIDTH), lambda a, c: (step_of(a, c), 1)),
            pl.BlockSpec((R, GLA_WIDTH), lambda a, c: (step_of(a, c), 1)),
            pl.BlockSpec((R, GLA_WIDTH), lambda a, c: (step_of(a, c), 2)),
            pl.BlockSpec((R, V7X_LANES), lambda a, c: (step_of(a, c), 0)),
            pl.BlockSpec((R, R), const),
            pl.BlockSpec((R, R), const),
        ]

    def tile_args(z, alow, R, C):
        mask = _group_mask(R, C)
        return [z, z, z, z, alow, jnp.asarray(mask, BF16), jnp.asarray(mask, F32)]

    s_block_p = (1, GLA_HEADS, GLA_DK, GLA_DV)
    s_block_s = (R_s // len_s, GLA_HEADS, GLA_DK, GLA_DV)
    state_p = pl.BlockSpec(s_block_p, lambda a, c: (a, 0, 0, 0))
    state_s = pl.BlockSpec(s_block_s, lambda a, c: (step_of(a, c), 0, 0, 0))
    cast_in, cast_out, cast_shape = _cast_specs(cast, n_steps, step_of)
    scratch = [pltpu.VMEM((GLA_HEADS, GLA_DK, GLA_DV), F32)]
    for R in (R_p, R_s):
        scratch += [pltpu.VMEM((GLA_HEADS, R, GLA_DK), F32), pltpu.VMEM((V7X_LANES, GLA_DK), F32)]
    return pl.pallas_call(
        functools.partial(_gla_kernel, C_seq=GLA_CHUNK, C_step=len_s, n_cast=len(cast)),
        grid=(n_seq_p, nt),
        in_specs=tile_specs(R_p) + tile_specs(R_s) + [
            state_s,
            pl.BlockSpec((V7X_LANES, GLA_KEY_WIDTH), const),
            pl.BlockSpec((1, GLA_KEY_WIDTH), const),
            pl.BlockSpec((1, GLA_WIDTH), const),
        ] + cast_in,
        out_specs=[
            pl.BlockSpec((R_p, GLA_WIDTH), lambda a, c: (step_of(a, c), 0)),
            state_p,
            pl.BlockSpec((R_s, GLA_WIDTH), lambda a, c: (step_of(a, c), 0)),
            state_s,
        ] + cast_out,
        out_shape=[
            jax.ShapeDtypeStruct((zp.shape[0], GLA_WIDTH), BF16),
            jax.ShapeDtypeStruct((n_seq_p, GLA_HEADS, GLA_DK, GLA_DV), F32),
            jax.ShapeDtypeStruct((zs.shape[0], GLA_WIDTH), BF16),
            jax.ShapeDtypeStruct((n_seq_s, GLA_HEADS, GLA_DK, GLA_DV), F32),
        ] + cast_shape,
        scratch_shapes=scratch,
        compiler_params=_params("parallel", "arbitrary"),
        name="gla",
    )(*tile_args(zp, alow_p, R_p, GLA_CHUNK), *tile_args(zs, alow_s, R_s, len_s), s0, wa2_pad, b_a, g_norm,
      *_cast_args(cast))


def _mixout_tile(h_ref, o_ref, u_ref, vg_ref, gln_ref, bln_ref, ws_ref, bias_ref, gout_ref, wo_ref, wm_ref,
                 h1_ref, vn_ref, acc_scr, *, tm, blk):
    n_chunks = tm // GMLP_CHUNK
    col_w = D_MODEL // n_chunks
    shift = blk.bit_length() - 1
    row = lax.broadcasted_iota(jnp.int32, (GMLP_CHUNK, GMLP_CHUNK), 0)
    col = lax.broadcasted_iota(jnp.int32, (GMLP_CHUNK, GMLP_CHUNK), 1)
    causal = jnp.logical_and((row >> shift) == (col >> shift), col <= row)
    w = [jnp.where(causal, ws_ref[g], 0.0).astype(BF16) for g in range(GMLP_GROUPS)]
    m_parts = []
    for c in range(n_chunks):
        cols = slice(c * col_w, (c + 1) * col_w)
        acc_scr[:, cols] = h_ref[:, cols] + _dot(o_ref[...], wo_ref[:, cols])
        rows = pl.ds(c * GMLP_CHUNK, GMLP_CHUNK)
        vg = jax.nn.gelu(vg_ref[rows, :].astype(F32))
        mu = jnp.mean(vg, axis=-1, keepdims=True)
        vc = vg - mu
        vn = vc * lax.rsqrt(jnp.mean(vc * vc, axis=-1, keepdims=True) + EPS) * gln_ref[...] + bln_ref[...]
        if vn_ref is not None:
            vn_ref[rows, :] = vn
        vn16 = vn.astype(BF16)
        mixed = jnp.concatenate(
            [_dot(w[g], vn16[:, g * GMLP_GROUP_DIM:(g + 1) * GMLP_GROUP_DIM]) for g in range(GMLP_GROUPS)],
            axis=1) + bias_ref[...]
        mm = jax.nn.gelu(u_ref[rows, :].astype(F32)) * mixed
        m_parts.append(_rms(mm, gout_ref[...]).astype(BF16))
    h1_ref[...] = acc_scr[...] + _dot(jnp.concatenate(m_parts, axis=0), wm_ref[...])


def _mixout_kernel(hp_ref, hs_ref, op_ref, os_ref, up_ref, vgp_ref, us_ref, vgs_ref, gln_ref, bln_ref,
                   wsp_ref, biasp_ref, wss_ref, biass_ref, gout_ref, wo_ref, wm_ref, h1_ref, vn_ref, acc_scr,
                   *, tm, n_p, blk_p, blk_s):
    shared = (gln_ref, bln_ref)
    tail = (gout_ref, wo_ref, wm_ref, h1_ref)
    i = pl.program_id(0)

    @pl.when(i < n_p)
    def _():
        _mixout_tile(hp_ref, op_ref, up_ref, vgp_ref, *shared, wsp_ref, biasp_ref, *tail, None, acc_scr, tm=tm,
                     blk=blk_p)

    @pl.when(i >= n_p)
    def _():
        _mixout_tile(hs_ref, os_ref, us_ref, vgs_ref, *shared, wss_ref, biass_ref, *tail, vn_ref, acc_scr, tm=tm,
                     blk=blk_s)


def _gmlp_operands(w_s, b_s, blk):
    reps = GMLP_CHUNK // blk
    sel = jnp.asarray(np.tile(np.eye(blk, dtype=np.float32), (reps, 1)))
    w_mix = jnp.einsum("ra,gab,cb->grc", sel, w_s[:, :blk, :blk], sel, precision=lax.Precision.HIGHEST)
    bias = jnp.tile(jnp.repeat(b_s[:, :blk].T, GMLP_GROUP_DIM, axis=1), (reps, 1))
    return w_mix, bias


def _mixout(hp, hs, op, os_, zp, zs, g_ln, b_ln, w_s, b_s, g_out, w_out, *, tm, len_p, len_s):
    n_p, n_s = hp.shape[0] // tm, hs.shape[0] // tm
    blk_p, blk_s = min(len_p, GMLP_CHUNK), min(len_s, GMLP_CHUNK)
    pidx = lambda i: jnp.minimum(i, n_p - 1)
    sidx = lambda i: jnp.maximum(i - n_p, 0)
    once = pl.Buffered(1)
    vec = pl.BlockSpec((1, GMLP_WIDTH), lambda i: (0, 0))
    mixw = pl.BlockSpec((GMLP_GROUPS, GMLP_CHUNK, GMLP_CHUNK), lambda i: (0, 0, 0))
    biasb = pl.BlockSpec((GMLP_CHUNK, GMLP_WIDTH), lambda i: (0, 0))
    return pl.pallas_call(
        functools.partial(_mixout_kernel, tm=tm, n_p=n_p, blk_p=blk_p, blk_s=blk_s),
        grid=(n_p + n_s,),
        in_specs=[
            pl.BlockSpec((tm, D_MODEL), lambda i: (pidx(i), 0)),
            pl.BlockSpec((tm, D_MODEL), lambda i: (sidx(i), 0)),
            pl.BlockSpec((tm, GLA_WIDTH), lambda i: (pidx(i), 0)),
            pl.BlockSpec((tm, GLA_WIDTH), lambda i: (sidx(i), 0)),
            pl.BlockSpec((tm, GMLP_WIDTH), lambda i: (pidx(i), Z_U_OFF // GMLP_WIDTH)),
            pl.BlockSpec((tm, GMLP_WIDTH), lambda i: (pidx(i), Z_VG_OFF // GMLP_WIDTH)),
            pl.BlockSpec((tm, GMLP_WIDTH), lambda i: (sidx(i), Z_U_OFF // GMLP_WIDTH)),
            pl.BlockSpec((tm, GMLP_WIDTH), lambda i: (sidx(i), Z_VG_OFF // GMLP_WIDTH)),
            vec, vec,
            mixw, biasb, mixw, biasb,
            vec,
            pl.BlockSpec((GLA_WIDTH, D_MODEL), lambda i: (0, 0), pipeline_mode=once),
            pl.BlockSpec((GMLP_WIDTH, D_MODEL), lambda i: (1, 0), pipeline_mode=once),
        ],
        out_specs=[
            pl.BlockSpec((tm, D_MODEL), lambda i: (i, 0)),
            pl.BlockSpec((tm, GMLP_WIDTH), lambda i: (sidx(i), 0)),
        ],
        out_shape=[
            jax.ShapeDtypeStruct((hp.shape[0] + hs.shape[0], D_MODEL), F32),
            jax.ShapeDtypeStruct((hs.shape[0], GMLP_WIDTH), F32),
        ],
        scratch_shapes=[pltpu.VMEM((tm, D_MODEL), F32)],
        compiler_params=_params("arbitrary"),
        name="mixout",
    )(hp, hs, op, os_, zp, zp, zs, zs, g_ln, b_ln, *_gmlp_operands(w_s, b_s, blk_p), *_gmlp_operands(w_s, b_s, blk_s),
      g_out, w_out, w_out)


def _ffn_kernel(h_ref, g_ref, wg_ref, wu_ref, wo_ref, o_ref, hb_scr, r_scr):
    j = pl.program_id(1)

    def tile(hb, r):
        gate = r * _dot(hb, wg_ref[...])
        up = r * _dot(hb, wu_ref[...])
        return _dot((jax.nn.silu(gate) * up).astype(BF16), wo_ref[...])

    @pl.when(j == 0)
    def _():
        h = h_ref[...]
        hb = (h * g_ref[...]).astype(BF16)
        hb_scr[...] = hb
        r = lax.rsqrt(jnp.mean(h * h, axis=-1, keepdims=True) + EPS)
        r_scr[...] = r
        o_ref[...] = h + tile(hb, r)

    @pl.when(j > 0)
    def _():
        o_ref[...] += tile(hb_scr[...], r_scr[...])


def _ffn(h1, g_ffn, w_gate, w_up, w_out, tm, tf):
    m = h1.shape[0]
    return pl.pallas_call(
        _ffn_kernel,
        grid=(m // tm, D_FF // tf),
        in_specs=[
            pl.BlockSpec((tm, D_MODEL), lambda i, j: (i, 0)),
            pl.BlockSpec((1, D_MODEL), lambda i, j: (0, 0)),
            pl.BlockSpec((D_MODEL, tf), lambda i, j: (0, j)),
            pl.BlockSpec((D_MODEL, tf), lambda i, j: (0, j)),
            pl.BlockSpec((tf, D_MODEL), lambda i, j: (j, 0)),
        ],
        out_specs=pl.BlockSpec((tm, D_MODEL), lambda i, j: (i, 0)),
        out_shape=jax.ShapeDtypeStruct((m, D_MODEL), F32),
        scratch_shapes=[pltpu.VMEM((tm, D_MODEL), BF16), pltpu.VMEM((tm, 1), F32)],
        compiler_params=_params("parallel", "arbitrary"),
        name="ffn",
    )(h1, g_ffn, w_gate, w_up, w_out)


def _ple_tile(h_ref, p_ref, wp_ref, gp_ref, gg_ref, wg_ref, gf_ref, y_ref, t_scr):
    h = h_ref[...]
    r = lax.rsqrt(jnp.mean(h * h, axis=-1, keepdims=True) + EPS)
    hb = (h * gg_ref[...]).astype(BF16)
    pe = _rms(_dot(p_ref[...].astype(BF16), wp_ref[...]), gp_ref[...])
    ssq = jnp.zeros((h.shape[0], 1), F32)
    for c in range(D_MODEL // PLE_COL_CHUNK):
        cols = slice(c * PLE_COL_CHUNK, (c + 1) * PLE_COL_CHUNK)
        t = h[:, cols] + pe[:, cols] * jax.nn.sigmoid(r * _dot(hb, wg_ref[:, cols]))
        ssq = ssq + jnp.sum(t * t, axis=-1, keepdims=True)
        t_scr[:, cols] = t
    y_ref[...] = t_scr[...] * lax.rsqrt(ssq / D_MODEL + EPS) * gf_ref[...]


def _ple_kernel(h_ref, pp_ref, ps_ref, wp_ref, gp_ref, gg_ref, wg_ref, gf_ref, yp_ref, ys_ref, t_scr, *, n_p):
    params = (wp_ref, gp_ref, gg_ref, wg_ref, gf_ref)
    i = pl.program_id(0)

    @pl.when(i < n_p)
    def _():
        _ple_tile(h_ref, pp_ref, *params, yp_ref, t_scr)

    @pl.when(i >= n_p)
    def _():
        _ple_tile(h_ref, ps_ref, *params, ys_ref, t_scr)


def _ple(h, pp, ps, w_ple, g_ple, g_gate, w_gate, g_final, tm):
    n_p, n_s = pp.shape[0] // tm, ps.shape[0] // tm
    pidx = lambda i: (jnp.minimum(i, n_p - 1), 0)
    sidx = lambda i: (jnp.maximum(i - n_p, 0), 0)
    vec = pl.BlockSpec((1, D_MODEL), lambda i: (0, 0))
    return pl.pallas_call(
        functools.partial(_ple_kernel, n_p=n_p),
        grid=(n_p + n_s,),
        in_specs=[
            pl.BlockSpec((tm, D_MODEL), lambda i: (i, 0)),
            pl.BlockSpec((tm, PLE_DIM), pidx),
            pl.BlockSpec((tm, PLE_DIM), sidx),
            pl.BlockSpec((PLE_DIM, D_MODEL), lambda i: (0, 0)),
            vec, vec,
            pl.BlockSpec((D_MODEL, D_MODEL), lambda i: (0, 0), pipeline_mode=pl.Buffered(1)),
            vec,
        ],
        out_specs=[pl.BlockSpec((tm, D_MODEL), pidx), pl.BlockSpec((tm, D_MODEL), sidx)],
        out_shape=[jax.ShapeDtypeStruct((pp.shape[0], D_MODEL), F32),
                   jax.ShapeDtypeStruct((ps.shape[0], D_MODEL), F32)],
        scratch_shapes=[pltpu.VMEM((tm, D_MODEL), F32)],
        compiler_params=_params("arbitrary"),
        name="ple",
    )(h, pp, ps, w_ple, g_ple, g_gate, w_gate, g_final)


def kernel(x_prompt, x_sample, state_gla, p_prompt, p_sample, g_mix, w_in, w_a2, b_a, g_gla_norm, g_gmlp_ln,
           b_gmlp_ln, w_s, b_s, g_gmlp_out, w_out, g_ffn, w_ffn_in, w_ffn_out, w_ple, g_ple, g_ple_gate,
           w_ple_gate, g_final):
    batch, seq, _ = x_prompt.shape
    dec_batch, dec_seq, _ = x_sample.shape
    depth = w_in.shape[0]
    assert depth == 1, "one layer: the prompt / sample passes below are not chained over depth"
    row = lambda a: a.reshape(1, -1)
    wts = dict(
        g_mix=row(g_mix[0]),
        wa2_pad=jnp.pad(w_a2[0], ((0, V7X_LANES - GLA_GATE_RANK), (0, 0))),
        b_a=row(b_a[0]),
        g_gla_norm=row(g_gla_norm[0]),
        g_gmlp_ln=row(g_gmlp_ln[0]),
        b_gmlp_ln=row(b_gmlp_ln[0]),
        w_s=w_s[0],
        b_s=b_s[0],
        g_gmlp_out=row(g_gmlp_out[0]),
        g_ffn=row(g_ffn[0]),
        w_ple=w_ple[0].astype(BF16),
        g_ple=row(g_ple[0]),
        g_ple_gate=row(g_ple_gate[0]),
        g_final=row(g_final),
    )
    xp = x_prompt.reshape(batch * seq, D_MODEL)
    xs = x_sample.reshape(dec_batch * dec_seq, D_MODEL)
    wt_in = jnp.swapaxes(w_in, 1, 2)[0]
    zs, alow_s, wt16, wl16 = _inproj(xs, wts["g_mix"], wt_in, wt_in, tm=xs.shape[0], tn=INPROJ_COLS, emit=True)
    zp, alow_p, wts["w_out"], wts["w_ffn_out"], w_gate16 = _inproj(
        xp, wts["g_mix"], wt16, wl16, tm=INPROJ_ROWS, tn=INPROJ_COLS_RING, emit=False,
        cast=(w_out[0], w_ffn_out[0], (w_ffn_in[0], 0, 2)))
    op, sp, os_, ss, w_up16, wts["w_ple_gate"] = _gla(
        zp, alow_p, zs, alow_s, state_gla[0], wts["wa2_pad"], wts["b_a"], wts["g_gla_norm"],
        n_seq_p=batch, len_p=seq, n_seq_s=dec_batch, len_s=dec_seq, cast=((w_ffn_in[0], 1, 2), w_ple_gate[0]))
    h1, vs = _mixout(xp, xs, op, os_, zp, zs, wts["g_gmlp_ln"], wts["b_gmlp_ln"], wts["w_s"], wts["b_s"],
                     wts["g_gmlp_out"], wts["w_out"], tm=MIX_ROWS, len_p=seq, len_s=dec_seq)
    h2 = _ffn(h1, wts["g_ffn"], w_gate16, w_up16, wts["w_ffn_out"], tm=FFN_ROWS, tf=FFN_COLS)
    yp, ys = _ple(h2, p_prompt[0].reshape(batch * seq, PLE_DIM), p_sample[0].reshape(dec_batch * dec_seq, PLE_DIM),
                  wts["w_ple"], wts["g_ple"], wts["g_ple_gate"], wts["w_ple_gate"], wts["g_final"], tm=PLE_ROWS)
    return (yp.reshape(batch, seq, D_MODEL), ys.reshape(dec_batch, dec_seq, D_MODEL), sp[None], ss[None],
            vs.reshape(1, dec_batch, dec_seq, GMLP_WIDTH))
```
